```python
import jax
import jax.numpy as jnp
from jax import lax
import numpy as np

D_MODEL = 1024
BATCH = 8
SEQ = 4096
DEPTH = 2
DEC_BATCH = 32
DEC_SEQ = 32
PAST_LEN = 1024

CHUNK = 64
N_MIXERS = 2
N_ATTN_LAYERS = (DEPTH + 1) // 2
N_DN_LAYERS = DEPTH // 2
EPS = 1e-6
F32 = jnp.float32

WINDOW = 128
WIN_CHUNKS = WINDOW // CHUNK
ATTN_CACHE = WINDOW
N_HEADS = 16
N_KV_HEADS = 4
HEAD_DIM = 64
GQA_GROUP = N_HEADS // N_KV_HEADS
ATTN_WIDTH = N_HEADS * HEAD_DIM
KV_WIDTH = N_KV_HEADS * HEAD_DIM
ATTN_IN = 2 * ATTN_WIDTH + 2 * KV_WIDTH

DN_HEADS = 8
DN_KEY_DIM = 128
DN_VAL_DIM = 128
DN_QK_WIDTH = DN_HEADS * DN_KEY_DIM
DN_V_WIDTH = DN_HEADS * DN_VAL_DIM
DN_CONV_DIM = 2 * DN_QK_WIDTH + DN_V_WIDTH
CONV_WIDTH = 4
DN_IN = DN_CONV_DIM + DN_V_WIDTH + 2 * DN_HEADS

kernel_name = 'chunk_causal_swa_sink_gated_deltanet_hybrid_step'


def rmsnorm(x, g):
    xf = x.astype(F32)
    y = xf * lax.rsqrt(jnp.mean(xf * xf, axis=-1, keepdims=True) + EPS) * g.astype(F32)
    return y.astype(x.dtype)


def l2norm(x):
    return x * lax.rsqrt(jnp.sum(x * x, axis=-1, keepdims=True) + EPS)


def alibi_slopes():
    return 2.0 ** (-8.0 * jnp.arange(1, N_HEADS + 1, dtype=F32) / N_HEADS)


def sink_attention(q, k, v, q_pos, k_pos, k_valid, sinks):
    s = jnp.einsum('bnqkgd,bnskd->bnkgqs', q.astype(F32), k.astype(F32)) * (HEAD_DIM ** -0.5)
    slopes = alibi_slopes().reshape(N_KV_HEADS, GQA_GROUP, 1, 1)
    dist = jnp.abs(q_pos[:, :, None] - k_pos[:, None, :])
    s = s - slopes * dist[:, None, None]
    s = jnp.where(k_valid[:, None, None, None, :], s, -jnp.inf)
    sink = sinks.astype(F32).reshape(N_KV_HEADS, GQA_GROUP, 1, 1)
    m = jnp.maximum(jnp.max(s, axis=-1, keepdims=True), sink)
    p = jnp.exp(s - m)
    den = jnp.sum(p, axis=-1, keepdims=True) + jnp.exp(sink - m)
    return jnp.einsum('bnkgqs,bnskd->bnqkgd', p / den, v.astype(F32))


def attn_project(h, w_in):
    B, L, _ = h.shape
    q, k, v, gate = jnp.split(h @ w_in, [ATTN_WIDTH, ATTN_WIDTH + KV_WIDTH, ATTN_WIDTH + 2 * KV_WIDTH], axis=-1)
    return (q.reshape(B, L, N_KV_HEADS, GQA_GROUP, HEAD_DIM),
            k.reshape(B, L, N_KV_HEADS, HEAD_DIM),
            v.reshape(B, L, N_KV_HEADS, HEAD_DIM), gate)


def attn_prompt(h, w_in, sinks, w_out):
    B, L, _ = h.shape
    nc = L // CHUNK
    q, k, v, gate = attn_project(h, w_in)

    def band(t):
        tc = t.reshape(B, nc, CHUNK, N_KV_HEADS, HEAD_DIM)
        tp = jnp.pad(tc, ((0, 0), (WIN_CHUNKS, 0), (0, 0), (0, 0), (0, 0)))
        return jnp.concatenate([tp[:, j:j + nc] for j in range(WIN_CHUNKS + 1)], axis=2)

    qb = q.reshape(B, nc, CHUNK, N_KV_HEADS, GQA_GROUP, HEAD_DIM)
    q_pos = (jnp.arange(nc)[:, None] * CHUNK + jnp.arange(CHUNK)[None, :]).astype(F32)
    k_pos = (jnp.arange(nc)[:, None] - WIN_CHUNKS) * CHUNK + jnp.arange((WIN_CHUNKS + 1) * CHUNK)[None, :]
    o = sink_attention(qb, band(k), band(v), q_pos, k_pos.astype(F32), k_pos >= 0, sinks)
    o = o.reshape(B, L, ATTN_WIDTH).astype(h.dtype)
    y = (o * jax.nn.silu(gate)) @ w_out
    return y, k[:, L - ATTN_CACHE:], v[:, L - ATTN_CACHE:]


def attn_sample(h, cache_k, cache_v, w_in, sinks, w_out):
    B, T, _ = h.shape
    C = cache_k.shape[1]
    q, k, v, gate = attn_project(h, w_in)
    k_all = jnp.concatenate([cache_k.astype(k.dtype), k], axis=1)
    v_all = jnp.concatenate([cache_v.astype(v.dtype), v], axis=1)
    q_pos = (PAST_LEN + jnp.arange(T))[None, :].astype(F32)
    k_pos = (PAST_LEN - C + jnp.arange(C + T))[None, :].astype(F32)
    valid = jnp.ones((1, C + T), dtype=bool)
    o = sink_attention(q[:, None], k_all[:, None], v_all[:, None], q_pos, k_pos, valid, sinks)
    o = o.reshape(B, T, ATTN_WIDTH).astype(h.dtype)
    y = (o * jax.nn.silu(gate)) @ w_out
    return y, k_all[:, T:], v_all[:, T:]


def causal_conv(x, hist, w):
    L = x.shape[1]
    xp = jnp.concatenate([hist.astype(x.dtype), x], axis=1)
    y = sum(xp[:, j:j + L] * w[j] for j in range(CONV_WIDTH))
    return jax.nn.silu(y), xp[:, L:]


def gated_delta_chunked(q, k, v, g, beta, s0, chunk):
    B, L, H, dk = q.shape
    dv = v.shape[-1]
    n = L // chunk
    blk4 = lambda t: t.reshape(B, n, chunk, H, t.shape[-1]).transpose(0, 1, 3, 2, 4)
    q, k, v = blk4(q), blk4(k), blk4(v)
    g = g.reshape(B, n, chunk, H).transpose(0, 1, 3, 2)
    beta = beta.reshape(B, n, chunk, H).transpose(0, 1, 3, 2)
    gc = jnp.cumsum(g, axis=-1)
    tri = jnp.tril(jnp.ones((chunk, chunk), dtype=bool))
    strict = jnp.tril(jnp.ones((chunk, chunk), dtype=bool), -1)
    decay = jnp.exp(jnp.where(tri, gc[..., :, None] - gc[..., None, :], -jnp.inf))
    kb = k * beta[..., None]
    vb = v * beta[..., None]
    m = jnp.where(strict, jnp.einsum('bnhid,bnhjd->bnhij', kb, k) * decay, 0.0)
    a = m + jnp.eye(chunk, dtype=F32)
    rhs = jnp.concatenate([vb, kb * jnp.exp(gc)[..., None]], axis=-1)
    sol = lax.linalg.triangular_solve(a, rhs, left_side=True, lower=True, unit_diagonal=True)
    u, w = sol[..., :dv], sol[..., dv:]
    intra = jnp.where(tri, jnp.einsum('bnhid,bnhjd->bnhij', q, k) * decay, 0.0)

    def step(s, xs):
        qi, ki, ui, wi, gi, ai = xs
        v_new = ui - jnp.einsum('bhcd,bhde->bhce', wi, s)
        o = jnp.einsum('bhcd,bhde->bhce', qi * jnp.exp(gi)[..., None], s) + jnp.einsum('bhij,bhje->bhie', ai, v_new)
        g_last = gi[..., -1]
        s = s * jnp.exp(g_last)[..., None, None] + jnp.einsum(
            'bhcd,bhce->bhde', ki * jnp.exp(g_last[..., None] - gi)[..., None], v_new)
        return s, o

    xs = tuple(jnp.moveaxis(t, 1, 0) for t in (q, k, u, w, gc, intra))
    s_fin, o = lax.scan(step, s0, xs)
    o = o.transpose(1, 0, 3, 2, 4).reshape(B, L, H, dv)
    return o, s_fin


def deltanet_branch(h, conv_hist, s0, w_in, conv_w, a_log, dt_bias, norm_g, w_out):
    B, L, _ = h.shape
    qkv, z, b, a = jnp.split(h @ w_in, [DN_CONV_DIM, DN_CONV_DIM + DN_V_WIDTH, DN_CONV_DIM + DN_V_WIDTH + DN_HEADS], axis=-1)
    qkv, new_hist = causal_conv(qkv, conv_hist, conv_w)
    q, k, v = jnp.split(qkv.astype(F32), [DN_QK_WIDTH, 2 * DN_QK_WIDTH], axis=-1)
    q = l2norm(q.reshape(B, L, DN_HEADS, DN_KEY_DIM)) * (DN_KEY_DIM ** -0.5)
    k = l2norm(k.reshape(B, L, DN_HEADS, DN_KEY_DIM))
    v = v.reshape(B, L, DN_HEADS, DN_VAL_DIM)
    beta = jax.nn.sigmoid(b.astype(F32))
    g = -jnp.exp(a_log.astype(F32)) * jax.nn.softplus(a.astype(F32) + dt_bias.astype(F32))
    o, s_new = gated_delta_chunked(q, k, v, g, beta, s0.astype(F32), min(CHUNK, L))
    o = rmsnorm(o, norm_g) * jax.nn.silu(z.astype(F32).reshape(B, L, DN_HEADS, DN_VAL_DIM))
    o = o.reshape(B, L, DN_V_WIDTH).astype(h.dtype)
    return o @ w_out, new_hist, s_new.astype(s0.dtype)


def setup_inputs(seed: int = 0) -> dict:
    key = jax.random.key(seed)
    ks = jax.random.split(key, 20)
    nrm = lambda k, shape, scale: scale * jax.random.normal(k, shape, F32)
    dt = jnp.exp(jax.random.uniform(ks[15], (N_DN_LAYERS, DN_HEADS), F32, np.log(1e-3), np.log(1e-1)))
    return {
        'x_prompt': nrm(ks[0], (BATCH, SEQ, D_MODEL), 1.0),
        'x_sample': nrm(ks[1], (DEC_BATCH, DEC_SEQ, D_MODEL), 1.0),
        'cache_k': nrm(ks[2], (N_ATTN_LAYERS, DEC_BATCH, min(WINDOW, PAST_LEN), N_KV_HEADS, HEAD_DIM), 1.0),
        'cache_v': nrm(ks[3], (N_ATTN_LAYERS, DEC_BATCH, min(WINDOW, PAST_LEN), N_KV_HEADS, HEAD_DIM), 1.0),
        'state_conv': nrm(ks[4], (N_DN_LAYERS, DEC_BATCH, CONV_WIDTH - 1, DN_CONV_DIM), 1.0),
        'state_ssm': nrm(ks[5], (N_DN_LAYERS, DEC_BATCH, DN_HEADS, DN_KEY_DIM, DN_VAL_DIM), 0.1),
        'norm_g': 1.0 + nrm(ks[6], (DEPTH, D_MODEL), 0.02),
        'final_norm_g': 1.0 + nrm(ks[7], (D_MODEL,), 0.02),
        'attn_w_in': nrm(ks[8], (N_ATTN_LAYERS, D_MODEL, ATTN_IN), D_MODEL ** -0.5),
        'attn_sinks': nrm(ks[9], (N_ATTN_LAYERS, N_HEADS), 0.5),
        'attn_w_out': nrm(ks[10], (N_ATTN_LAYERS, ATTN_WIDTH, D_MODEL), ATTN_WIDTH ** -0.5),
        'dn_w_in': nrm(ks[11], (N_DN_LAYERS, D_MODEL, DN_IN), D_MODEL ** -0.5),
        'dn_conv_w': nrm(ks[12], (N_DN_LAYERS, CONV_WIDTH, DN_CONV_DIM), CONV_WIDTH ** -0.5),
        'dn_a_log': jnp.log(jax.random.uniform(ks[13], (N_DN_LAYERS, DN_HEADS), F32, 1.0, 16.0)),
        'dn_dt_bias': dt + jnp.log(-jnp.expm1(-dt)),
        'dn_norm_g': 1.0 + nrm(ks[14], (N_DN_LAYERS, DN_VAL_DIM), 0.02),
        'dn_w_out': nrm(ks[16], (N_DN_LAYERS, DN_V_WIDTH, D_MODEL), DN_V_WIDTH ** -0.5),
    }


def reference(x_prompt, x_sample, cache_k, cache_v, state_conv, state_ssm, norm_g, final_norm_g,
              attn_w_in, attn_sinks, attn_w_out, dn_w_in, dn_conv_w, dn_a_log, dn_dt_bias,
              dn_norm_g, dn_w_out):
    xp, xs = x_prompt, x_sample
    kp_l, vp_l, cp_l, sp_l = [], [], [], []
    ks_l, vs_l, cs_l, ss_l = [], [], [], []
    for i in range(DEPTH):
        hp = rmsnorm(xp, norm_g[i])
        hs = rmsnorm(xs, norm_g[i])
        j = i // N_MIXERS
        if i % N_MIXERS == 0:
            yp, kp, vp = attn_prompt(hp, attn_w_in[j], attn_sinks[j], attn_w_out[j])
            ys, kd, vd = attn_sample(hs, cache_k[j], cache_v[j], attn_w_in[j], attn_sinks[j], attn_w_out[j])
            kp_l.append(kp); vp_l.append(vp); ks_l.append(kd); vs_l.append(vd)
        else:
            hist0 = jnp.zeros((hp.shape[0], CONV_WIDTH - 1, DN_CONV_DIM), hp.dtype)
            s00 = jnp.zeros((hp.shape[0], DN_HEADS, DN_KEY_DIM, DN_VAL_DIM), state_ssm.dtype)
            yp, cp, sp = deltanet_branch(hp, hist0, s00, dn_w_in[j], dn_conv_w[j], dn_a_log[j],
                                         dn_dt_bias[j], dn_norm_g[j], dn_w_out[j])
            ys, cd, sd = deltanet_branch(hs, state_conv[j], state_ssm[j], dn_w_in[j], dn_conv_w[j],
                                         dn_a_log[j], dn_dt_bias[j], dn_norm_g[j], dn_w_out[j])
            cp_l.append(cp); sp_l.append(sp); cs_l.append(cd); ss_l.append(sd)
        xp = xp + yp
        xs = xs + ys
    y_prompt = rmsnorm(xp, final_norm_g)
    y_sample = rmsnorm(xs, final_norm_g)
    return (y_prompt, y_sample,
            jnp.stack(kp_l), jnp.stack(vp_l), jnp.stack(cp_l), jnp.stack(sp_l),
            jnp.stack(ks_l), jnp.stack(vs_l), jnp.stack(cs_l), jnp.stack(ss_l))
```

```python
import functools
import math

import numpy as np
import jax
import jax.numpy as jnp
from jax import lax
from jax.experimental import pallas as pl
from jax.experimental.pallas import tpu as pltpu

F32 = jnp.float32
BF16 = jnp.bfloat16

D_MODEL = 1024
EPS = 1e-6
CHUNK = 64
WINDOW = 128
N_HEADS = 16
N_KV_HEADS = 4
HEAD_DIM = 64
DN_HEADS = 8
DN_DIM = 128
CONV_WIDTH = 4
CONV_PAD = 8
LANES = 128
MXU_DIM = 256
VMEM_LIMIT_BYTES = 52 * 1024 * 1024

_NT = (((1,), (1,)), ((), ()))
_TN = (((0,), (0,)), ((), ()))


def _dot(a, b):
    return jnp.dot(a, b, preferred_element_type=F32)


def _dot_nt(a, b):
    return lax.dot_general(a, b, _NT, preferred_element_type=F32)


def _dot_tn(a, b):
    return lax.dot_general(a, b, _TN, preferred_element_type=F32)


def _sigmoid(x):
    return 1.0 / (1.0 + jnp.exp(-x))


def _rms_scale(x):
    return lax.rsqrt(jnp.mean(x * x, axis=-1, keepdims=True) + EPS)


def _attn_kernel(*refs, ns, tls, chunk, use_cache):
    if use_cache:
        (x_ref, g_ref, win_ref, wout_ref, sink_ref, bias_ref, ck_ref, cv_ref,
         x1_ref, kout_ref, vout_ref,
         q_s, gate_s, klo_s, khi_s, v_s, og_s) = refs
    else:
        (x_ref, g_ref, win_ref, wout_ref, sink_ref, bias_ref,
         x1_ref, kout_ref, vout_ref,
         q_s, gate_s, klo_s, khi_s, v_s, og_s) = refs
    t = pl.program_id(1)
    rows = ns * tls
    keys = WINDOW + chunk
    kvw = 2 * N_KV_HEADS * HEAD_DIM

    lane_lo = lax.broadcasted_iota(jnp.int32, (1, kvw), 1) % LANES < HEAD_DIM

    x = x_ref[...].reshape(rows, D_MODEL)
    h = (x * _rms_scale(x) * g_ref[...]).astype(BF16)
    q_s[...] = _dot(h, win_ref[:, 0:1024]).astype(BF16)
    kd = _dot(h, win_ref[:, 1024:1024 + kvw])
    vd = _dot(h, win_ref[:, 1024 + kvw:1024 + 2 * kvw])
    gate_s[...] = _dot(h, win_ref[:, 1024 + 2 * kvw:])

    if use_cache:
        ck = ck_ref[...]
        cv = cv_ref[...]
        klo_s[:, 0:WINDOW, :] = jnp.where(lane_lo, ck, 0.0).astype(BF16)
        khi_s[:, 0:WINDOW, :] = jnp.where(lane_lo, 0.0, ck).astype(BF16)
        v_s[:, 0:WINDOW, :] = cv.astype(BF16)
    else:
        @pl.when(t == 0)
        def _():
            zeros = jnp.zeros((ns, WINDOW, kvw), BF16)
            klo_s[:, 0:WINDOW, :] = zeros
            khi_s[:, 0:WINDOW, :] = zeros
            v_s[:, 0:WINDOW, :] = zeros

    kd3 = kd.reshape(ns, tls, kvw)
    klo_s[:, WINDOW:, :] = jnp.where(lane_lo, kd3, 0.0).astype(BF16)
    khi_s[:, WINDOW:, :] = jnp.where(lane_lo, 0.0, kd3).astype(BF16)
    v_s[:, WINDOW:, :] = vd.reshape(ns, tls, kvw).astype(BF16)

    if use_cache:
        kout_ref[:, 0:WINDOW - tls, :] = ck_ref[:, tls:, :]
        vout_ref[:, 0:WINDOW - tls, :] = cv_ref[:, tls:, :]
        kout_ref[:, WINDOW - tls:, :] = kd3
        vout_ref[:, WINDOW - tls:, :] = vd.reshape(ns, tls, kvw)
    else:
        @pl.when(t == pl.num_programs(1) - 1)
        def _():
            kout_ref[...] = kd3[:, tls - WINDOW:, :]
            vout_ref[...] = vd.reshape(ns, tls, kvw)[:, tls - WINDOW:, :]

    col = lax.broadcasted_iota(jnp.int32, (chunk, keys), 1)
    pair_lo = lax.broadcasted_iota(jnp.int32, (chunk, LANES), 1) < HEAD_DIM
    ones_blk = jnp.ones((keys, LANES), BF16)
    n_chunks = tls // chunk

    for s in range(ns):
        for c in range(n_chunks):
            r0 = c * chunk
            f0 = s * tls + r0
            masked = (not use_cache) and r0 < WINDOW
            if masked:
                n_inv = jnp.maximum(0, WINDOW - (t * tls + r0))
                valid = col >= n_inv
            for j in range(N_KV_HEADS):
                qj = q_s[f0:f0 + chunk, j * 256:(j + 1) * 256]
                lhs = jnp.concatenate([qj[:, :LANES], qj[:, LANES:]], axis=0)
                klo = klo_s[s, r0:r0 + keys, j * LANES:(j + 1) * LANES]
                khi = khi_s[s, r0:r0 + keys, j * LANES:(j + 1) * LANES]
                s_even = _dot_nt(lhs, klo)
                s_odd = _dot_nt(lhs, khi)
                scores = (s_even[:chunk], s_odd[:chunk], s_even[chunk:], s_odd[chunk:])
                ps, sink_terms = [], []
                for g in range(4):
                    hd = 4 * j + g
                    sg = scores[g] + bias_ref[hd]
                    if masked:
                        sg = jnp.where(valid, sg, -jnp.inf)
                    sink = sink_ref[hd]
                    m = jnp.maximum(jnp.max(sg, axis=-1, keepdims=True), sink)
                    ps.append(jnp.exp(sg - m).astype(BF16))
                    sink_terms.append(jnp.exp(sink - m))
                p_all = jnp.concatenate(ps, axis=0)
                vj = v_s[s, r0:r0 + keys, j * LANES:(j + 1) * LANES]
                ov = _dot(p_all, jnp.concatenate([vj, ones_blk], axis=1))
                outs = []
                for g in range(4):
                    blk = ov[g * chunk:(g + 1) * chunk]
                    outs.append(blk[:, :LANES] / (blk[:, LANES:] + sink_terms[g]))
                o01 = jnp.where(pair_lo, outs[0], outs[1])
                o23 = jnp.where(pair_lo, outs[2], outs[3])
                gt = gate_s[f0:f0 + chunk, j * 256:(j + 1) * 256]
                og = jnp.concatenate([o01, o23], axis=1) * (gt * _sigmoid(gt))
                og_s[f0:f0 + chunk, j * 256:(j + 1) * 256] = og.astype(BF16)

    if not use_cache:
        klo_s[:, 0:WINDOW, :] = klo_s[:, tls:tls + WINDOW, :]
        khi_s[:, 0:WINDOW, :] = khi_s[:, tls:tls + WINDOW, :]
        v_s[:, 0:WINDOW, :] = v_s[:, tls:tls + WINDOW, :]

    y = _dot(og_s[...], wout_ref[...])
    x1_ref[...] = (x + y).reshape(ns, tls, D_MODEL)


def _alibi_bias(chunk):
    slopes = (2.0 ** (-8.0 * np.arange(1, N_HEADS + 1, dtype=np.float32) / N_HEADS)).astype(np.float32)
    i = np.arange(chunk, dtype=np.float32)[:, None]
    j = np.arange(WINDOW + chunk, dtype=np.float32)[None, :]
    dist = np.abs(i + WINDOW - j).astype(np.float32)
    return jnp.asarray(-(slopes[:, None, None] * dist[None]), F32)


def _attn_layer(x, g, w_in_prep, w_out, sinks, *, ns, tls, chunk, cache=None):
    nb, seq, _ = x.shape
    use_cache = cache is not None
    grid = (nb // ns, seq // tls)
    kvw = 2 * N_KV_HEADS * HEAD_DIM
    keys = WINDOW + chunk
    rows = ns * tls
    const2 = lambda b, t: (0, 0)
    in_specs = [
        pl.BlockSpec((ns, tls, D_MODEL), lambda b, t: (b, t, 0)),
        pl.BlockSpec((1, D_MODEL), const2),
        pl.BlockSpec(w_in_prep.shape, const2),
        pl.BlockSpec(w_out.shape, const2),
        pl.BlockSpec(memory_space=pltpu.SMEM),
        pl.BlockSpec((N_HEADS, chunk, keys), lambda b, t: (0, 0, 0)),
    ]
    args = [x, g, w_in_prep, w_out, sinks, _alibi_bias(chunk)]
    if use_cache:
        in_specs += [pl.BlockSpec((ns, WINDOW, kvw), lambda b, t: (b, 0, 0))] * 2
        args += list(cache)
    out_shape = (
        jax.ShapeDtypeStruct((nb, seq, D_MODEL), F32),
        jax.ShapeDtypeStruct((nb, WINDOW, kvw), F32),
        jax.ShapeDtypeStruct((nb, WINDOW, kvw), F32),
    )
    out_specs = (
        pl.BlockSpec((ns, tls, D_MODEL), lambda b, t: (b, t, 0)),
        pl.BlockSpec((ns, WINDOW, kvw), lambda b, t: (b, 0, 0)),
        pl.BlockSpec((ns, WINDOW, kvw), lambda b, t: (b, 0, 0)),
    )
    scratch = [
        pltpu.VMEM((rows, D_MODEL), BF16),
        pltpu.VMEM((rows, D_MODEL), F32),
        pltpu.VMEM((ns, WINDOW + tls, kvw), BF16),
        pltpu.VMEM((ns, WINDOW + tls, kvw), BF16),
        pltpu.VMEM((ns, WINDOW + tls, kvw), BF16),
        pltpu.VMEM((rows, D_MODEL), BF16),
    ]
    return pl.pallas_call(
        functools.partial(_attn_kernel, ns=ns, tls=tls, chunk=chunk, use_cache=use_cache),
        grid=grid, in_specs=in_specs, out_specs=out_specs, out_shape=out_shape,
        scratch_shapes=scratch,
        compiler_params=pltpu.CompilerParams(
            dimension_semantics=("arbitrary", "arbitrary"),
            vmem_limit_bytes=VMEM_LIMIT_BYTES),
        name="attn_cache" if use_cache else "attn_prompt",
    )(*args)


def _split3(x):
    hi = x.astype(BF16)
    r1 = x - hi.astype(F32)
    mid = r1.astype(BF16)
    lo = (r1 - mid.astype(F32)).astype(BF16)
    return hi, mid, lo


def _dn_kernel(*refs, ns, tls, chunk, use_state):
    if use_state:
        (x_ref, g_ref, w1_ref, w2_ref, cw_ref, alog_ref, dtb_ref, ng_ref, wout_ref, fg_ref,
         tri_ref, cst_ref, sst_ref,
         y_ref, cout_ref, s_ref,
         cbuf, qn_s, kn_s, v_s, z_s, bt_s, gc_s, o_s, on_s) = refs
    else:
        (x_ref, g_ref, w1_ref, w2_ref, cw_ref, alog_ref, dtb_ref, ng_ref, wout_ref, fg_ref,
         tri_ref,
         y_ref, cout_ref, s_ref,
         cbuf, qn_s, kn_s, v_s, z_s, bt_s, gc_s, o_s, on_s) = refs
    t = pl.program_id(1)
    rows = ns * tls
    hist = CONV_WIDTH - 1
    qk_w = DN_HEADS * DN_DIM
    conv_w = 3 * qk_w

    if use_state:
        s_ref[...] = sst_ref[...]
        cbuf[:, CONV_PAD - hist:CONV_PAD, :] = cst_ref[...]
    else:
        @pl.when(t == 0)
        def _():
            s_ref[...] = jnp.zeros(s_ref.shape, F32)
            cbuf[:, CONV_PAD - hist:CONV_PAD, :] = jnp.zeros((ns, hist, conv_w), F32)

    x = x_ref[...].reshape(rows, D_MODEL)
    h = (x * _rms_scale(x) * g_ref[...]).astype(BF16)
    for seg in range(3):
        raw = _dot(h, w1_ref[:, seg * qk_w:(seg + 1) * qk_w])
        cbuf[:, CONV_PAD:, seg * qk_w:(seg + 1) * qk_w] = raw.reshape(ns, tls, qk_w)
    z_s[...] = _dot(h, w1_ref[:, conv_w:])
    ba = _dot(h, w2_ref[...])
    bt_s[...] = _sigmoid(ba[:, :LANES])
    a = ba[:, LANES:] + dtb_ref[...]
    softplus = jnp.maximum(a, 0.0) + jnp.log1p(jnp.exp(-jnp.abs(a)))
    g = -jnp.exp(alog_ref[...]) * softplus
    g_hi, g_mid, g_lo = _split3(g)
    gcs = _dot(tri_ref[...], jnp.concatenate([g_hi, g_mid, g_lo], axis=1))
    gc_s[...] = gcs[:, :LANES] + gcs[:, LANES:2 * LANES] + gcs[:, 2 * LANES:]

    for s in range(ns):
        for blk in range(conv_w // LANES):
            cols = slice(blk * LANES, (blk + 1) * LANES)
            acc = None
            for j in range(CONV_WIDTH):
                term = cbuf[s, CONV_PAD - hist + j:CONV_PAD - hist + j + tls, cols] * cw_ref[j:j + 1, cols]
                acc = term if acc is None else acc + term
            yv = acc * _sigmoid(acc)
            seg, hb = divmod(blk, DN_HEADS)
            dst = slice(hb * LANES, (hb + 1) * LANES)
            rs = slice(s * tls, (s + 1) * tls)
            if seg == 2:
                v_s[rs, dst] = yv
            else:
                inv = lax.rsqrt(jnp.sum(yv * yv, axis=-1, keepdims=True) + EPS)
                if seg == 0:
                    qn_s[rs, dst] = (yv * inv * (DN_DIM ** -0.5)).astype(BF16)
                else:
                    kn_s[rs, dst] = (yv * inv).astype(BF16)

    tail = cbuf[:, CONV_PAD + tls - hist:CONV_PAD + tls, :]
    cout_ref[...] = tail
    if not use_state:
        cbuf[:, CONV_PAD - hist:CONV_PAD, :] = tail

    hg = MXU_DIM // chunk
    n_groups = DN_HEADS // hg
    n_levels = int(round(math.log2(chunk)))
    ri = lax.broadcasted_iota(jnp.int32, (MXU_DIM, MXU_DIM), 0)
    ci = lax.broadcasted_iota(jnp.int32, (MXU_DIM, MXU_DIM), 1)
    tri_mask = (ri // chunk == ci // chunk) & (ri >= ci)
    diag = ri == ci
    eye = jnp.where(diag, 1.0, 0.0).astype(F32)
    lane = lax.broadcasted_iota(jnp.int32, (MXU_DIM, LANES), 1)

    for s in range(ns):
        for c in range(tls // chunk):
            f0 = s * tls + c * chunk
            rsl = slice(f0, f0 + chunk)
            for grp in range(n_groups):
                heads = list(range(grp * hg, (grp + 1) * hg))

                def stack(ref):
                    return jnp.concatenate([ref[rsl, hd * LANES:(hd + 1) * LANES] for hd in heads], axis=0)

                def col(ref):
                    return jnp.concatenate([ref[rsl, hd:hd + 1] for hd in heads], axis=0)

                k_st = stack(kn_s)
                q_st = stack(qn_s)
                v_st = stack(v_s)
                beta_c = col(bt_s)
                gc_c = col(gc_s)
                glast = [gc_s[f0 + chunk - 1:f0 + chunk, hd:hd + 1] for hd in heads]
                glast_c = jnp.concatenate([jnp.broadcast_to(gl, (chunk, 1)) for gl in glast], axis=0)
                eg = jnp.exp(gc_c)
                k32 = k_st.astype(F32)
                vb = (v_st * beta_c).astype(BF16)
                kbg = (k32 * (beta_c * eg)).astype(BF16)
                qg = (q_st.astype(F32) * eg).astype(BF16)
                kdec = (k32 * jnp.exp(glast_c - gc_c)).astype(BF16)

                hi, mid, lo = (p.astype(F32) for p in _split3(gc_c))
                lmat = jnp.where(lane < 3, 1.0,
                                 jnp.where(lane == 3, hi, jnp.where(lane == 4, mid, jnp.where(lane == 5, lo, 0.0))))
                rmat = jnp.where(lane == 0, -hi,
                                 jnp.where(lane == 1, -mid,
                                           jnp.where(lane == 2, -lo, jnp.where(lane < 6, 1.0, 0.0))))
                gd = _dot_nt(lmat.astype(BF16), rmat.astype(BF16))
                decay = jnp.exp(jnp.where(tri_mask, gd, -jnp.inf))

                kq = _dot_nt(jnp.concatenate([k_st, q_st], axis=0), k_st)
                m = jnp.where(diag, 0.0, kq[:MXU_DIM] * decay * beta_c)
                intra = (kq[MXU_DIM:] * decay).astype(BF16)

                p = eye - m
                mb = m.astype(BF16)
                qpow = _dot(mb, mb)
                for lvl in range(1, n_levels):
                    qb = qpow.astype(BF16)
                    if lvl < n_levels - 1:
                        pq = _dot(jnp.concatenate([p.astype(BF16), qb], axis=0), qb)
                        p = p + pq[:MXU_DIM]
                        qpow = pq[MXU_DIM:]
                    else:
                        p = p + _dot(p.astype(BF16), qb)
                uw = _dot(p.astype(BF16), jnp.concatenate([vb, kbg], axis=1))
                u = uw[:, :LANES]
                w = uw[:, LANES:].astype(BF16)

                ws, qs = [], []
                for i, hd in enumerate(heads):
                    hs = slice(i * chunk, (i + 1) * chunk)
                    sb = s_ref[s, hd].astype(BF16)
                    both = _dot(jnp.concatenate([w[hs], qg[hs]], axis=0), sb)
                    ws.append(both[:chunk])
                    qs.append(both[chunk:])
                v_new = u - jnp.concatenate(ws, axis=0)
                v_new_b = v_new.astype(BF16)
                o_st = jnp.concatenate(qs, axis=0) + _dot(intra, v_new_b)
                for i, hd in enumerate(heads):
                    hs = slice(i * chunk, (i + 1) * chunk)
                    s_ref[s, hd] = s_ref[s, hd] * jnp.exp(glast[i]) + _dot_tn(kdec[hs], v_new_b[hs])
                    o_s[rsl, hd * LANES:(hd + 1) * LANES] = o_st[hs]

    for hd in range(DN_HEADS):
        cols = slice(hd * LANES, (hd + 1) * LANES)
        o = o_s[:, cols]
        zz = z_s[:, cols]
        on_s[:, cols] = (o * _rms_scale(o) * ng_ref[...] * (zz * _sigmoid(zz))).astype(BF16)
    x2 = x + _dot(on_s[...], wout_ref[...])
    y_ref[...] = (x2 * _rms_scale(x2) * fg_ref[...]).reshape(ns, tls, D_MODEL)


def _block_tri(rows, chunk):
    r = np.arange(rows)
    m = (r[:, None] // chunk == r[None, :] // chunk) & (r[:, None] >= r[None, :])
    return jnp.asarray(m.astype(np.float32), BF16)


def _dn_layer(x, g, w1, w2, conv_w, alog, dtb, ng, w_out, fg, *, ns, tls, chunk, state=None):
    nb, seq, _ = x.shape
    use_state = state is not None
    grid = (nb // ns, seq // tls)
    rows = ns * tls
    hist = CONV_WIDTH - 1
    conv_dim = 3 * DN_HEADS * DN_DIM
    const2 = lambda b, t: (0, 0)
    tri = _block_tri(rows, chunk)
    args = [x, g, w1, w2, conv_w, alog, dtb, ng, w_out, fg, tri]
    in_specs = [pl.BlockSpec((ns, tls, D_MODEL), lambda b, t: (b, t, 0))]
    in_specs += [pl.BlockSpec(a.shape, const2) for a in args[1:]]
    if use_state:
        in_specs += [
            pl.BlockSpec((ns, hist, conv_dim), lambda b, t: (b, 0, 0)),
            pl.BlockSpec((ns, DN_HEADS, DN_DIM, DN_DIM), lambda b, t: (b, 0, 0, 0)),
        ]
        args += list(state)
    out_shape = (
        jax.ShapeDtypeStruct((nb, seq, D_MODEL), F32),
        jax.ShapeDtypeStruct((nb, hist, conv_dim), F32),
        jax.ShapeDtypeStruct((nb, DN_HEADS, DN_DIM, DN_DIM), F32),
    )
    out_specs = (
        pl.BlockSpec((ns, tls, D_MODEL), lambda b, t: (b, t, 0)),
        pl.BlockSpec((ns, hist, conv_dim), lambda b, t: (b, 0, 0)),
        pl.BlockSpec((ns, DN_HEADS, DN_DIM, DN_DIM), lambda b, t: (b, 0, 0, 0)),
    )
    scratch = [
        pltpu.VMEM((ns, CONV_PAD + tls, conv_dim), F32),
        pltpu.VMEM((rows, D_MODEL), BF16),
        pltpu.VMEM((rows, D_MODEL), BF16),
        pltpu.VMEM((rows, D_MODEL), F32),
        pltpu.VMEM((rows, D_MODEL), F32),
        pltpu.VMEM((rows, LANES), F32),
        pltpu.VMEM((rows, LANES), F32),
        pltpu.VMEM((rows, D_MODEL), F32),
        pltpu.VMEM((rows, D_MODEL), BF16),
    ]
    return pl.pallas_call(
        functools.partial(_dn_kernel, ns=ns, tls=tls, chunk=chunk, use_state=use_state),
        grid=grid, in_specs=in_specs, out_specs=out_specs, out_shape=out_shape,
        scratch_shapes=scratch,
        compiler_params=pltpu.CompilerParams(
            dimension_semantics=("arbitrary", "arbitrary"),
            vmem_limit_bytes=VMEM_LIMIT_BYTES),
        name="dn_state" if use_state else "dn_prompt",
    )(*args)


def _dup_heads(t):
    lead = t.shape[:-1]
    t4 = t.reshape(lead + (N_KV_HEADS, 1, HEAD_DIM))
    return jnp.broadcast_to(t4, lead + (N_KV_HEADS, 2, HEAD_DIM)).reshape(lead + (2 * N_KV_HEADS * HEAD_DIM,))


def _undup_heads(t):
    lead = t.shape[:-1]
    return t.reshape(lead + (N_KV_HEADS, 2, HEAD_DIM))[..., 0, :]


def _pad_lanes(v):
    return jnp.zeros((1, LANES), F32).at[0, :v.shape[0]].set(v)


def kernel(x_prompt, x_sample, cache_k, cache_v, state_conv, state_ssm, norm_g, final_norm_g,
           attn_w_in, attn_sinks, attn_w_out, dn_w_in, dn_conv_w, dn_a_log, dn_dt_bias,
           dn_norm_g, dn_w_out):
    attn_width = N_HEADS * HEAD_DIM
    kv_width = N_KV_HEADS * HEAD_DIM
    w = attn_w_in[0]
    wq = w[:, :attn_width] * (HEAD_DIM ** -0.5)
    wk = _dup_heads(w[:, attn_width:attn_width + kv_width])
    wv = _dup_heads(w[:, attn_width + kv_width:attn_width + 2 * kv_width])
    wg = w[:, attn_width + 2 * kv_width:]
    a_win = jnp.concatenate([wq, wk, wv, wg], axis=1).astype(BF16)
    a_wout = attn_w_out[0].astype(BF16)
    g0 = norm_g[0].reshape(1, D_MODEL)
    sinks = attn_sinks[0]

    conv_dim = 3 * DN_HEADS * DN_DIM
    dw = dn_w_in[0]
    d_w1 = dw[:, :conv_dim + DN_HEADS * DN_DIM].astype(BF16)
    tail = dw[:, conv_dim + DN_HEADS * DN_DIM:]
    d_w2 = jnp.zeros((D_MODEL, 2 * LANES), F32)
    d_w2 = d_w2.at[:, :DN_HEADS].set(tail[:, :DN_HEADS]).at[:, LANES:LANES + DN_HEADS].set(tail[:, DN_HEADS:])
    d_w2 = d_w2.astype(BF16)
    d_wout = dn_w_out[0].astype(BF16)
    g1 = norm_g[1].reshape(1, D_MODEL)
    fg = final_norm_g.reshape(1, D_MODEL)
    alog = _pad_lanes(dn_a_log[0])
    dtb = _pad_lanes(dn_dt_bias[0])
    ng = dn_norm_g[0].reshape(1, DN_DIM)
    dn_args = (g1, d_w1, d_w2, dn_conv_w[0], alog, dtb, ng, d_wout, fg)

    tl_p = 256
    x1p, kp, vp = _attn_layer(x_prompt, g0, a_win, a_wout, sinks, ns=1, tls=tl_p, chunk=CHUNK)
    yp, cp, sp = _dn_layer(x1p, *dn_args, ns=1, tls=tl_p, chunk=CHUNK)

    dec_seq = x_sample.shape[1]
    ns_s = 8
    ck = _dup_heads(cache_k[0].reshape(cache_k.shape[1], WINDOW, kv_width))
    cv = _dup_heads(cache_v[0].reshape(cache_v.shape[1], WINDOW, kv_width))
    x1s, ks, vs = _attn_layer(x_sample, g0, a_win, a_wout, sinks, ns=ns_s, tls=dec_seq,
                              chunk=min(CHUNK, dec_seq), cache=(ck, cv))
    ys, cs, ss = _dn_layer(x1s, *dn_args, ns=ns_s, tls=dec_seq, chunk=min(CHUNK, dec_seq),
                           state=(state_conv[0], state_ssm[0]))

    def kv_out(t):
        return _undup_heads(t).reshape(t.shape[0], WINDOW, N_KV_HEADS, HEAD_DIM)[None]

    return (yp, ys, kv_out(kp), kv_out(vp), cp[None], sp[None],
            kv_out(ks), kv_out(vs), cs[None], ss[None])
```

```python
import functools
import math

import numpy as np
import jax
import jax.numpy as jnp
from jax import lax
from jax.experimental import pallas as pl
from jax.experimental.pallas import tpu as pltpu

F32 = jnp.float32
BF16 = jnp.bfloat16

D_MODEL = 1024
EPS = 1e-6
CHUNK = 64
WINDOW = 128
N_HEADS = 16
N_KV_HEADS = 4
HEAD_DIM = 64
DN_HEADS = 8
DN_DIM = 128
CONV_WIDTH = 4
CONV_PAD = 8
LANES = 128
MXU_DIM = 256
VMEM_LIMIT_BYTES = 52 * 1024 * 1024
NEG_LOG2E = -1.4426950408889634

_NT = (((1,), (1,)), ((), ()))
_TN = (((0,), (0,)), ((), ()))


def _dot(a, b):
    return jnp.dot(a, b, preferred_element_type=F32)


def _dot_nt(a, b):
    return lax.dot_general(a, b, _NT, preferred_element_type=F32)


def _dot_tn(a, b):
    return lax.dot_general(a, b, _TN, preferred_element_type=F32)


def _sigmoid(x):
    return 1.0 / (1.0 + jnp.exp2(x * NEG_LOG2E))


def _rms_scale(x):
    return lax.rsqrt(jnp.mean(x * x, axis=-1, keepdims=True) + EPS)


def _run_interleaved(*gens):
    live = list(gens)
    while live:
        for g in list(live):
            try:
                next(g)
            except StopIteration:
                live.remove(g)


def _chain(gens):
    for g in gens:
        yield from g


def _attn_kernel(*refs, ns, tls, chunk, use_cache):
    if use_cache:
        (x_ref, g_ref, win_ref, wout_ref, sink_ref, bias_ref, ck_ref, cv_ref,
         x1_ref, kout_ref, vout_ref,
         q_s, gate_s, klo_s, khi_s, v_s, og_s) = refs
    else:
        (x_ref, g_ref, win_ref, wout_ref, sink_ref, bias_ref,
         x1_ref, kout_ref, vout_ref,
         q_s, gate_s, klo_s, khi_s, v_s, og_s) = refs
    t = pl.program_id(1)
    rows = ns * tls
    keys = WINDOW + chunk
    kvw = 2 * N_KV_HEADS * HEAD_DIM

    lane_lo = lax.broadcasted_iota(jnp.int32, (1, kvw), 1) % LANES < HEAD_DIM

    x = x_ref[...].reshape(rows, D_MODEL)
    h = (x * _rms_scale(x) * g_ref[...]).astype(BF16)
    q_s[...] = _dot(h, win_ref[:, 0:1024]).astype(BF16)
    kd = _dot(h, win_ref[:, 1024:1024 + kvw])
    vd = _dot(h, win_ref[:, 1024 + kvw:1024 + 2 * kvw])
    gate_s[...] = _dot(h, win_ref[:, 1024 + 2 * kvw:])

    if use_cache:
        ck = ck_ref[...]
        cv = cv_ref[...]
        klo_s[:, 0:WINDOW, :] = jnp.where(lane_lo, ck, 0.0).astype(BF16)
        khi_s[:, 0:WINDOW, :] = jnp.where(lane_lo, 0.0, ck).astype(BF16)
        v_s[:, 0:WINDOW, :] = cv.astype(BF16)
    else:
        @pl.when(t == 0)
        def _():
            zeros = jnp.zeros((ns, WINDOW, kvw), BF16)
            klo_s[:, 0:WINDOW, :] = zeros
            khi_s[:, 0:WINDOW, :] = zeros
            v_s[:, 0:WINDOW, :] = zeros

    kd3 = kd.reshape(ns, tls, kvw)
    klo_s[:, WINDOW:, :] = jnp.where(lane_lo, kd3, 0.0).astype(BF16)
    khi_s[:, WINDOW:, :] = jnp.where(lane_lo, 0.0, kd3).astype(BF16)
    v_s[:, WINDOW:, :] = vd.reshape(ns, tls, kvw).astype(BF16)

    if use_cache:
        kout_ref[:, 0:WINDOW - tls, :] = ck_ref[:, tls:, :]
        vout_ref[:, 0:WINDOW - tls, :] = cv_ref[:, tls:, :]
        kout_ref[:, WINDOW - tls:, :] = kd3
        vout_ref[:, WINDOW - tls:, :] = vd.reshape(ns, tls, kvw)
    else:
        @pl.when(t == pl.num_programs(1) - 1)
        def _():
            kout_ref[...] = kd3[:, tls - WINDOW:, :]
            vout_ref[...] = vd.reshape(ns, tls, kvw)[:, tls - WINDOW:, :]

    col = lax.broadcasted_iota(jnp.int32, (chunk, keys), 1)
    pair_lo = lax.broadcasted_iota(jnp.int32, (chunk, LANES), 1) < HEAD_DIM
    ones_blk = jnp.ones((keys, LANES), BF16)
    n_chunks = tls // chunk

    def attend(s, c):
        r0 = c * chunk
        f0 = s * tls + r0
        masked = (not use_cache) and r0 < WINDOW
        if masked:
            n_inv = jnp.maximum(0, WINDOW - (t * tls + r0))
            valid = col >= n_inv
        scores = []
        for j in range(N_KV_HEADS):
            qj = q_s[f0:f0 + chunk, j * 256:(j + 1) * 256]
            lhs = jnp.concatenate([qj[:, :LANES], qj[:, LANES:]], axis=0)
            klo = klo_s[s, r0:r0 + keys, j * LANES:(j + 1) * LANES]
            khi = khi_s[s, r0:r0 + keys, j * LANES:(j + 1) * LANES]
            s_even = _dot_nt(lhs, klo)
            s_odd = _dot_nt(lhs, khi)
            scores.append((s_even[:chunk], s_odd[:chunk], s_even[chunk:], s_odd[chunk:]))
        yield
        ovs, sink_terms = [], []
        for j in range(N_KV_HEADS):
            ps, st = [], []
            for g in range(4):
                hd = 4 * j + g
                sg = scores[j][g] + bias_ref[hd]
                if masked:
                    sg = jnp.where(valid, sg, -jnp.inf)
                sink = sink_ref[hd]
                m = jnp.maximum(jnp.max(sg, axis=-1, keepdims=True), sink)
                ps.append(jnp.exp(sg - m).astype(BF16))
                st.append(jnp.exp(sink - m))
            p_all = jnp.concatenate(ps, axis=0)
            vj = v_s[s, r0:r0 + keys, j * LANES:(j + 1) * LANES]
            ovs.append(_dot(p_all, jnp.concatenate([vj, ones_blk], axis=1)))
            sink_terms.append(st)
        yield
        for j in range(N_KV_HEADS):
            outs = []
            for g in range(4):
                blk = ovs[j][g * chunk:(g + 1) * chunk]
                outs.append(blk[:, :LANES] / (blk[:, LANES:] + sink_terms[j][g]))
            o01 = jnp.where(pair_lo, outs[0], outs[1])
            o23 = jnp.where(pair_lo, outs[2], outs[3])
            gt = gate_s[f0:f0 + chunk, j * 256:(j + 1) * 256]
            og = jnp.concatenate([o01, o23], axis=1) * (gt * _sigmoid(gt))
            og_s[f0:f0 + chunk, j * 256:(j + 1) * 256] = og.astype(BF16)
        yield

    units = [(s, c) for s in range(ns) for c in range(n_chunks)]
    for i in range(0, len(units), 2):
        _run_interleaved(*[attend(s, c) for s, c in units[i:i + 2]])

    if not use_cache:
        klo_s[:, 0:WINDOW, :] = klo_s[:, tls:tls + WINDOW, :]
        khi_s[:, 0:WINDOW, :] = khi_s[:, tls:tls + WINDOW, :]
        v_s[:, 0:WINDOW, :] = v_s[:, tls:tls + WINDOW, :]

    y = _dot(og_s[...], wout_ref[...])
    x1_ref[...] = (x + y).reshape(ns, tls, D_MODEL)


def _alibi_bias(chunk):
    slopes = (2.0 ** (-8.0 * np.arange(1, N_HEADS + 1, dtype=np.float32) / N_HEADS)).astype(np.float32)
    i = np.arange(chunk, dtype=np.float32)[:, None]
    j = np.arange(WINDOW + chunk, dtype=np.float32)[None, :]
    dist = np.abs(i + WINDOW - j).astype(np.float32)
    return jnp.asarray(-(slopes[:, None, None] * dist[None]), F32)


def _attn_layer(x, g, w_in_prep, w_out, sinks, *, ns, tls, chunk, cache=None):
    nb, seq, _ = x.shape
    use_cache = cache is not None
    grid = (nb // ns, seq // tls)
    kvw = 2 * N_KV_HEADS * HEAD_DIM
    keys = WINDOW + chunk
    rows = ns * tls
    const2 = lambda b, t: (0, 0)
    in_specs = [
        pl.BlockSpec((ns, tls, D_MODEL), lambda b, t: (b, t, 0)),
        pl.BlockSpec((1, D_MODEL), const2),
        pl.BlockSpec(w_in_prep.shape, const2),
        pl.BlockSpec(w_out.shape, const2),
        pl.BlockSpec(memory_space=pltpu.SMEM),
        pl.BlockSpec((N_HEADS, chunk, keys), lambda b, t: (0, 0, 0)),
    ]
    args = [x, g, w_in_prep, w_out, sinks, _alibi_bias(chunk)]
    if use_cache:
        in_specs += [pl.BlockSpec((ns, WINDOW, kvw), lambda b, t: (b, 0, 0))] * 2
        args += list(cache)
    out_shape = (
        jax.ShapeDtypeStruct((nb, seq, D_MODEL), F32),
        jax.ShapeDtypeStruct((nb, WINDOW, kvw), F32),
        jax.ShapeDtypeStruct((nb, WINDOW, kvw), F32),
    )
    out_specs = (
        pl.BlockSpec((ns, tls, D_MODEL), lambda b, t: (b, t, 0)),
        pl.BlockSpec((ns, WINDOW, kvw), lambda b, t: (b, 0, 0)),
        pl.BlockSpec((ns, WINDOW, kvw), lambda b, t: (b, 0, 0)),
    )
    scratch = [
        pltpu.VMEM((rows, D_MODEL), BF16),
        pltpu.VMEM((rows, D_MODEL), F32),
        pltpu.VMEM((ns, WINDOW + tls, kvw), BF16),
        pltpu.VMEM((ns, WINDOW + tls, kvw), BF16),
        pltpu.VMEM((ns, WINDOW + tls, kvw), BF16),
        pltpu.VMEM((rows, D_MODEL), BF16),
    ]
    return pl.pallas_call(
        functools.partial(_attn_kernel, ns=ns, tls=tls, chunk=chunk, use_cache=use_cache),
        grid=grid, in_specs=in_specs, out_specs=out_specs, out_shape=out_shape,
        scratch_shapes=scratch,
        compiler_params=pltpu.CompilerParams(
            dimension_semantics=("arbitrary", "arbitrary"),
            vmem_limit_bytes=VMEM_LIMIT_BYTES),
        name="attn_cache" if use_cache else "attn_prompt",
    )(*args)


def _split3(x):
    hi = x.astype(BF16)
    r1 = x - hi.astype(F32)
    mid = r1.astype(BF16)
    lo = (r1 - mid.astype(F32)).astype(BF16)
    return hi, mid, lo


def _dn_kernel(*refs, ns, tls, chunk, use_state):
    if use_state:
        (x_ref, g_ref, w1_ref, w2_ref, cw_ref, alog_ref, dtb_ref, ng_ref, wout_ref, fg_ref,
         tri_ref, cst_ref, sst_ref,
         y_ref, cout_ref, s_ref,
         cbuf, qn_s, kn_s, v_s, z_s, bt_s, gc_s, o_s, on_s,
         qg_s, kd_s, u_s, w_s, intra_s, sb_s) = refs
    else:
        (x_ref, g_ref, w1_ref, w2_ref, cw_ref, alog_ref, dtb_ref, ng_ref, wout_ref, fg_ref,
         tri_ref,
         y_ref, cout_ref, s_ref,
         cbuf, qn_s, kn_s, v_s, z_s, bt_s, gc_s, o_s, on_s,
         qg_s, kd_s, u_s, w_s, intra_s, sb_s) = refs
    t = pl.program_id(1)
    rows = ns * tls
    hist = CONV_WIDTH - 1
    qk_w = DN_HEADS * DN_DIM
    conv_w = 3 * qk_w
    n_pairs = DN_HEADS // 2

    if use_state:
        s_ref[...] = sst_ref[...]
        cbuf[:, CONV_PAD - hist:CONV_PAD, :] = cst_ref[...]
        sb_s[...] = jnp.zeros(sb_s.shape, BF16)
        for s in range(ns):
            for hd in range(DN_HEADS):
                d0 = (hd % 2) * DN_DIM
                sb_s[s, hd // 2, d0:d0 + DN_DIM, d0:d0 + DN_DIM] = sst_ref[s, hd].astype(BF16)
    else:
        @pl.when(t == 0)
        def _():
            s_ref[...] = jnp.zeros(s_ref.shape, F32)
            sb_s[...] = jnp.zeros(sb_s.shape, BF16)
            cbuf[:, CONV_PAD - hist:CONV_PAD, :] = jnp.zeros((ns, hist, conv_w), F32)

    x = x_ref[...].reshape(rows, D_MODEL)
    h = (x * _rms_scale(x) * g_ref[...]).astype(BF16)

    def phase_a():
        for c0 in range(0, conv_w, MXU_DIM):
            raw = _dot(h, w1_ref[:, c0:c0 + MXU_DIM])
            cbuf[:, CONV_PAD:, c0:c0 + MXU_DIM] = raw.reshape(ns, tls, MXU_DIM)
            yield
        for c0 in range(0, qk_w, MXU_DIM):
            z_s[:, c0:c0 + MXU_DIM] = _dot(h, w1_ref[:, conv_w + c0:conv_w + c0 + MXU_DIM])
            yield
        ba = _dot(h, w2_ref[...])
        bt_s[...] = _sigmoid(ba[:, :LANES])
        a = ba[:, LANES:] + dtb_ref[...]
        softplus = jnp.maximum(a, 0.0) + jnp.log1p(jnp.exp(-jnp.abs(a)))
        g = -jnp.exp(alog_ref[...]) * softplus
        g_hi, g_mid, g_lo = _split3(g)
        gcs = _dot(tri_ref[...], jnp.concatenate([g_hi, g_mid, g_lo], axis=1))
        gc_s[...] = gcs[:, :LANES] + gcs[:, LANES:2 * LANES] + gcs[:, 2 * LANES:]
        yield

    def phase_b():
        for blk in range(conv_w // LANES):
            cols = slice(blk * LANES, (blk + 1) * LANES)
            seg, hb = divmod(blk, DN_HEADS)
            dst = slice(hb * LANES, (hb + 1) * LANES)
            for s in range(ns):
                acc = None
                for j in range(CONV_WIDTH):
                    lo = CONV_PAD - hist + j
                    term = cbuf[s, lo:lo + tls, cols] * cw_ref[j:j + 1, cols]
                    acc = term if acc is None else acc + term
                yv = acc * _sigmoid(acc)
                rs = slice(s * tls, (s + 1) * tls)
                if seg == 2:
                    v_s[rs, dst] = yv
                else:
                    inv = lax.rsqrt(jnp.sum(yv * yv, axis=-1, keepdims=True) + EPS)
                    if seg == 0:
                        qn_s[rs, dst] = (yv * inv * (DN_DIM ** -0.5)).astype(BF16)
                    else:
                        kn_s[rs, dst] = (yv * inv).astype(BF16)
            yield

    ga, gb = phase_a(), phase_b()
    for _ in range(qk_w // MXU_DIM):
        next(ga)
    _run_interleaved(ga, gb)

    tail = cbuf[:, CONV_PAD + tls - hist:CONV_PAD + tls, :]
    cout_ref[...] = tail
    if not use_state:
        cbuf[:, CONV_PAD - hist:CONV_PAD, :] = tail

    hg = MXU_DIM // chunk
    n_groups = DN_HEADS // hg
    n_levels = int(round(math.log2(chunk)))
    ri = lax.broadcasted_iota(jnp.int32, (MXU_DIM, MXU_DIM), 0)
    ci = lax.broadcasted_iota(jnp.int32, (MXU_DIM, MXU_DIM), 1)
    tri_mask = (ri // chunk == ci // chunk) & (ri >= ci)
    diag = ri == ci
    eye = jnp.where(diag, 1.0, 0.0).astype(F32)
    lane = lax.broadcasted_iota(jnp.int32, (MXU_DIM, LANES), 1)
    n_chunks = tls // chunk

    def hcols(hd):
        return slice(hd * LANES, (hd + 1) * LANES)

    def phase_c(groups):
        st = []
        for (s, c, grp) in groups:
            f0 = s * tls + c * chunk
            rsl = slice(f0, f0 + chunk)
            heads = list(range(grp * hg, (grp + 1) * hg))
            k_h = [kn_s[rsl, hcols(hd)] for hd in heads]
            q_h = [qn_s[rsl, hcols(hd)] for hd in heads]
            rhs_rows, gc_cols, beta_cols = [], [], []
            for i, hd in enumerate(heads):
                beta_h = bt_s[rsl, hd:hd + 1]
                gc_h = gc_s[rsl, hd:hd + 1]
                gl_h = gc_s[f0 + chunk - 1:f0 + chunk, hd:hd + 1]
                eg = jnp.exp(gc_h)
                kf = k_h[i].astype(F32)
                vb = (v_s[rsl, hcols(hd)] * beta_h).astype(BF16)
                kbg = (kf * (beta_h * eg)).astype(BF16)
                qg_s[rsl, hcols(hd)] = (q_h[i].astype(F32) * eg).astype(BF16)
                kd_s[rsl, hcols(hd)] = (kf * jnp.exp(gl_h - gc_h)).astype(BF16)
                rhs_rows.append(jnp.concatenate([vb, kbg], axis=1))
                gc_cols.append(gc_h)
                beta_cols.append(beta_h)
            gc_c = jnp.concatenate(gc_cols, axis=0)
            hi, mid, lo = (p.astype(F32) for p in _split3(gc_c))
            lmat = jnp.where(lane < 3, 1.0,
                             jnp.where(lane == 3, hi, jnp.where(lane == 4, mid, jnp.where(lane == 5, lo, 0.0))))
            rmat = jnp.where(lane == 0, -hi,
                             jnp.where(lane == 1, -mid,
                                       jnp.where(lane == 2, -lo, jnp.where(lane < 6, 1.0, 0.0))))
            k_st = jnp.concatenate(k_h, axis=0)
            q_st = jnp.concatenate(q_h, axis=0)
            st.append(dict(
                rsl=rsl, heads=heads, gi=(s * n_chunks + c) * n_groups + grp,
                rhs=jnp.concatenate(rhs_rows, axis=0),
                beta=jnp.concatenate(beta_cols, axis=0),
                gd=_dot_nt(lmat.astype(BF16), rmat.astype(BF16)),
                kq=_dot_nt(jnp.concatenate([k_st, q_st], axis=0), k_st)))
        yield
        for d in st:
            decay = jnp.exp(jnp.where(tri_mask, d.pop('gd'), -jnp.inf))
            kq = d.pop('kq')
            m = jnp.where(diag, 0.0, kq[:MXU_DIM] * decay * d.pop('beta'))
            intra_s[d['gi']] = (kq[MXU_DIM:] * decay).astype(BF16)
            mb = m.astype(BF16)
            d['p'] = eye - m
            d['q'] = _dot(mb, mb)
        yield
        for lvl in range(1, n_levels):
            for d in st:
                qb = d['q'].astype(BF16)
                if lvl < n_levels - 1:
                    pq = _dot(jnp.concatenate([d['p'].astype(BF16), qb], axis=0), qb)
                    d['p'] = d['p'] + pq[:MXU_DIM]
                    d['q'] = pq[MXU_DIM:]
                else:
                    d['p'] = d['p'] + _dot(d['p'].astype(BF16), qb)
            yield
        for d in st:
            uw = _dot(d['p'].astype(BF16), d['rhs'])
            for i, hd in enumerate(d['heads']):
                hs = slice(i * chunk, (i + 1) * chunk)
                u_s[d['rsl'], hcols(hd)] = uw[hs, :LANES]
                w_s[d['rsl'], hcols(hd)] = uw[hs, LANES:].astype(BF16)
        yield

    def phase_d(s, c):
        f0 = s * tls + c * chunk
        rsl = slice(f0, f0 + chunk)
        both = []
        for p in range(n_pairs):
            pc = slice(p * MXU_DIM, (p + 1) * MXU_DIM)
            lhs = jnp.concatenate([w_s[rsl, pc], qg_s[rsl, pc]], axis=0)
            both.append(_dot(lhs, sb_s[s, p]))
        yield
        vnb, outer = [], []
        for p in range(n_pairs):
            pc = slice(p * MXU_DIM, (p + 1) * MXU_DIM)
            v_new = (u_s[rsl, pc] - both[p][:chunk]).astype(BF16)
            for i in range(2):
                hd = 2 * p + i
                vh = v_new[:, i * LANES:(i + 1) * LANES]
                vnb.append(vh)
                outer.append(_dot_tn(kd_s[rsl, hcols(hd)], vh))
        for grp in range(n_groups):
            heads = list(range(grp * hg, (grp + 1) * hg))
            v_st = jnp.concatenate([vnb[hd] for hd in heads], axis=0)
            qs_st = jnp.concatenate(
                [both[hd // 2][chunk:, (hd % 2) * LANES:(hd % 2 + 1) * LANES] for hd in heads], axis=0)
            o_st = qs_st + _dot(intra_s[(s * n_chunks + c) * n_groups + grp], v_st)
            for i, hd in enumerate(heads):
                o_s[rsl, hcols(hd)] = o_st[i * chunk:(i + 1) * chunk]
        yield
        for hd in range(DN_HEADS):
            gl = gc_s[f0 + chunk - 1:f0 + chunk, hd:hd + 1]
            s_new = s_ref[s, hd] * jnp.exp(gl) + outer[hd]
            s_ref[s, hd] = s_new
            d0 = (hd % 2) * DN_DIM
            sb_s[s, hd // 2, d0:d0 + DN_DIM, d0:d0 + DN_DIM] = s_new.astype(BF16)
        yield

    def phase_e(r0, r1):
        rsl = slice(r0, r1)
        for hd in range(DN_HEADS):
            o = o_s[rsl, hcols(hd)]
            zz = z_s[rsl, hcols(hd)]
            on_s[rsl, hcols(hd)] = (o * _rms_scale(o) * ng_ref[...] * (zz * _sigmoid(zz))).astype(BF16)
            yield
        xr = x_ref[...].reshape(rows, D_MODEL)[rsl]
        x2 = xr + _dot(on_s[rsl, :], wout_ref[...])
        yv = x2 * _rms_scale(x2) * fg_ref[...]
        if ns == 1:
            y_ref[0, rsl, :] = yv
        else:
            y_ref[r0 // tls:r1 // tls] = yv.reshape((r1 - r0) // tls, tls, D_MODEL)
        yield

    units = [(s, c) for s in range(ns) for c in range(n_chunks)]
    upb = max(1, 4 // n_groups)
    batches = [units[i:i + upb] for i in range(0, len(units), upb)]
    brows = upb * chunk

    def c_of(batch):
        return phase_c([(s, c, grp) for (s, c) in batch for grp in range(n_groups)])

    def d_of(batch):
        if use_state:
            return [phase_d(s, c) for (s, c) in batch]
        return [_chain([phase_d(s, c) for (s, c) in batch])]

    _run_interleaved(c_of(batches[0]))
    for i in range(1, len(batches)):
        extra = [phase_e((i - 2) * brows, (i - 1) * brows)] if i >= 2 else []
        _run_interleaved(c_of(batches[i]), *d_of(batches[i - 1]), *extra)
    last = len(batches) - 1
    extra = [phase_e((last - 1) * brows, last * brows)] if last >= 1 else []
    _run_interleaved(*d_of(batches[last]), *extra)
    _run_interleaved(phase_e(last * brows, (last + 1) * brows))


def _block_tri(rows, chunk):
    r = np.arange(rows)
    m = (r[:, None] // chunk == r[None, :] // chunk) & (r[:, None] >= r[None, :])
    return jnp.asarray(m.astype(np.float32), BF16)


def _dn_layer(x, g, w1, w2, conv_w, alog, dtb, ng, w_out, fg, *, ns, tls, chunk, state=None):
    nb, seq, _ = x.shape
    use_state = state is not None
    grid = (nb // ns, seq // tls)
    rows = ns * tls
    hist = CONV_WIDTH - 1
    conv_dim = 3 * DN_HEADS * DN_DIM
    n_stacks = rows // MXU_DIM * DN_HEADS
    const2 = lambda b, t: (0, 0)
    tri = _block_tri(rows, chunk)
    args = [x, g, w1, w2, conv_w, alog, dtb, ng, w_out, fg, tri]
    in_specs = [pl.BlockSpec((ns, tls, D_MODEL), lambda b, t: (b, t, 0))]
    in_specs += [pl.BlockSpec(a.shape, const2) for a in args[1:]]
    if use_state:
        in_specs += [
            pl.BlockSpec((ns, hist, conv_dim), lambda b, t: (b, 0, 0)),
            pl.BlockSpec((ns, DN_HEADS, DN_DIM, DN_DIM), lambda b, t: (b, 0, 0, 0)),
        ]
        args += list(state)
    out_shape = (
        jax.ShapeDtypeStruct((nb, seq, D_MODEL), F32),
        jax.ShapeDtypeStruct((nb, hist, conv_dim), F32),
        jax.ShapeDtypeStruct((nb, DN_HEADS, DN_DIM, DN_DIM), F32),
    )
    out_specs = (
        pl.BlockSpec((ns, tls, D_MODEL), lambda b, t: (b, t, 0)),
        pl.BlockSpec((ns, hist, conv_dim), lambda b, t: (b, 0, 0)),
        pl.BlockSpec((ns, DN_HEADS, DN_DIM, DN_DIM), lambda b, t: (b, 0, 0, 0)),
    )
    scratch = [
        pltpu.VMEM((ns, CONV_PAD + tls, conv_dim), F32),
        pltpu.VMEM((rows, D_MODEL), BF16),
        pltpu.VMEM((rows, D_MODEL), BF16),
        pltpu.VMEM((rows, D_MODEL), F32),
        pltpu.VMEM((rows, D_MODEL), F32),
        pltpu.VMEM((rows, LANES), F32),
        pltpu.VMEM((rows, LANES), F32),
        pltpu.VMEM((rows, D_MODEL), F32),
        pltpu.VMEM((rows, D_MODEL), BF16),
        pltpu.VMEM((rows, D_MODEL), BF16),
        pltpu.VMEM((rows, D_MODEL), BF16),
        pltpu.VMEM((rows, D_MODEL), F32),
        pltpu.VMEM((rows, D_MODEL), BF16),
        pltpu.VMEM((n_stacks, MXU_DIM, MXU_DIM), BF16),
        pltpu.VMEM((ns, DN_HEADS // 2, MXU_DIM, MXU_DIM), BF16),
    ]
    return pl.pallas_call(
        functools.partial(_dn_kernel, ns=ns, tls=tls, chunk=chunk, use_state=use_state),
        grid=grid, in_specs=in_specs, out_specs=out_specs, out_shape=out_shape,
        scratch_shapes=scratch,
        compiler_params=pltpu.CompilerParams(
            dimension_semantics=("arbitrary", "arbitrary"),
            vmem_limit_bytes=VMEM_LIMIT_BYTES),
        name="dn_state" if use_state else "dn_prompt",
    )(*args)


def _dup_heads(t):
    lead = t.shape[:-1]
    t4 = t.reshape(lead + (N_KV_HEADS, 1, HEAD_DIM))
    return jnp.broadcast_to(t4, lead + (N_KV_HEADS, 2, HEAD_DIM)).reshape(lead + (2 * N_KV_HEADS * HEAD_DIM,))


def _undup_heads(t):
    lead = t.shape[:-1]
    return t.reshape(lead + (N_KV_HEADS, 2, HEAD_DIM))[..., 0, :]


def _pad_lanes(v):
    return jnp.zeros((1, LANES), F32).at[0, :v.shape[0]].set(v)


def kernel(x_prompt, x_sample, cache_k, cache_v, state_conv, state_ssm, norm_g, final_norm_g,
           attn_w_in, attn_sinks, attn_w_out, dn_w_in, dn_conv_w, dn_a_log, dn_dt_bias,
           dn_norm_g, dn_w_out):
    attn_width = N_HEADS * HEAD_DIM
    kv_width = N_KV_HEADS * HEAD_DIM
    w = attn_w_in[0]
    wq = w[:, :attn_width] * (HEAD_DIM ** -0.5)
    wk = _dup_heads(w[:, attn_width:attn_width + kv_width])
    wv = _dup_heads(w[:, attn_width + kv_width:attn_width + 2 * kv_width])
    wg = w[:, attn_width + 2 * kv_width:]
    a_win = jnp.concatenate([wq, wk, wv, wg], axis=1).astype(BF16)
    a_wout = attn_w_out[0].astype(BF16)
    g0 = norm_g[0].reshape(1, D_MODEL)
    sinks = attn_sinks[0]

    conv_dim = 3 * DN_HEADS * DN_DIM
    dw = dn_w_in[0]
    d_w1 = dw[:, :conv_dim + DN_HEADS * DN_DIM].astype(BF16)
    tail = dw[:, conv_dim + DN_HEADS * DN_DIM:]
    d_w2 = jnp.zeros((D_MODEL, 2 * LANES), F32)
    d_w2 = d_w2.at[:, :DN_HEADS].set(tail[:, :DN_HEADS]).at[:, LANES:LANES + DN_HEADS].set(tail[:, DN_HEADS:])
    d_w2 = d_w2.astype(BF16)
    d_wout = dn_w_out[0].astype(BF16)
    g1 = norm_g[1].reshape(1, D_MODEL)
    fg = final_norm_g.reshape(1, D_MODEL)
    alog = _pad_lanes(dn_a_log[0])
    dtb = _pad_lanes(dn_dt_bias[0])
    ng = dn_norm_g[0].reshape(1, DN_DIM)
    dn_args = (g1, d_w1, d_w2, dn_conv_w[0], alog, dtb, ng, d_wout, fg)

    tl_p = 256
    x1p, kp, vp = _attn_layer(x_prompt, g0, a_win, a_wout, sinks, ns=1, tls=tl_p, chunk=CHUNK)
    yp, cp, sp = _dn_layer(x1p, *dn_args, ns=1, tls=tl_p, chunk=CHUNK)

    dec_seq = x_sample.shape[1]
    ns_s = 8
    ck = _dup_heads(cache_k[0].reshape(cache_k.shape[1], WINDOW, kv_width))
    cv = _dup_heads(cache_v[0].reshape(cache_v.shape[1], WINDOW, kv_width))
    x1s, ks, vs = _attn_layer(x_sample, g0, a_win, a_wout, sinks, ns=ns_s, tls=dec_seq,
                              chunk=min(CHUNK, dec_seq), cache=(ck, cv))
    ys, cs, ss = _dn_layer(x1s, *dn_args, ns=ns_s, tls=dec_seq, chunk=min(CHUNK, dec_seq),
                           state=(state_conv[0], state_ssm[0]))

    def kv_out(t):
        return _undup_heads(t).reshape(t.shape[0], WINDOW, N_KV_HEADS, HEAD_DIM)[None]

    return (yp, ys, kv_out(kp), kv_out(vp), cp[None], sp[None],
            kv_out(ks), kv_out(vs), cs[None], ss[None])
```

```python
import functools
import math

import numpy as np
import jax
import jax.numpy as jnp
from jax import lax
from jax.experimental import pallas as pl
from jax.experimental.pallas import tpu as pltpu

F32 = jnp.float32
BF16 = jnp.bfloat16

D_MODEL = 1024
EPS = 1e-6
CHUNK = 64
WINDOW = 128
N_HEADS = 16
N_KV_HEADS = 4
HEAD_DIM = 64
DN_HEADS = 8
DN_DIM = 128
CONV_WIDTH = 4
CONV_PAD = 8
LANES = 128
MXU_DIM = 256
VMEM_LIMIT_BYTES = 56 * 1024 * 1024
NEG_LOG2E = -1.4426950408889634

_NT = (((1,), (1,)), ((), ()))
_TN = (((0,), (0,)), ((), ()))


def _dot(a, b):
    return jnp.dot(a, b, preferred_element_type=F32)


def _dot_nt(a, b):
    return lax.dot_general(a, b, _NT, preferred_element_type=F32)


def _dot_tn(a, b):
    return lax.dot_general(a, b, _TN, preferred_element_type=F32)


def _sigmoid(x):
    return 1.0 / (1.0 + jnp.exp2(x * NEG_LOG2E))


def _rms_scale(x):
    return lax.rsqrt(jnp.mean(x * x, axis=-1, keepdims=True) + EPS)


def _run_interleaved(*gens):
    live = list(gens)
    while live:
        for g in list(live):
            try:
                next(g)
            except StopIteration:
                live.remove(g)


def _chain(gens):
    for g in gens:
        yield from g


def _attn_kernel(*refs, ns, tls, chunk, use_cache):
    if use_cache:
        (x_ref, g_ref, win_ref, wout_ref, sink_ref, bias_ref, ck_ref, cv_ref,
         x1_ref, kout_ref, vout_ref,
         q_s, gate_s, klo_s, khi_s, v_s, og_s) = refs
    else:
        (x_ref, g_ref, win_ref, wout_ref, sink_ref, bias_ref,
         x1_ref, kout_ref, vout_ref,
         q_s, gate_s, klo_s, khi_s, v_s, og_s) = refs
    t = pl.program_id(1)
    rows = ns * tls
    keys = WINDOW + chunk
    kvw = 2 * N_KV_HEADS * HEAD_DIM

    lane_lo = lax.broadcasted_iota(jnp.int32, (1, kvw), 1) % LANES < HEAD_DIM

    x = x_ref[...].reshape(rows, D_MODEL)
    h = (x * _rms_scale(x) * g_ref[...]).astype(BF16)
    q_s[...] = _dot(h, win_ref[:, 0:1024]).astype(BF16)
    kd = _dot(h, win_ref[:, 1024:1024 + kvw])
    vd = _dot(h, win_ref[:, 1024 + kvw:1024 + 2 * kvw])
    gate_s[...] = _dot(h, win_ref[:, 1024 + 2 * kvw:])

    if use_cache:
        ck = ck_ref[...]
        cv = cv_ref[...]
        klo_s[:, 0:WINDOW, :] = jnp.where(lane_lo, ck, 0.0).astype(BF16)
        khi_s[:, 0:WINDOW, :] = jnp.where(lane_lo, 0.0, ck).astype(BF16)
        v_s[:, 0:WINDOW, :] = cv.astype(BF16)
    else:
        @pl.when(t == 0)
        def _():
            zeros = jnp.zeros((ns, WINDOW, kvw), BF16)
            klo_s[:, 0:WINDOW, :] = zeros
            khi_s[:, 0:WINDOW, :] = zeros
            v_s[:, 0:WINDOW, :] = zeros

    kd3 = kd.reshape(ns, tls, kvw)
    klo_s[:, WINDOW:, :] = jnp.where(lane_lo, kd3, 0.0).astype(BF16)
    khi_s[:, WINDOW:, :] = jnp.where(lane_lo, 0.0, kd3).astype(BF16)
    v_s[:, WINDOW:, :] = vd.reshape(ns, tls, kvw).astype(BF16)

    if use_cache:
        kout_ref[:, 0:WINDOW - tls, :] = ck_ref[:, tls:, :]
        vout_ref[:, 0:WINDOW - tls, :] = cv_ref[:, tls:, :]
        kout_ref[:, WINDOW - tls:, :] = kd3
        vout_ref[:, WINDOW - tls:, :] = vd.reshape(ns, tls, kvw)
    else:
        @pl.when(t == pl.num_programs(1) - 1)
        def _():
            kout_ref[...] = kd3[:, tls - WINDOW:, :]
            vout_ref[...] = vd.reshape(ns, tls, kvw)[:, tls - WINDOW:, :]

    col = lax.broadcasted_iota(jnp.int32, (chunk, keys), 1)
    pair_lo = lax.broadcasted_iota(jnp.int32, (chunk, LANES), 1) < HEAD_DIM
    ones_blk = jnp.ones((keys, LANES), BF16)
    n_chunks = tls // chunk

    def attend(s, c):
        r0 = c * chunk
        f0 = s * tls + r0
        masked = (not use_cache) and r0 < WINDOW
        if masked:
            n_inv = jnp.maximum(0, WINDOW - (t * tls + r0))
            valid = col >= n_inv
        scores = []
        for j in range(N_KV_HEADS):
            qj = q_s[f0:f0 + chunk, j * 256:(j + 1) * 256]
            lhs = jnp.concatenate([qj[:, :LANES], qj[:, LANES:]], axis=0)
            klo = klo_s[s, r0:r0 + keys, j * LANES:(j + 1) * LANES]
            khi = khi_s[s, r0:r0 + keys, j * LANES:(j + 1) * LANES]
            s_even = _dot_nt(lhs, klo)
            s_odd = _dot_nt(lhs, khi)
            scores.append((s_even[:chunk], s_odd[:chunk], s_even[chunk:], s_odd[chunk:]))
        yield
        ovs, sink_terms = [], []
        for j in range(N_KV_HEADS):
            ps, st = [], []
            for g in range(4):
                hd = 4 * j + g
                sg = scores[j][g] + bias_ref[hd]
                if masked:
                    sg = jnp.where(valid, sg, -jnp.inf)
                sink = sink_ref[hd]
                m = jnp.maximum(jnp.max(sg, axis=-1, keepdims=True), sink)
                ps.append(jnp.exp(sg - m).astype(BF16))
                st.append(jnp.exp(sink - m))
            p_all = jnp.concatenate(ps, axis=0)
            vj = v_s[s, r0:r0 + keys, j * LANES:(j + 1) * LANES]
            ovs.append(_dot(p_all, jnp.concatenate([vj, ones_blk], axis=1)))
            sink_terms.append(st)
        yield
        for j in range(N_KV_HEADS):
            outs = []
            for g in range(4):
                blk = ovs[j][g * chunk:(g + 1) * chunk]
                outs.append(blk[:, :LANES] / (blk[:, LANES:] + sink_terms[j][g]))
            o01 = jnp.where(pair_lo, outs[0], outs[1])
            o23 = jnp.where(pair_lo, outs[2], outs[3])
            gt = gate_s[f0:f0 + chunk, j * 256:(j + 1) * 256]
            og = jnp.concatenate([o01, o23], axis=1) * (gt * _sigmoid(gt))
            og_s[f0:f0 + chunk, j * 256:(j + 1) * 256] = og.astype(BF16)
        yield

    units = [(s, c) for s in range(ns) for c in range(n_chunks)]
    for i in range(0, len(units), 2):
        _run_interleaved(*[attend(s, c) for s, c in units[i:i + 2]])

    if not use_cache:
        klo_s[:, 0:WINDOW, :] = klo_s[:, tls:tls + WINDOW, :]
        khi_s[:, 0:WINDOW, :] = khi_s[:, tls:tls + WINDOW, :]
        v_s[:, 0:WINDOW, :] = v_s[:, tls:tls + WINDOW, :]

    y = _dot(og_s[...], wout_ref[...])
    x1_ref[...] = (x + y).reshape(ns, tls, D_MODEL)


def _alibi_bias(chunk):
    slopes = (2.0 ** (-8.0 * np.arange(1, N_HEADS + 1, dtype=np.float32) / N_HEADS)).astype(np.float32)
    i = np.arange(chunk, dtype=np.float32)[:, None]
    j = np.arange(WINDOW + chunk, dtype=np.float32)[None, :]
    dist = np.abs(i + WINDOW - j).astype(np.float32)
    return jnp.asarray(-(slopes[:, None, None] * dist[None]), F32)


def _attn_layer(x, g, w_in_prep, w_out, sinks, *, ns, tls, chunk, cache=None):
    nb, seq, _ = x.shape
    use_cache = cache is not None
    grid = (nb // ns, seq // tls)
    kvw = 2 * N_KV_HEADS * HEAD_DIM
    keys = WINDOW + chunk
    rows = ns * tls
    const2 = lambda b, t: (0, 0)
    in_specs = [
        pl.BlockSpec((ns, tls, D_MODEL), lambda b, t: (b, t, 0)),
        pl.BlockSpec((1, D_MODEL), const2),
        pl.BlockSpec(w_in_prep.shape, const2, pipeline_mode=pl.Buffered(1)),
        pl.BlockSpec(w_out.shape, const2, pipeline_mode=pl.Buffered(1)),
        pl.BlockSpec(memory_space=pltpu.SMEM),
        pl.BlockSpec((N_HEADS, chunk, keys), lambda b, t: (0, 0, 0), pipeline_mode=pl.Buffered(1)),
    ]
    args = [x, g, w_in_prep, w_out, sinks, _alibi_bias(chunk)]
    if use_cache:
        in_specs += [pl.BlockSpec((ns, WINDOW, kvw), lambda b, t: (b, 0, 0))] * 2
        args += list(cache)
    out_shape = (
        jax.ShapeDtypeStruct((nb, seq, D_MODEL), F32),
        jax.ShapeDtypeStruct((nb, WINDOW, kvw), F32),
        jax.ShapeDtypeStruct((nb, WINDOW, kvw), F32),
    )
    out_specs = (
        pl.BlockSpec((ns, tls, D_MODEL), lambda b, t: (b, t, 0)),
        pl.BlockSpec((ns, WINDOW, kvw), lambda b, t: (b, 0, 0)),
        pl.BlockSpec((ns, WINDOW, kvw), lambda b, t: (b, 0, 0)),
    )
    scratch = [
        pltpu.VMEM((rows, D_MODEL), BF16),
        pltpu.VMEM((rows, D_MODEL), F32),
        pltpu.VMEM((ns, WINDOW + tls, kvw), BF16),
        pltpu.VMEM((ns, WINDOW + tls, kvw), BF16),
        pltpu.VMEM((ns, WINDOW + tls, kvw), BF16),
        pltpu.VMEM((rows, D_MODEL), BF16),
    ]
    return pl.pallas_call(
        functools.partial(_attn_kernel, ns=ns, tls=tls, chunk=chunk, use_cache=use_cache),
        grid=grid, in_specs=in_specs, out_specs=out_specs, out_shape=out_shape,
        scratch_shapes=scratch,
        compiler_params=pltpu.CompilerParams(
            dimension_semantics=("arbitrary", "arbitrary"),
            vmem_limit_bytes=VMEM_LIMIT_BYTES),
        name="attn_cache" if use_cache else "attn_prompt",
    )(*args)


def _split3(x):
    hi = x.astype(BF16)
    r1 = x - hi.astype(F32)
    mid = r1.astype(BF16)
    lo = (r1 - mid.astype(F32)).astype(BF16)
    return hi, mid, lo


def _dn_kernel(*refs, ns, tls, chunk, use_state):
    if use_state:
        (x_ref, g_ref, w1_ref, w2_ref, cw_ref, alog_ref, dtb_ref, ng_ref, wout_ref, fg_ref,
         tri_ref, cst_ref, sst_ref,
         y_ref, cout_ref, s_ref,
         cbuf, qn_s, kn_s, v_s, z_s, bt_s, gc_s, o_s, on_s,
         qg_s, kd_s, u_s, w_s, intra_s, sb_s) = refs
    else:
        (x_ref, g_ref, w1_ref, w2_ref, cw_ref, alog_ref, dtb_ref, ng_ref, wout_ref, fg_ref,
         tri_ref,
         y_ref, cout_ref, s_ref,
         cbuf, qn_s, kn_s, v_s, z_s, bt_s, gc_s, o_s, on_s,
         qg_s, kd_s, u_s, w_s, intra_s, sb_s) = refs
    t = pl.program_id(1)
    rows = ns * tls
    hist = CONV_WIDTH - 1
    qk_w = DN_HEADS * DN_DIM
    conv_w = 3 * qk_w
    n_pairs = DN_HEADS // 2

    if use_state:
        s_ref[...] = sst_ref[...]
        cbuf[:, CONV_PAD - hist:CONV_PAD, :] = cst_ref[...]
        sb_s[...] = jnp.zeros(sb_s.shape, BF16)
        for s in range(ns):
            for hd in range(DN_HEADS):
                d0 = (hd % 2) * DN_DIM
                sb_s[s, hd // 2, d0:d0 + DN_DIM, d0:d0 + DN_DIM] = sst_ref[s, hd].astype(BF16)
    else:
        @pl.when(t == 0)
        def _():
            s_ref[...] = jnp.zeros(s_ref.shape, F32)
            sb_s[...] = jnp.zeros(sb_s.shape, BF16)
            cbuf[:, CONV_PAD - hist:CONV_PAD, :] = jnp.zeros((ns, hist, conv_w), F32)

    x = x_ref[...].reshape(rows, D_MODEL)
    h = (x * _rms_scale(x) * g_ref[...]).astype(BF16)

    def phase_a():
        for c0 in range(0, conv_w, MXU_DIM):
            raw = _dot(h, w1_ref[:, c0:c0 + MXU_DIM])
            cbuf[:, CONV_PAD:, c0:c0 + MXU_DIM] = raw.reshape(ns, tls, MXU_DIM)
            yield
        for c0 in range(0, qk_w, MXU_DIM):
            z_s[:, c0:c0 + MXU_DIM] = _dot(h, w1_ref[:, conv_w + c0:conv_w + c0 + MXU_DIM])
            yield
        ba = _dot(h, w2_ref[...])
        bt_s[...] = _sigmoid(ba[:, :LANES])
        a = ba[:, LANES:] + dtb_ref[...]
        softplus = jnp.maximum(a, 0.0) + jnp.log1p(jnp.exp(-jnp.abs(a)))
        g = -jnp.exp(alog_ref[...]) * softplus
        g_hi, g_mid, g_lo = _split3(g)
        gcs = _dot(tri_ref[...], jnp.concatenate([g_hi, g_mid, g_lo], axis=1))
        gc_s[...] = gcs[:, :LANES] + gcs[:, LANES:2 * LANES] + gcs[:, 2 * LANES:]
        yield

    def phase_b():
        for blk in range(conv_w // LANES):
            cols = slice(blk * LANES, (blk + 1) * LANES)
            seg, hb = divmod(blk, DN_HEADS)
            dst = slice(hb * LANES, (hb + 1) * LANES)
            for s in range(ns):
                acc = None
                for j in range(CONV_WIDTH):
                    lo = CONV_PAD - hist + j
                    term = cbuf[s, lo:lo + tls, cols] * cw_ref[j:j + 1, cols]
                    acc = term if acc is None else acc + term
                yv = acc * _sigmoid(acc)
                rs = slice(s * tls, (s + 1) * tls)
                if seg == 2:
                    v_s[rs, dst] = yv
                else:
                    inv = lax.rsqrt(jnp.sum(yv * yv, axis=-1, keepdims=True) + EPS)
                    if seg == 0:
                        qn_s[rs, dst] = (yv * inv * (DN_DIM ** -0.5)).astype(BF16)
                    else:
                        kn_s[rs, dst] = (yv * inv).astype(BF16)
            yield

    ga, gb = phase_a(), phase_b()
    for _ in range(qk_w // MXU_DIM):
        next(ga)
    _run_interleaved(ga, gb)

    tail = cbuf[:, CONV_PAD + tls - hist:CONV_PAD + tls, :]
    cout_ref[...] = tail
    if not use_state:
        cbuf[:, CONV_PAD - hist:CONV_PAD, :] = tail

    hg = MXU_DIM // chunk
    n_groups = DN_HEADS // hg
    n_levels = int(round(math.log2(chunk)))
    ri = lax.broadcasted_iota(jnp.int32, (MXU_DIM, MXU_DIM), 0)
    ci = lax.broadcasted_iota(jnp.int32, (MXU_DIM, MXU_DIM), 1)
    tri_mask = (ri // chunk == ci // chunk) & (ri >= ci)
    diag = ri == ci
    eye = jnp.where(diag, 1.0, 0.0).astype(F32)
    lane = lax.broadcasted_iota(jnp.int32, (MXU_DIM, LANES), 1)
    n_chunks = tls // chunk

    def hcols(hd):
        return slice(hd * LANES, (hd + 1) * LANES)

    def phase_c(groups):
        st = []
        for (s, c, grp) in groups:
            f0 = s * tls + c * chunk
            rsl = slice(f0, f0 + chunk)
            heads = list(range(grp * hg, (grp + 1) * hg))
            k_h = [kn_s[rsl, hcols(hd)] for hd in heads]
            q_h = [qn_s[rsl, hcols(hd)] for hd in heads]
            rhs_rows, gc_cols, beta_cols = [], [], []
            for i, hd in enumerate(heads):
                beta_h = bt_s[rsl, hd:hd + 1]
                gc_h = gc_s[rsl, hd:hd + 1]
                gl_h = gc_s[f0 + chunk - 1:f0 + chunk, hd:hd + 1]
                eg = jnp.exp(gc_h)
                kf = k_h[i].astype(F32)
                vb = (v_s[rsl, hcols(hd)] * beta_h).astype(BF16)
                kbg = (kf * (beta_h * eg)).astype(BF16)
                qg_s[rsl, hcols(hd)] = (q_h[i].astype(F32) * eg).astype(BF16)
                kd_s[rsl, hcols(hd)] = (kf * jnp.exp(gl_h - gc_h)).astype(BF16)
                rhs_rows.append(jnp.concatenate([vb, kbg], axis=1))
                gc_cols.append(gc_h)
                beta_cols.append(beta_h)
            gc_c = jnp.concatenate(gc_cols, axis=0)
            hi, mid, lo = (p.astype(F32) for p in _split3(gc_c))
            lmat = jnp.where(lane < 3, 1.0,
                             jnp.where(lane == 3, hi, jnp.where(lane == 4, mid, jnp.where(lane == 5, lo, 0.0))))
            rmat = jnp.where(lane == 0, -hi,
                             jnp.where(lane == 1, -mid,
                                       jnp.where(lane == 2, -lo, jnp.where(lane < 6, 1.0, 0.0))))
            k_st = jnp.concatenate(k_h, axis=0)
            q_st = jnp.concatenate(q_h, axis=0)
            st.append(dict(
                rsl=rsl, heads=heads, gi=(s * n_chunks + c) * n_groups + grp,
                rhs=jnp.concatenate(rhs_rows, axis=0),
                beta=jnp.concatenate(beta_cols, axis=0),
                gd=_dot_nt(lmat.astype(BF16), rmat.astype(BF16)),
                kq=_dot_nt(jnp.concatenate([k_st, q_st], axis=0), k_st)))
        yield
        for d in st:
            decay = jnp.exp(jnp.where(tri_mask, d.pop('gd'), -jnp.inf))
            kq = d.pop('kq')
            m = jnp.where(diag, 0.0, kq[:MXU_DIM] * decay * d.pop('beta'))
            intra_s[d['gi']] = (kq[MXU_DIM:] * decay).astype(BF16)
            mb = m.astype(BF16)
            d['p'] = eye - m
            d['q'] = _dot(mb, mb)
        yield
        for lvl in range(1, n_levels):
            for d in st:
                qb = d['q'].astype(BF16)
                if lvl < n_levels - 1:
                    pq = _dot(jnp.concatenate([d['p'].astype(BF16), qb], axis=0), qb)
                    d['p'] = d['p'] + pq[:MXU_DIM]
                    d['q'] = pq[MXU_DIM:]
                else:
                    d['p'] = d['p'] + _dot(d['p'].astype(BF16), qb)
            yield
        for d in st:
            uw = _dot(d['p'].astype(BF16), d['rhs'])
            for i, hd in enumerate(d['heads']):
                hs = slice(i * chunk, (i + 1) * chunk)
                u_s[d['rsl'], hcols(hd)] = uw[hs, :LANES]
                w_s[d['rsl'], hcols(hd)] = uw[hs, LANES:].astype(BF16)
        yield

    def phase_d(s, c):
        f0 = s * tls + c * chunk
        rsl = slice(f0, f0 + chunk)
        both = []
        for p in range(n_pairs):
            pc = slice(p * MXU_DIM, (p + 1) * MXU_DIM)
            lhs = jnp.concatenate([w_s[rsl, pc], qg_s[rsl, pc]], axis=0)
            both.append(_dot(lhs, sb_s[s, p]))
        yield
        vnb, outer = [], []
        for p in range(n_pairs):
            pc = slice(p * MXU_DIM, (p + 1) * MXU_DIM)
            v_new = (u_s[rsl, pc] - both[p][:chunk]).astype(BF16)
            for i in range(2):
                hd = 2 * p + i
                vh = v_new[:, i * LANES:(i + 1) * LANES]
                vnb.append(vh)
                outer.append(_dot_tn(kd_s[rsl, hcols(hd)], vh))
        for grp in range(n_groups):
            heads = list(range(grp * hg, (grp + 1) * hg))
            v_st = jnp.concatenate([vnb[hd] for hd in heads], axis=0)
            qs_st = jnp.concatenate(
                [both[hd // 2][chunk:, (hd % 2) * LANES:(hd % 2 + 1) * LANES] for hd in heads], axis=0)
            o_st = qs_st + _dot(intra_s[(s * n_chunks + c) * n_groups + grp], v_st)
            for i, hd in enumerate(heads):
                o_s[rsl, hcols(hd)] = o_st[i * chunk:(i + 1) * chunk]
        yield
        for hd in range(DN_HEADS):
            gl = gc_s[f0 + chunk - 1:f0 + chunk, hd:hd + 1]
            s_new = s_ref[s, hd] * jnp.exp(gl) + outer[hd]
            s_ref[s, hd] = s_new
            d0 = (hd % 2) * DN_DIM
            sb_s[s, hd // 2, d0:d0 + DN_DIM, d0:d0 + DN_DIM] = s_new.astype(BF16)
        yield

    def phase_e(r0, r1):
        rsl = slice(r0, r1)
        for hd in range(DN_HEADS):
            o = o_s[rsl, hcols(hd)]
            zz = z_s[rsl, hcols(hd)]
            on_s[rsl, hcols(hd)] = (o * _rms_scale(o) * ng_ref[...] * (zz * _sigmoid(zz))).astype(BF16)
            yield
        xr = x_ref[...].reshape(rows, D_MODEL)[rsl]
        x2 = xr + _dot(on_s[rsl, :], wout_ref[...])
        yv = x2 * _rms_scale(x2) * fg_ref[...]
        if ns == 1:
            y_ref[0, rsl, :] = yv
        else:
            y_ref[r0 // tls:r1 // tls] = yv.reshape((r1 - r0) // tls, tls, D_MODEL)
        yield

    units = [(s, c) for s in range(ns) for c in range(n_chunks)]
    upb = max(1, 4 // n_groups)
    batches = [units[i:i + upb] for i in range(0, len(units), upb)]
    brows = upb * chunk

    def c_of(batch):
        return phase_c([(s, c, grp) for (s, c) in batch for grp in range(n_groups)])

    def d_of(batch):
        if use_state:
            return [phase_d(s, c) for (s, c) in batch]
        return [_chain([phase_d(s, c) for (s, c) in batch])]

    _run_interleaved(c_of(batches[0]))
    for i in range(1, len(batches)):
        extra = [phase_e((i - 2) * brows, (i - 1) * brows)] if i >= 2 else []
        _run_interleaved(c_of(batches[i]), *d_of(batches[i - 1]), *extra)
    last = len(batches) - 1
    extra = [phase_e((last - 1) * brows, last * brows)] if last >= 1 else []
    _run_interleaved(*d_of(batches[last]), *extra)
    _run_interleaved(phase_e(last * brows, (last + 1) * brows))


def _block_tri(rows, chunk):
    r = np.arange(rows)
    m = (r[:, None] // chunk == r[None, :] // chunk) & (r[:, None] >= r[None, :])
    return jnp.asarray(m.astype(np.float32), BF16)


def _dn_layer(x, g, w1, w2, conv_w, alog, dtb, ng, w_out, fg, *, ns, tls, chunk, state=None):
    nb, seq, _ = x.shape
    use_state = state is not None
    grid = (nb // ns, seq // tls)
    rows = ns * tls
    hist = CONV_WIDTH - 1
    conv_dim = 3 * DN_HEADS * DN_DIM
    n_stacks = rows // MXU_DIM * DN_HEADS
    const2 = lambda b, t: (0, 0)
    tri = _block_tri(rows, chunk)
    args = [x, g, w1, w2, conv_w, alog, dtb, ng, w_out, fg, tri]
    in_specs = [pl.BlockSpec((ns, tls, D_MODEL), lambda b, t: (b, t, 0))]
    in_specs += [pl.BlockSpec(a.shape, const2, pipeline_mode=pl.Buffered(1)) for a in args[1:]]
    if use_state:
        in_specs += [
            pl.BlockSpec((ns, hist, conv_dim), lambda b, t: (b, 0, 0)),
            pl.BlockSpec((ns, DN_HEADS, DN_DIM, DN_DIM), lambda b, t: (b, 0, 0, 0)),
        ]
        args += list(state)
    out_shape = (
        jax.ShapeDtypeStruct((nb, seq, D_MODEL), F32),
        jax.ShapeDtypeStruct((nb, hist, conv_dim), F32),
        jax.ShapeDtypeStruct((nb, DN_HEADS, DN_DIM, DN_DIM), F32),
    )
    out_specs = (
        pl.BlockSpec((ns, tls, D_MODEL), lambda b, t: (b, t, 0)),
        pl.BlockSpec((ns, hist, conv_dim), lambda b, t: (b, 0, 0)),
        pl.BlockSpec((ns, DN_HEADS, DN_DIM, DN_DIM), lambda b, t: (b, 0, 0, 0)),
    )
    scratch = [
        pltpu.VMEM((ns, CONV_PAD + tls, conv_dim), F32),
        pltpu.VMEM((rows, D_MODEL), BF16),
        pltpu.VMEM((rows, D_MODEL), BF16),
        pltpu.VMEM((rows, D_MODEL), F32),
        pltpu.VMEM((rows, D_MODEL), F32),
        pltpu.VMEM((rows, LANES), F32),
        pltpu.VMEM((rows, LANES), F32),
        pltpu.VMEM((rows, D_MODEL), F32),
        pltpu.VMEM((rows, D_MODEL), BF16),
        pltpu.VMEM((rows, D_MODEL), BF16),
        pltpu.VMEM((rows, D_MODEL), BF16),
        pltpu.VMEM((rows, D_MODEL), F32),
        pltpu.VMEM((rows, D_MODEL), BF16),
        pltpu.VMEM((n_stacks, MXU_DIM, MXU_DIM), BF16),
        pltpu.VMEM((ns, DN_HEADS // 2, MXU_DIM, MXU_DIM), BF16),
    ]
    return pl.pallas_call(
        functools.partial(_dn_kernel, ns=ns, tls=tls, chunk=chunk, use_state=use_state),
        grid=grid, in_specs=in_specs, out_specs=out_specs, out_shape=out_shape,
        scratch_shapes=scratch,
        compiler_params=pltpu.CompilerParams(
            dimension_semantics=("arbitrary", "arbitrary"),
            vmem_limit_bytes=VMEM_LIMIT_BYTES),
        name="dn_state" if use_state else "dn_prompt",
    )(*args)


def _dup_heads(t):
    lead = t.shape[:-1]
    t4 = t.reshape(lead + (N_KV_HEADS, 1, HEAD_DIM))
    return jnp.broadcast_to(t4, lead + (N_KV_HEADS, 2, HEAD_DIM)).reshape(lead + (2 * N_KV_HEADS * HEAD_DIM,))


def _undup_heads(t):
    lead = t.shape[:-1]
    return t.reshape(lead + (N_KV_HEADS, 2, HEAD_DIM))[..., 0, :]


def _pad_lanes(v):
    return jnp.zeros((1, LANES), F32).at[0, :v.shape[0]].set(v)


def kernel(x_prompt, x_sample, cache_k, cache_v, state_conv, state_ssm, norm_g, final_norm_g,
           attn_w_in, attn_sinks, attn_w_out, dn_w_in, dn_conv_w, dn_a_log, dn_dt_bias,
           dn_norm_g, dn_w_out):
    attn_width = N_HEADS * HEAD_DIM
    kv_width = N_KV_HEADS * HEAD_DIM
    w = attn_w_in[0]
    wq = w[:, :attn_width] * (HEAD_DIM ** -0.5)
    wk = _dup_heads(w[:, attn_width:attn_width + kv_width])
    wv = _dup_heads(w[:, attn_width + kv_width:attn_width + 2 * kv_width])
    wg = w[:, attn_width + 2 * kv_width:]
    a_win = jnp.concatenate([wq, wk, wv, wg], axis=1).astype(BF16)
    a_wout = attn_w_out[0].astype(BF16)
    g0 = norm_g[0].reshape(1, D_MODEL)
    sinks = attn_sinks[0]

    conv_dim = 3 * DN_HEADS * DN_DIM
    dw = dn_w_in[0]
    d_w1 = dw[:, :conv_dim + DN_HEADS * DN_DIM].astype(BF16)
    tail = dw[:, conv_dim + DN_HEADS * DN_DIM:]
    d_w2 = jnp.zeros((D_MODEL, 2 * LANES), F32)
    d_w2 = d_w2.at[:, :DN_HEADS].set(tail[:, :DN_HEADS]).at[:, LANES:LANES + DN_HEADS].set(tail[:, DN_HEADS:])
    d_w2 = d_w2.astype(BF16)
    d_wout = dn_w_out[0].astype(BF16)
    g1 = norm_g[1].reshape(1, D_MODEL)
    fg = final_norm_g.reshape(1, D_MODEL)
    alog = _pad_lanes(dn_a_log[0])
    dtb = _pad_lanes(dn_dt_bias[0])
    ng = dn_norm_g[0].reshape(1, DN_DIM)
    dn_args = (g1, d_w1, d_w2, dn_conv_w[0], alog, dtb, ng, d_wout, fg)

    tl_p = 512
    x1p, kp, vp = _attn_layer(x_prompt, g0, a_win, a_wout, sinks, ns=1, tls=tl_p, chunk=CHUNK)
    yp, cp, sp = _dn_layer(x1p, *dn_args, ns=1, tls=tl_p, chunk=CHUNK)

    dec_seq = x_sample.shape[1]
    ns_s = 8
    ck = _dup_heads(cache_k[0].reshape(cache_k.shape[1], WINDOW, kv_width))
    cv = _dup_heads(cache_v[0].reshape(cache_v.shape[1], WINDOW, kv_width))
    x1s, ks, vs = _attn_layer(x_sample, g0, a_win, a_wout, sinks, ns=ns_s, tls=dec_seq,
                              chunk=min(CHUNK, dec_seq), cache=(ck, cv))
    ys, cs, ss = _dn_layer(x1s, *dn_args, ns=ns_s, tls=dec_seq, chunk=min(CHUNK, dec_seq),
                           state=(state_conv[0], state_ssm[0]))

    def kv_out(t):
        return _undup_heads(t).reshape(t.shape[0], WINDOW, N_KV_HEADS, HEAD_DIM)[None]

    return (yp, ys, kv_out(kp), kv_out(vp), cp[None], sp[None],
            kv_out(ks), kv_out(vs), cs[None], ss[None])
```

```python
import functools
import math

import numpy as np
import jax
import jax.numpy as jnp
from jax import lax
from jax.experimental import pallas as pl
from jax.experimental.pallas import tpu as pltpu

F32 = jnp.float32
BF16 = jnp.bfloat16

D_MODEL = 1024
EPS = 1e-6
CHUNK = 64
WINDOW = 128
N_HEADS = 16
N_KV_HEADS = 4
HEAD_DIM = 64
DN_HEADS = 8
DN_DIM = 128
CONV_WIDTH = 4
CONV_PAD = 8
SUBLANES = 8
LANES = 128
MXU_DIM = 256
VMEM_LIMIT_BYTES = 56 * 1024 * 1024
LOG2E = 1.4426950408889634

_NT = (((1,), (1,)), ((), ()))
_TN = (((0,), (0,)), ((), ()))


def _dot(a, b):
    return jnp.dot(a, b, preferred_element_type=F32)


def _dot_nt(a, b):
    return lax.dot_general(a, b, _NT, preferred_element_type=F32)


def _dot_tn(a, b):
    return lax.dot_general(a, b, _TN, preferred_element_type=F32)


def _sigmoid(x):
    return 1.0 / (1.0 + jnp.exp2(x * -LOG2E))


def _rms_scale(x):
    return lax.rsqrt(jnp.mean(x * x, axis=-1, keepdims=True) + EPS)


def _run_interleaved(*gens):
    live = list(gens)
    while live:
        for g in list(live):
            try:
                next(g)
            except StopIteration:
                live.remove(g)


def _chain(gens):
    for g in gens:
        yield from g


def _attn_kernel(*refs, ns, tls, chunk, use_cache):
    if use_cache:
        (x_ref, g_ref, win_ref, wout_ref, bias_ref, ck_ref, cv_ref,
         x1_ref, kout_ref, vout_ref,
         q_s, gate_s, klo_s, khi_s, v_s, og_s) = refs
    else:
        (x_ref, g_ref, win_ref, wout_ref, bias_ref,
         x1_ref, kout_ref, vout_ref,
         q_s, gate_s, klo_s, khi_s, v_s, og_s) = refs
    t = pl.program_id(1)
    rows = ns * tls
    keys = WINDOW + chunk
    kvw = 2 * N_KV_HEADS * HEAD_DIM

    lane_lo = lax.broadcasted_iota(jnp.int32, (1, kvw), 1) % LANES < HEAD_DIM

    x = x_ref[...].reshape(rows, D_MODEL)
    h = (x * _rms_scale(x) * g_ref[...]).astype(BF16)
    q_s[...] = _dot(h, win_ref[:, 0:1024]).astype(BF16)
    kd = _dot(h, win_ref[:, 1024:1024 + kvw])
    vd = _dot(h, win_ref[:, 1024 + kvw:1024 + 2 * kvw])
    gate_s[...] = _dot(h, win_ref[:, 1024 + 2 * kvw:])

    if use_cache:
        ck = ck_ref[...]
        cv = cv_ref[...]
        klo_s[:, 0:WINDOW, :] = jnp.where(lane_lo, ck, 0.0).astype(BF16)
        khi_s[:, 0:WINDOW, :] = jnp.where(lane_lo, 0.0, ck).astype(BF16)
        v_s[:, 0:WINDOW, :] = cv.astype(BF16)
    else:
        @pl.when(t == 0)
        def _():
            zeros = jnp.zeros((ns, WINDOW, kvw), BF16)
            klo_s[:, 0:WINDOW, :] = zeros
            khi_s[:, 0:WINDOW, :] = zeros
            v_s[:, 0:WINDOW, :] = zeros

    kd3 = kd.reshape(ns, tls, kvw)
    klo_s[:, WINDOW:, :] = jnp.where(lane_lo, kd3, 0.0).astype(BF16)
    khi_s[:, WINDOW:, :] = jnp.where(lane_lo, 0.0, kd3).astype(BF16)
    v_s[:, WINDOW:, :] = vd.reshape(ns, tls, kvw).astype(BF16)

    if use_cache:
        kout_ref[:, 0:WINDOW - tls, :] = ck_ref[:, tls:, :]
        vout_ref[:, 0:WINDOW - tls, :] = cv_ref[:, tls:, :]
        kout_ref[:, WINDOW - tls:, :] = kd3
        vout_ref[:, WINDOW - tls:, :] = vd.reshape(ns, tls, kvw)
    else:
        @pl.when(t == pl.num_programs(1) - 1)
        def _():
            kout_ref[...] = kd3[:, tls - WINDOW:, :]
            vout_ref[...] = vd.reshape(ns, tls, kvw)[:, tls - WINDOW:, :]

    pad = MXU_DIM - keys
    col = lax.broadcasted_iota(jnp.int32, (chunk, MXU_DIM), 1)
    pair_lo = lax.broadcasted_iota(jnp.int32, (chunk, LANES), 1) < HEAD_DIM
    ones_blk = jnp.ones((keys, LANES), BF16)
    k_pad = jnp.zeros((pad, LANES), BF16)
    v_pad = jnp.concatenate([jnp.zeros((pad, LANES), BF16), jnp.ones((pad, LANES), BF16)], axis=1)
    n_chunks = tls // chunk

    def attend(s, c):
        r0 = c * chunk
        f0 = s * tls + r0
        masked = (not use_cache) and r0 < WINDOW
        if masked:
            n_inv = jnp.maximum(0, WINDOW - (t * tls + r0))
            valid = col >= n_inv
        scores = []
        for j in range(N_KV_HEADS):
            qj = q_s[f0:f0 + chunk, j * 256:(j + 1) * 256]
            lhs = jnp.concatenate([qj[:, :LANES], qj[:, LANES:]], axis=0)
            klo = jnp.concatenate([klo_s[s, r0:r0 + keys, j * LANES:(j + 1) * LANES], k_pad], axis=0)
            khi = jnp.concatenate([khi_s[s, r0:r0 + keys, j * LANES:(j + 1) * LANES], k_pad], axis=0)
            s_even = _dot_nt(lhs, klo)
            s_odd = _dot_nt(lhs, khi)
            scores.append((s_even[:chunk], s_odd[:chunk], s_even[chunk:], s_odd[chunk:]))
        yield
        ovs = []
        for j in range(N_KV_HEADS):
            ps = []
            for g in range(4):
                sg = scores[j][g] + bias_ref[4 * j + g]
                if masked:
                    sg = jnp.where(valid, sg, -jnp.inf)
                m = jnp.max(sg, axis=-1, keepdims=True)
                ps.append(jnp.exp2(sg - m).astype(BF16))
            p_all = jnp.concatenate(ps, axis=0)
            vj = v_s[s, r0:r0 + keys, j * LANES:(j + 1) * LANES]
            vaug = jnp.concatenate([jnp.concatenate([vj, ones_blk], axis=1), v_pad], axis=0)
            ovs.append(_dot(p_all, vaug))
        yield
        for j in range(N_KV_HEADS):
            outs = []
            for g in range(4):
                blk = ovs[j][g * chunk:(g + 1) * chunk]
                outs.append(blk[:, :LANES] / blk[:, LANES:])
            o01 = jnp.where(pair_lo, outs[0], outs[1])
            o23 = jnp.where(pair_lo, outs[2], outs[3])
            gt = gate_s[f0:f0 + chunk, j * 256:(j + 1) * 256]
            og = jnp.concatenate([o01, o23], axis=1) * (gt * _sigmoid(gt))
            og_s[f0:f0 + chunk, j * 256:(j + 1) * 256] = og.astype(BF16)
        yield

    units = [(s, c) for s in range(ns) for c in range(n_chunks)]
    for i in range(0, len(units), 2):
        _run_interleaved(*[attend(s, c) for s, c in units[i:i + 2]])

    if not use_cache:
        klo_s[:, 0:WINDOW, :] = klo_s[:, tls:tls + WINDOW, :]
        khi_s[:, 0:WINDOW, :] = khi_s[:, tls:tls + WINDOW, :]
        v_s[:, 0:WINDOW, :] = v_s[:, tls:tls + WINDOW, :]

    y = _dot(og_s[...], wout_ref[...])
    x1_ref[...] = (x + y).reshape(ns, tls, D_MODEL)


def _attn_bias(chunk, sinks):
    keys = WINDOW + chunk
    slopes = (2.0 ** (-8.0 * np.arange(1, N_HEADS + 1, dtype=np.float32) / N_HEADS)).astype(np.float32)
    i = np.arange(chunk, dtype=np.float32)[:, None]
    j = np.arange(keys, dtype=np.float32)[None, :]
    dist = np.abs(i + WINDOW - j).astype(np.float32)
    alibi = jnp.asarray(-(slopes[:, None, None] * dist[None]), F32) * LOG2E
    sink_col = jnp.broadcast_to((sinks.astype(F32) * LOG2E)[:, None, None], (N_HEADS, chunk, 1))
    rest = jnp.full((N_HEADS, chunk, MXU_DIM - keys - 1), -jnp.inf, F32)
    return jnp.concatenate([alibi, sink_col, rest], axis=2)


def _attn_layer(x, g, w_in_prep, w_out, sinks, *, ns, tls, chunk, cache=None):
    nb, seq, _ = x.shape
    use_cache = cache is not None
    grid = (nb // ns, seq // tls)
    kvw = 2 * N_KV_HEADS * HEAD_DIM
    rows = ns * tls
    const2 = lambda b, t: (0, 0)
    in_specs = [
        pl.BlockSpec((ns, tls, D_MODEL), lambda b, t: (b, t, 0)),
        pl.BlockSpec((1, D_MODEL), const2),
        pl.BlockSpec(w_in_prep.shape, const2, pipeline_mode=pl.Buffered(1)),
        pl.BlockSpec(w_out.shape, const2, pipeline_mode=pl.Buffered(1)),
        pl.BlockSpec((N_HEADS, chunk, MXU_DIM), lambda b, t: (0, 0, 0), pipeline_mode=pl.Buffered(1)),
    ]
    args = [x, g, w_in_prep, w_out, _attn_bias(chunk, sinks)]
    if use_cache:
        in_specs += [pl.BlockSpec((ns, WINDOW, kvw), lambda b, t: (b, 0, 0))] * 2
        args += list(cache)
    out_shape = (
        jax.ShapeDtypeStruct((nb, seq, D_MODEL), F32),
        jax.ShapeDtypeStruct((nb, WINDOW, kvw), F32),
        jax.ShapeDtypeStruct((nb, WINDOW, kvw), F32),
    )
    out_specs = (
        pl.BlockSpec((ns, tls, D_MODEL), lambda b, t: (b, t, 0)),
        pl.BlockSpec((ns, WINDOW, kvw), lambda b, t: (b, 0, 0)),
        pl.BlockSpec((ns, WINDOW, kvw), lambda b, t: (b, 0, 0)),
    )
    scratch = [
        pltpu.VMEM((rows, D_MODEL), BF16),
        pltpu.VMEM((rows, D_MODEL), F32),
        pltpu.VMEM((ns, WINDOW + tls, kvw), BF16),
        pltpu.VMEM((ns, WINDOW + tls, kvw), BF16),
        pltpu.VMEM((ns, WINDOW + tls, kvw), BF16),
        pltpu.VMEM((rows, D_MODEL), BF16),
    ]
    return pl.pallas_call(
        functools.partial(_attn_kernel, ns=ns, tls=tls, chunk=chunk, use_cache=use_cache),
        grid=grid, in_specs=in_specs, out_specs=out_specs, out_shape=out_shape,
        scratch_shapes=scratch,
        compiler_params=pltpu.CompilerParams(
            dimension_semantics=("arbitrary", "arbitrary"),
            vmem_limit_bytes=VMEM_LIMIT_BYTES),
        name="attn_cache" if use_cache else "attn_prompt",
    )(*args)


def _split3(x):
    hi = x.astype(BF16)
    r1 = x - hi.astype(F32)
    mid = r1.astype(BF16)
    lo = (r1 - mid.astype(F32)).astype(BF16)
    return hi, mid, lo


def _dn_kernel(*refs, ns, tls, chunk, use_state):
    rm = not use_state
    if use_state:
        (x_ref, g_ref, w1_ref, w2_ref, cw_ref, alog_ref, dtb_ref, ng_ref, wout_ref, fg_ref,
         tri_ref, cst_ref, sst_ref,
         y_ref, cout_ref, s_ref,
         cbuf, qn_s, kn_s, v_s, z_s, bt_s, gc_s, o_s, on_s,
         qg_s, kd_s, u_s, w_s, intra_s, sb_s) = refs
    else:
        (x_ref, g_ref, w1_ref, w2_ref, cw_ref, alog_ref, dtb_ref, ng_ref, wout_ref, fg_ref,
         tri_ref,
         y_ref, cout_ref, s_ref,
         cbuf, qn_s, kn_s, v_s, z_s, bt_s, gc_s, o_s, on_s,
         qg_s, kd_s, u_s, w_s, intra_s, sb_s, hist_s) = refs
        assert ns == 1 and chunk == SUBLANES * SUBLANES
    t = pl.program_id(1)
    rows = ns * tls
    hist = CONV_WIDTH - 1
    qk_w = DN_HEADS * DN_DIM
    conv_w = 3 * qk_w
    n_pairs = DN_HEADS // 2
    rres = tls // SUBLANES

    if use_state:
        s_ref[...] = sst_ref[...]
        cbuf[:, CONV_PAD - hist:CONV_PAD, :] = cst_ref[...]
        sb_s[...] = jnp.zeros(sb_s.shape, BF16)
        for s in range(ns):
            for hd in range(DN_HEADS):
                d0 = (hd % 2) * DN_DIM
                sb_s[s, hd // 2, d0:d0 + DN_DIM, d0:d0 + DN_DIM] = sst_ref[s, hd].astype(BF16)
    else:
        @pl.when(t == 0)
        def _():
            s_ref[...] = jnp.zeros(s_ref.shape, F32)
            sb_s[...] = jnp.zeros(sb_s.shape, BF16)
            hist_s[...] = jnp.zeros(hist_s.shape, F32)

    if rm:
        x = jnp.concatenate([x_ref[0, :, r * D_MODEL:(r + 1) * D_MODEL] for r in range(SUBLANES)], axis=0)
    else:
        x = x_ref[...].reshape(rows, D_MODEL)
    h = (x * _rms_scale(x) * g_ref[...]).astype(BF16)

    def phase_a():
        for c0 in range(0, conv_w, MXU_DIM):
            raw = _dot(h, w1_ref[:, c0:c0 + MXU_DIM])
            if rm:
                cbuf[:, c0:c0 + MXU_DIM] = raw
            else:
                cbuf[:, CONV_PAD:, c0:c0 + MXU_DIM] = raw.reshape(ns, tls, MXU_DIM)
            yield
        for c0 in range(0, qk_w, MXU_DIM):
            z_s[:, c0:c0 + MXU_DIM] = _dot(h, w1_ref[:, conv_w + c0:conv_w + c0 + MXU_DIM])
            yield
        ba = _dot(h, w2_ref[...])
        bt_s[...] = _sigmoid(ba[:, :LANES])
        a = ba[:, LANES:] + dtb_ref[...]
        softplus = jnp.maximum(a, 0.0) + jnp.log1p(jnp.exp(-jnp.abs(a)))
        g = -jnp.exp(alog_ref[...]) * softplus
        g_hi, g_mid, g_lo = _split3(g)
        gcs = _dot(tri_ref[...], jnp.concatenate([g_hi, g_mid, g_lo], axis=1))
        gc_s[...] = gcs[:, :LANES] + gcs[:, LANES:2 * LANES] + gcs[:, 2 * LANES:]
        yield

    def conv_block(s, cols):
        if not rm:
            acc = None
            for j in range(CONV_WIDTH):
                lo = CONV_PAD - hist + j
                term = cbuf[s, lo:lo + tls, cols] * cw_ref[j:j + 1, cols]
                acc = term if acc is None else acc + term
            return acc
        xb = cbuf[:, cols]
        wrapped = []
        for k in range(hist):
            r = SUBLANES - hist + k
            full = jnp.concatenate([hist_s[k, :, cols], xb[r * rres:(r + 1) * rres]], axis=0)
            wrapped.append(full[SUBLANES - 1:SUBLANES - 1 + rres])
        ext = jnp.concatenate(wrapped + [xb], axis=0)
        acc = None
        for j in range(CONV_WIDTH):
            term = ext[j * rres:j * rres + tls] * cw_ref[j:j + 1, cols]
            acc = term if acc is None else acc + term
        return acc

    def phase_b():
        for blk in range(conv_w // LANES):
            cols = slice(blk * LANES, (blk + 1) * LANES)
            seg, hb = divmod(blk, DN_HEADS)
            dst = slice(hb * LANES, (hb + 1) * LANES)
            for s in range(ns):
                acc = conv_block(s, cols)
                yv = acc * _sigmoid(acc)
                rs = slice(s * tls, (s + 1) * tls)
                if seg == 2:
                    v_s[rs, dst] = yv
                else:
                    inv = lax.rsqrt(jnp.sum(yv * yv, axis=-1, keepdims=True) + EPS)
                    if seg == 0:
                        qn_s[rs, dst] = yv * inv * (DN_DIM ** -0.5)
                    else:
                        kn_s[rs, dst] = yv * inv
            yield

    ga, gb = phase_a(), phase_b()
    for _ in range(qk_w // MXU_DIM):
        next(ga)
    _run_interleaved(ga, gb)

    if rm:
        for k in range(hist):
            r = SUBLANES - hist + k
            last = cbuf[(r + 1) * rres - SUBLANES:(r + 1) * rres, :]
            hist_s[k] = last
            cout_ref[0, k:k + 1, :] = last[SUBLANES - 1:SUBLANES]
    else:
        cout_ref[...] = cbuf[:, CONV_PAD + tls - hist:CONV_PAD + tls, :]

    hg = MXU_DIM // chunk
    n_groups = DN_HEADS // hg
    n_levels = int(round(math.log2(chunk)))
    ri = lax.broadcasted_iota(jnp.int32, (MXU_DIM, MXU_DIM), 0)
    ci = lax.broadcasted_iota(jnp.int32, (MXU_DIM, MXU_DIM), 1)

    def time_of(pos):
        p = pos % chunk
        return SUBLANES * (p % SUBLANES) + p // SUBLANES if rm else p

    tri_mask = (ri // chunk == ci // chunk) & (time_of(ri) >= time_of(ci))
    diag = ri == ci
    eye = jnp.where(diag, 1.0, 0.0).astype(F32)
    lane = lax.broadcasted_iota(jnp.int32, (MXU_DIM, LANES), 1)
    n_chunks = tls // chunk

    def hcols(hd):
        return slice(hd * LANES, (hd + 1) * LANES)

    def crows(ref, s, c, cols):
        if rm:
            return jnp.concatenate(
                [ref[r * rres + SUBLANES * c:r * rres + SUBLANES * (c + 1), cols] for r in range(SUBLANES)], axis=0)
        f0 = s * tls + c * chunk
        return ref[f0:f0 + chunk, cols]

    def crow_last(ref, s, c, cols):
        r = (SUBLANES - 1) * rres + SUBLANES * c + SUBLANES - 1 if rm else s * tls + (c + 1) * chunk - 1
        return ref[r:r + 1, cols]

    def phase_c(groups):
        st = []
        for (s, c, grp) in groups:
            f0 = s * tls + c * chunk
            rsl = slice(f0, f0 + chunk)
            heads = list(range(grp * hg, (grp + 1) * hg))
            k_h, q_h, rhs_rows, gc_cols, beta_cols = [], [], [], [], []
            for hd in heads:
                beta_h = crows(bt_s, s, c, slice(hd, hd + 1))
                gc_h = crows(gc_s, s, c, slice(hd, hd + 1))
                gl_h = crow_last(gc_s, s, c, slice(hd, hd + 1))
                eg = jnp.exp(gc_h)
                kf = crows(kn_s, s, c, hcols(hd))
                qf = crows(qn_s, s, c, hcols(hd))
                vb = (crows(v_s, s, c, hcols(hd)) * beta_h).astype(BF16)
                kbg = (kf * (beta_h * eg)).astype(BF16)
                qg_s[rsl, hcols(hd)] = (qf * eg).astype(BF16)
                kd_s[rsl, hcols(hd)] = (kf * jnp.exp(gl_h - gc_h)).astype(BF16)
                k_h.append(kf.astype(BF16))
                q_h.append(qf.astype(BF16))
                rhs_rows.append(jnp.concatenate([vb, kbg], axis=1))
                gc_cols.append(gc_h)
                beta_cols.append(beta_h)
            gc_c = jnp.concatenate(gc_cols, axis=0)
            hi, mid, lo = (p.astype(F32) for p in _split3(gc_c))
            lmat = jnp.where(lane < 3, 1.0,
                             jnp.where(lane == 3, hi, jnp.where(lane == 4, mid, jnp.where(lane == 5, lo, 0.0))))
            rmat = jnp.where(lane == 0, -hi,
                             jnp.where(lane == 1, -mid,
                                       jnp.where(lane == 2, -lo, jnp.where(lane < 6, 1.0, 0.0))))
            k_st = jnp.concatenate(k_h, axis=0)
            q_st = jnp.concatenate(q_h, axis=0)
            st.append(dict(
                rsl=rsl, heads=heads, gi=(s * n_chunks + c) * n_groups + grp,
                rhs=jnp.concatenate(rhs_rows, axis=0),
                beta=jnp.concatenate(beta_cols, axis=0),
                gd=_dot_nt(lmat.astype(BF16), rmat.astype(BF16)),
                kq=_dot_nt(jnp.concatenate([k_st, q_st], axis=0), k_st)))
        yield
        for d in st:
            decay = jnp.exp(jnp.where(tri_mask, d.pop('gd'), -jnp.inf))
            kq = d.pop('kq')
            m = jnp.where(diag, 0.0, kq[:MXU_DIM] * decay * d.pop('beta'))
            intra_s[d['gi']] = (kq[MXU_DIM:] * decay).astype(BF16)
            mb = m.astype(BF16)
            d['p'] = eye - m
            d['q'] = _dot(mb, mb)
        yield
        for lvl in range(1, n_levels):
            for d in st:
                qb = d['q'].astype(BF16)
                if lvl < n_levels - 1:
                    pq = _dot(jnp.concatenate([d['p'].astype(BF16), qb], axis=0), qb)
                    d['p'] = d['p'] + pq[:MXU_DIM]
                    d['q'] = pq[MXU_DIM:]
                else:
                    d['p'] = d['p'] + _dot(d['p'].astype(BF16), qb)
            yield
        for d in st:
            uw = _dot(d['p'].astype(BF16), d['rhs'])
            for i, hd in enumerate(d['heads']):
                hs = slice(i * chunk, (i + 1) * chunk)
                u_s[d['rsl'], hcols(hd)] = uw[hs, :LANES]
                w_s[d['rsl'], hcols(hd)] = uw[hs, LANES:].astype(BF16)
        yield

    def phase_d(s, c):
        f0 = s * tls + c * chunk
        rsl = slice(f0, f0 + chunk)
        both = []
        for p in range(n_pairs):
            pc = slice(p * MXU_DIM, (p + 1) * MXU_DIM)
            lhs = jnp.concatenate([w_s[rsl, pc], qg_s[rsl, pc]], axis=0)
            both.append(_dot(lhs, sb_s[s, p]))
        yield
        vnb, outer = [], []
        for p in range(n_pairs):
            pc = slice(p * MXU_DIM, (p + 1) * MXU_DIM)
            v_new = (u_s[rsl, pc] - both[p][:chunk]).astype(BF16)
            for i in range(2):
                hd = 2 * p + i
                vh = v_new[:, i * LANES:(i + 1) * LANES]
                vnb.append(vh)
                outer.append(_dot_tn(kd_s[rsl, hcols(hd)], vh))
        for grp in range(n_groups):
            heads = list(range(grp * hg, (grp + 1) * hg))
            v_st = jnp.concatenate([vnb[hd] for hd in heads], axis=0)
            qs_st = jnp.concatenate(
                [both[hd // 2][chunk:, (hd % 2) * LANES:(hd % 2 + 1) * LANES] for hd in heads], axis=0)
            o_st = qs_st + _dot(intra_s[(s * n_chunks + c) * n_groups + grp], v_st)
            for i, hd in enumerate(heads):
                o_s[rsl, hcols(hd)] = o_st[i * chunk:(i + 1) * chunk]
        yield
        for hd in range(DN_HEADS):
            gl = crow_last(gc_s, s, c, slice(hd, hd + 1))
            s_new = s_ref[s, hd] * jnp.exp(gl) + outer[hd]
            s_ref[s, hd] = s_new
            d0 = (hd % 2) * DN_DIM
            sb_s[s, hd // 2, d0:d0 + DN_DIM, d0:d0 + DN_DIM] = s_new.astype(BF16)
        yield

    def phase_e(r0, r1):
        rsl = slice(r0, r1)
        for hd in range(DN_HEADS):
            o = o_s[rsl, hcols(hd)]
            if rm:
                zz = jnp.concatenate([crows(z_s, 0, c, hcols(hd)) for c in range(r0 // chunk, r1 // chunk)], axis=0)
            else:
                zz = z_s[rsl, hcols(hd)]
            on_s[rsl, hcols(hd)] = (o * _rms_scale(o) * ng_ref[...] * (zz * _sigmoid(zz))).astype(BF16)
            yield
        if rm:
            xr = jnp.concatenate(
                [x_ref[0, SUBLANES * c:SUBLANES * (c + 1), r * D_MODEL:(r + 1) * D_MODEL]
                 for c in range(r0 // chunk, r1 // chunk) for r in range(SUBLANES)], axis=0)
        else:
            xr = x_ref[r0 // tls:r1 // tls].reshape(r1 - r0, D_MODEL)
        x2 = xr + _dot(on_s[rsl, :], wout_ref[...])
        yv = x2 * _rms_scale(x2) * fg_ref[...]
        if rm:
            for i, c in enumerate(range(r0 // chunk, r1 // chunk)):
                for r in range(SUBLANES):
                    p0 = i * chunk + r * SUBLANES
                    y_ref[0, SUBLANES * c:SUBLANES * (c + 1), r * D_MODEL:(r + 1) * D_MODEL] = yv[p0:p0 + SUBLANES]
        else:
            y_ref[r0 // tls:r1 // tls] = yv.reshape((r1 - r0) // tls, tls, D_MODEL)
        yield

    units = [(s, c) for s in range(ns) for c in range(n_chunks)]
    upb = max(1, 4 // n_groups)
    batches = [units[i:i + upb] for i in range(0, len(units), upb)]
    brows = upb * chunk

    def c_of(batch):
        return phase_c([(s, c, grp) for (s, c) in batch for grp in range(n_groups)])

    def d_of(batch):
        if use_state:
            return [phase_d(s, c) for (s, c) in batch]
        return [_chain([phase_d(s, c) for (s, c) in batch])]

    _run_interleaved(c_of(batches[0]))
    for i in range(1, len(batches)):
        extra = [phase_e((i - 2) * brows, (i - 1) * brows)] if i >= 2 else []
        _run_interleaved(c_of(batches[i]), *d_of(batches[i - 1]), *extra)
    last = len(batches) - 1
    extra = [phase_e((last - 1) * brows, last * brows)] if last >= 1 else []
    _run_interleaved(*d_of(batches[last]), *extra)
    _run_interleaved(phase_e(last * brows, (last + 1) * brows))


def _block_tri(rows, chunk, residue_major):
    q = np.arange(rows)
    tm = SUBLANES * (q % (rows // SUBLANES)) + q // (rows // SUBLANES) if residue_major else q
    m = (tm[:, None] // chunk == tm[None, :] // chunk) & (tm[:, None] >= tm[None, :])
    return jnp.asarray(m.astype(np.float32), BF16)


def _dn_layer(x, g, w1, w2, conv_w, alog, dtb, ng, w_out, fg, *, ns, tls, chunk, state=None):
    nb, seq, _ = x.shape
    use_state = state is not None
    grid = (nb // ns, seq // tls)
    rows = ns * tls
    hist = CONV_WIDTH - 1
    conv_dim = 3 * DN_HEADS * DN_DIM
    n_stacks = rows * DN_HEADS // MXU_DIM
    const2 = lambda b, t: (0, 0)
    tri = _block_tri(rows, chunk, not use_state)
    args = [x, g, w1, w2, conv_w, alog, dtb, ng, w_out, fg, tri]
    if use_state:
        x_spec = pl.BlockSpec((ns, tls, D_MODEL), lambda b, t: (b, t, 0))
        y_shape = (nb, seq, D_MODEL)
    else:
        args[0] = x.reshape(nb, seq // SUBLANES, SUBLANES * D_MODEL)
        x_spec = pl.BlockSpec((1, tls // SUBLANES, SUBLANES * D_MODEL), lambda b, t: (b, t, 0))
        y_shape = (nb, seq // SUBLANES, SUBLANES * D_MODEL)
    in_specs = [x_spec]
    in_specs += [pl.BlockSpec(a.shape, const2, pipeline_mode=pl.Buffered(1)) for a in args[1:]]
    if use_state:
        in_specs += [
            pl.BlockSpec((ns, hist, conv_dim), lambda b, t: (b, 0, 0)),
            pl.BlockSpec((ns, DN_HEADS, DN_DIM, DN_DIM), lambda b, t: (b, 0, 0, 0)),
        ]
        args += list(state)
    out_shape = (
        jax.ShapeDtypeStruct(y_shape, F32),
        jax.ShapeDtypeStruct((nb, hist, conv_dim), F32),
        jax.ShapeDtypeStruct((nb, DN_HEADS, DN_DIM, DN_DIM), F32),
    )
    out_specs = (
        x_spec,
        pl.BlockSpec((ns, hist, conv_dim), lambda b, t: (b, 0, 0)),
        pl.BlockSpec((ns, DN_HEADS, DN_DIM, DN_DIM), lambda b, t: (b, 0, 0, 0)),
    )
    if use_state:
        cbuf = pltpu.VMEM((ns, CONV_PAD + tls, conv_dim), F32)
    else:
        cbuf = pltpu.VMEM((tls, conv_dim), F32)
    scratch = [
        cbuf,
        pltpu.VMEM((rows, D_MODEL), F32),
        pltpu.VMEM((rows, D_MODEL), F32),
        pltpu.VMEM((rows, D_MODEL), F32),
        pltpu.VMEM((rows, D_MODEL), F32),
        pltpu.VMEM((rows, LANES), F32),
        pltpu.VMEM((rows, LANES), F32),
        pltpu.VMEM((rows, D_MODEL), F32),
        pltpu.VMEM((rows, D_MODEL), BF16),
        pltpu.VMEM((rows, D_MODEL), BF16),
        pltpu.VMEM((rows, D_MODEL), BF16),
        pltpu.VMEM((rows, D_MODEL), F32),
        pltpu.VMEM((rows, D_MODEL), BF16),
        pltpu.VMEM((n_stacks, MXU_DIM, MXU_DIM), BF16),
        pltpu.VMEM((ns, DN_HEADS // 2, MXU_DIM, MXU_DIM), BF16),
    ]
    if not use_state:
        scratch.append(pltpu.VMEM((hist, SUBLANES, conv_dim), F32))
    y, cout, sout = pl.pallas_call(
        functools.partial(_dn_kernel, ns=ns, tls=tls, chunk=chunk, use_state=use_state),
        grid=grid, in_specs=in_specs, out_specs=out_specs, out_shape=out_shape,
        scratch_shapes=scratch,
        compiler_params=pltpu.CompilerParams(
            dimension_semantics=("arbitrary", "arbitrary"),
            vmem_limit_bytes=VMEM_LIMIT_BYTES),
        name="dn_state" if use_state else "dn_prompt",
    )(*args)
    return y.reshape(nb, seq, D_MODEL), cout, sout


def _dup_heads(t):
    lead = t.shape[:-1]
    t4 = t.reshape(lead + (N_KV_HEADS, 1, HEAD_DIM))
    return jnp.broadcast_to(t4, lead + (N_KV_HEADS, 2, HEAD_DIM)).reshape(lead + (2 * N_KV_HEADS * HEAD_DIM,))


def _undup_heads(t):
    lead = t.shape[:-1]
    return t.reshape(lead + (N_KV_HEADS, 2, HEAD_DIM))[..., 0, :]


def _pad_lanes(v):
    return jnp.zeros((1, LANES), F32).at[0, :v.shape[0]].set(v)


def kernel(x_prompt, x_sample, cache_k, cache_v, state_conv, state_ssm, norm_g, final_norm_g,
           attn_w_in, attn_sinks, attn_w_out, dn_w_in, dn_conv_w, dn_a_log, dn_dt_bias,
           dn_norm_g, dn_w_out):
    attn_width = N_HEADS * HEAD_DIM
    kv_width = N_KV_HEADS * HEAD_DIM
    w = attn_w_in[0]
    wq = w[:, :attn_width] * (HEAD_DIM ** -0.5 * LOG2E)
    wk = _dup_heads(w[:, attn_width:attn_width + kv_width])
    wv = _dup_heads(w[:, attn_width + kv_width:attn_width + 2 * kv_width])
    wg = w[:, attn_width + 2 * kv_width:]
    a_win = jnp.concatenate([wq, wk, wv, wg], axis=1).astype(BF16)
    a_wout = attn_w_out[0].astype(BF16)
    g0 = norm_g[0].reshape(1, D_MODEL)
    sinks = attn_sinks[0]

    conv_dim = 3 * DN_HEADS * DN_DIM
    dw = dn_w_in[0]
    d_w1 = dw[:, :conv_dim + DN_HEADS * DN_DIM].astype(BF16)
    tail = dw[:, conv_dim + DN_HEADS * DN_DIM:]
    d_w2 = jnp.zeros((D_MODEL, 2 * LANES), F32)
    d_w2 = d_w2.at[:, :DN_HEADS].set(tail[:, :DN_HEADS]).at[:, LANES:LANES + DN_HEADS].set(tail[:, DN_HEADS:])
    d_w2 = d_w2.astype(BF16)
    d_wout = dn_w_out[0].astype(BF16)
    g1 = norm_g[1].reshape(1, D_MODEL)
    fg = final_norm_g.reshape(1, D_MODEL)
    alog = _pad_lanes(dn_a_log[0])
    dtb = _pad_lanes(dn_dt_bias[0])
    ng = dn_norm_g[0].reshape(1, DN_DIM)
    dn_args = (g1, d_w1, d_w2, dn_conv_w[0], alog, dtb, ng, d_wout, fg)

    tl_p = 512
    x1p, kp, vp = _attn_layer(x_prompt, g0, a_win, a_wout, sinks, ns=1, tls=tl_p, chunk=CHUNK)
    yp, cp, sp = _dn_layer(x1p, *dn_args, ns=1, tls=tl_p, chunk=CHUNK)

    dec_seq = x_sample.shape[1]
    ns_s = 8
    ck = _dup_heads(cache_k[0].reshape(cache_k.shape[1], WINDOW, kv_width))
    cv = _dup_heads(cache_v[0].reshape(cache_v.shape[1], WINDOW, kv_width))
    x1s, ks, vs = _attn_layer(x_sample, g0, a_win, a_wout, sinks, ns=ns_s, tls=dec_seq,
                              chunk=min(CHUNK, dec_seq), cache=(ck, cv))
    ys, cs, ss = _dn_layer(x1s, *dn_args, ns=ns_s, tls=dec_seq, chunk=min(CHUNK, dec_seq),
                           state=(state_conv[0], state_ssm[0]))

    def kv_out(t):
        return _undup_heads(t).reshape(t.shape[0], WINDOW, N_KV_HEADS, HEAD_DIM)[None]

    return (yp, ys, kv_out(kp), kv_out(vp), cp[None], sp[None],
            kv_out(ks), kv_out(vs), cs[None], ss[None])
```

```python
import functools
import math

import numpy as np
import jax
import jax.numpy as jnp
from jax import lax
from jax.experimental import pallas as pl
from jax.experimental.pallas import tpu as pltpu

F32 = jnp.float32
BF16 = jnp.bfloat16

D_MODEL = 1024
EPS = 1e-6
CHUNK = 64
WINDOW = 128
N_HEADS = 16
N_KV_HEADS = 4
HEAD_DIM = 64
DN_HEADS = 8
DN_DIM = 128
CONV_WIDTH = 4
CONV_PAD = 8
SUBLANES = 8
LANES = 128
MXU_DIM = 256
VMEM_LIMIT_BYTES = 56 * 1024 * 1024
LOG2E = 1.4426950408889634

_NT = (((1,), (1,)), ((), ()))
_TN = (((0,), (0,)), ((), ()))


def _dot(a, b):
    return jnp.dot(a, b, preferred_element_type=F32)


def _dot_nt(a, b):
    return lax.dot_general(a, b, _NT, preferred_element_type=F32)


def _dot_tn(a, b):
    return lax.dot_general(a, b, _TN, preferred_element_type=F32)


def _sigmoid(x):
    return 1.0 / (1.0 + jnp.exp2(x * -LOG2E))


def _rms_scale(x):
    return lax.rsqrt(jnp.mean(x * x, axis=-1, keepdims=True) + EPS)


def _run_interleaved(*gens):
    live = list(gens)
    while live:
        for g in list(live):
            try:
                next(g)
            except StopIteration:
                live.remove(g)


def _chain(gens):
    for g in gens:
        yield from g


def _attn_kernel(*refs, ns, tls, chunk, use_cache):
    if use_cache:
        (x_ref, g_ref, win_ref, wout_ref, bias_ref, ck_ref, cv_ref,
         x1_ref, kout_ref, vout_ref,
         q_s, gate_s, klo_s, khi_s, v_s, og_s) = refs
    else:
        (x_ref, g_ref, win_ref, wout_ref, bias_ref,
         x1_ref, kout_ref, vout_ref,
         q_s, gate_s, klo_s, khi_s, v_s, og_s) = refs
    t = pl.program_id(1)
    rows = ns * tls
    keys = WINDOW + chunk
    kvw = 2 * N_KV_HEADS * HEAD_DIM

    lane_lo = lax.broadcasted_iota(jnp.int32, (1, kvw), 1) % LANES < HEAD_DIM

    x = x_ref[...].reshape(rows, D_MODEL)
    h = (x * _rms_scale(x) * g_ref[...]).astype(BF16)
    q_s[...] = _dot(h, win_ref[:, 0:1024]).astype(BF16)
    kd = _dot(h, win_ref[:, 1024:1024 + kvw])
    vd = _dot(h, win_ref[:, 1024 + kvw:1024 + 2 * kvw])
    gate_s[...] = _dot(h, win_ref[:, 1024 + 2 * kvw:])

    if use_cache:
        ck = ck_ref[...]
        cv = cv_ref[...]
        klo_s[:, 0:WINDOW, :] = jnp.where(lane_lo, ck, 0.0).astype(BF16)
        khi_s[:, 0:WINDOW, :] = jnp.where(lane_lo, 0.0, ck).astype(BF16)
        v_s[:, 0:WINDOW, :] = cv.astype(BF16)
    else:
        @pl.when(t == 0)
        def _():
            zeros = jnp.zeros((ns, WINDOW, kvw), BF16)
            klo_s[:, 0:WINDOW, :] = zeros
            khi_s[:, 0:WINDOW, :] = zeros
            v_s[:, 0:WINDOW, :] = zeros

    kd3 = kd.reshape(ns, tls, kvw)
    klo_s[:, WINDOW:, :] = jnp.where(lane_lo, kd3, 0.0).astype(BF16)
    khi_s[:, WINDOW:, :] = jnp.where(lane_lo, 0.0, kd3).astype(BF16)
    v_s[:, WINDOW:, :] = vd.reshape(ns, tls, kvw).astype(BF16)

    if use_cache:
        kout_ref[:, 0:WINDOW - tls, :] = ck_ref[:, tls:, :]
        vout_ref[:, 0:WINDOW - tls, :] = cv_ref[:, tls:, :]
        kout_ref[:, WINDOW - tls:, :] = kd3
        vout_ref[:, WINDOW - tls:, :] = vd.reshape(ns, tls, kvw)
    else:
        @pl.when(t == pl.num_programs(1) - 1)
        def _():
            kout_ref[...] = kd3[:, tls - WINDOW:, :]
            vout_ref[...] = vd.reshape(ns, tls, kvw)[:, tls - WINDOW:, :]

    pad = MXU_DIM - keys
    col = lax.broadcasted_iota(jnp.int32, (chunk, MXU_DIM), 1)
    pair_lo = lax.broadcasted_iota(jnp.int32, (chunk, LANES), 1) < HEAD_DIM
    ones_blk = jnp.ones((keys, LANES), BF16)
    k_pad = jnp.zeros((pad, LANES), BF16)
    v_pad = jnp.concatenate([jnp.zeros((pad, LANES), BF16), jnp.ones((pad, LANES), BF16)], axis=1)
    n_chunks = tls // chunk

    def attend(s, c):
        r0 = c * chunk
        f0 = s * tls + r0
        masked = (not use_cache) and r0 < WINDOW
        if masked:
            n_inv = jnp.maximum(0, WINDOW - (t * tls + r0))
            valid = col >= n_inv
        scores = []
        for j in range(N_KV_HEADS):
            qj = q_s[f0:f0 + chunk, j * 256:(j + 1) * 256]
            lhs = jnp.concatenate([qj[:, :LANES], qj[:, LANES:]], axis=0)
            klo = jnp.concatenate([klo_s[s, r0:r0 + keys, j * LANES:(j + 1) * LANES], k_pad], axis=0)
            khi = jnp.concatenate([khi_s[s, r0:r0 + keys, j * LANES:(j + 1) * LANES], k_pad], axis=0)
            s_even = _dot_nt(lhs, klo)
            s_odd = _dot_nt(lhs, khi)
            scores.append((s_even[:chunk], s_odd[:chunk], s_even[chunk:], s_odd[chunk:]))
        yield
        ovs = []
        for j in range(N_KV_HEADS):
            ps = []
            for g in range(4):
                sg = scores[j][g] + bias_ref[4 * j + g]
                if masked:
                    sg = jnp.where(valid, sg, -jnp.inf)
                m = jnp.max(sg, axis=-1, keepdims=True)
                ps.append(jnp.exp2(sg - m).astype(BF16))
            p_all = jnp.concatenate(ps, axis=0)
            vj = v_s[s, r0:r0 + keys, j * LANES:(j + 1) * LANES]
            vaug = jnp.concatenate([jnp.concatenate([vj, ones_blk], axis=1), v_pad], axis=0)
            ovs.append(_dot(p_all, vaug))
        yield
        for j in range(N_KV_HEADS):
            outs = []
            for g in range(4):
                blk = ovs[j][g * chunk:(g + 1) * chunk]
                outs.append(blk[:, :LANES] / blk[:, LANES:])
            o01 = jnp.where(pair_lo, outs[0], outs[1])
            o23 = jnp.where(pair_lo, outs[2], outs[3])
            gt = gate_s[f0:f0 + chunk, j * 256:(j + 1) * 256]
            og = jnp.concatenate([o01, o23], axis=1) * (gt * _sigmoid(gt))
            og_s[f0:f0 + chunk, j * 256:(j + 1) * 256] = og.astype(BF16)
        yield

    units = [(s, c) for s in range(ns) for c in range(n_chunks)]
    for i in range(0, len(units), 2):
        _run_interleaved(*[attend(s, c) for s, c in units[i:i + 2]])

    if not use_cache:
        klo_s[:, 0:WINDOW, :] = klo_s[:, tls:tls + WINDOW, :]
        khi_s[:, 0:WINDOW, :] = khi_s[:, tls:tls + WINDOW, :]
        v_s[:, 0:WINDOW, :] = v_s[:, tls:tls + WINDOW, :]

    y = _dot(og_s[...], wout_ref[...])
    x1_ref[...] = (x + y).reshape(ns, tls, D_MODEL)


def _attn_bias(chunk, sinks):
    keys = WINDOW + chunk
    slopes = (2.0 ** (-8.0 * np.arange(1, N_HEADS + 1, dtype=np.float32) / N_HEADS)).astype(np.float32)
    i = np.arange(chunk, dtype=np.float32)[:, None]
    j = np.arange(keys, dtype=np.float32)[None, :]
    dist = np.abs(i + WINDOW - j).astype(np.float32)
    alibi = jnp.asarray(-(slopes[:, None, None] * dist[None]), F32) * LOG2E
    sink_col = jnp.broadcast_to((sinks.astype(F32) * LOG2E)[:, None, None], (N_HEADS, chunk, 1))
    rest = jnp.full((N_HEADS, chunk, MXU_DIM - keys - 1), -jnp.inf, F32)
    return jnp.concatenate([alibi, sink_col, rest], axis=2)


def _attn_layer(x, g, w_in_prep, w_out, sinks, *, ns, tls, chunk, cache=None):
    nb, seq, _ = x.shape
    use_cache = cache is not None
    grid = (nb // ns, seq // tls)
    kvw = 2 * N_KV_HEADS * HEAD_DIM
    rows = ns * tls
    const2 = lambda b, t: (0, 0)
    in_specs = [
        pl.BlockSpec((ns, tls, D_MODEL), lambda b, t: (b, t, 0)),
        pl.BlockSpec((1, D_MODEL), const2),
        pl.BlockSpec(w_in_prep.shape, const2, pipeline_mode=pl.Buffered(1)),
        pl.BlockSpec(w_out.shape, const2, pipeline_mode=pl.Buffered(1)),
        pl.BlockSpec((N_HEADS, chunk, MXU_DIM), lambda b, t: (0, 0, 0), pipeline_mode=pl.Buffered(1)),
    ]
    args = [x, g, w_in_prep, w_out, _attn_bias(chunk, sinks)]
    if use_cache:
        in_specs += [pl.BlockSpec((ns, WINDOW, kvw), lambda b, t: (b, 0, 0))] * 2
        args += list(cache)
    out_shape = (
        jax.ShapeDtypeStruct((nb, seq, D_MODEL), F32),
        jax.ShapeDtypeStruct((nb, WINDOW, kvw), F32),
        jax.ShapeDtypeStruct((nb, WINDOW, kvw), F32),
    )
    out_specs = (
        pl.BlockSpec((ns, tls, D_MODEL), lambda b, t: (b, t, 0)),
        pl.BlockSpec((ns, WINDOW, kvw), lambda b, t: (b, 0, 0)),
        pl.BlockSpec((ns, WINDOW, kvw), lambda b, t: (b, 0, 0)),
    )
    scratch = [
        pltpu.VMEM((rows, D_MODEL), BF16),
        pltpu.VMEM((rows, D_MODEL), F32),
        pltpu.VMEM((ns, WINDOW + tls, kvw), BF16),
        pltpu.VMEM((ns, WINDOW + tls, kvw), BF16),
        pltpu.VMEM((ns, WINDOW + tls, kvw), BF16),
        pltpu.VMEM((rows, D_MODEL), BF16),
    ]
    return pl.pallas_call(
        functools.partial(_attn_kernel, ns=ns, tls=tls, chunk=chunk, use_cache=use_cache),
        grid=grid, in_specs=in_specs, out_specs=out_specs, out_shape=out_shape,
        scratch_shapes=scratch,
        compiler_params=pltpu.CompilerParams(
            dimension_semantics=("arbitrary", "arbitrary"),
            vmem_limit_bytes=VMEM_LIMIT_BYTES),
        name="attn_cache" if use_cache else "attn_prompt",
    )(*args)


def _split3(x):
    hi = x.astype(BF16)
    r1 = x - hi.astype(F32)
    mid = r1.astype(BF16)
    lo = (r1 - mid.astype(F32)).astype(BF16)
    return hi, mid, lo


def _dn_kernel(*refs, ns, tls, chunk, use_state):
    rm = not use_state
    if use_state:
        (x_ref, g_ref, w1_ref, w2_ref, cw_ref, alog_ref, dtb_ref, ng_ref, wout_ref, fg_ref,
         tri_ref, cst_ref, sst_ref,
         y_ref, cout_ref, s_ref,
         cbuf, qn_s, kn_s, v_s, z_s, bt_s, gc_s, o_s, on_s,
         qg_s, kd_s, u_s, w_s, intra_s, sb_s) = refs
    else:
        (x_ref, g_ref, w1_ref, w2_ref, cw_ref, alog_ref, dtb_ref, ng_ref, wout_ref, fg_ref,
         tri_ref,
         y_ref, cout_ref, s_ref,
         cbuf, qn_s, kn_s, v_s, z_s, bt_s, gc_s, o_s, on_s,
         qg_s, kd_s, u_s, w_s, intra_s, sb_s, hist_s, xs_s, ys_s) = refs
        assert ns == 1 and chunk == SUBLANES * SUBLANES
    t = pl.program_id(1)
    rows = ns * tls
    hist = CONV_WIDTH - 1
    qk_w = DN_HEADS * DN_DIM
    conv_w = 3 * qk_w
    n_pairs = DN_HEADS // 2
    rres = tls // SUBLANES

    if use_state:
        s_ref[...] = sst_ref[...]
        cbuf[:, CONV_PAD - hist:CONV_PAD, :] = cst_ref[...]
        sb_s[...] = jnp.zeros(sb_s.shape, BF16)
        for s in range(ns):
            for hd in range(DN_HEADS):
                d0 = (hd % 2) * DN_DIM
                sb_s[s, hd // 2, d0:d0 + DN_DIM, d0:d0 + DN_DIM] = sst_ref[s, hd].astype(BF16)
    else:
        @pl.when(t == 0)
        def _():
            s_ref[...] = jnp.zeros(s_ref.shape, F32)
            sb_s[...] = jnp.zeros(sb_s.shape, BF16)
            hist_s[...] = jnp.zeros(hist_s.shape, F32)

    n_slabs = D_MODEL // LANES
    if rm:
        for j in range(n_slabs):
            xs_s[j * tls:(j + 1) * tls, :] = x_ref[0, :, j * LANES:(j + 1) * LANES]
        x = jnp.concatenate(
            [jnp.concatenate([xs_s[pl.ds(j * tls + r, rres, stride=SUBLANES), :] for j in range(n_slabs)], axis=1)
             for r in range(SUBLANES)], axis=0)
    else:
        x = x_ref[...].reshape(rows, D_MODEL)
    h = (x * _rms_scale(x) * g_ref[...]).astype(BF16)

    def phase_a():
        for c0 in range(0, conv_w, MXU_DIM):
            raw = _dot(h, w1_ref[:, c0:c0 + MXU_DIM])
            if rm:
                cbuf[:, c0:c0 + MXU_DIM] = raw
            else:
                cbuf[:, CONV_PAD:, c0:c0 + MXU_DIM] = raw.reshape(ns, tls, MXU_DIM)
            yield
        for c0 in range(0, qk_w, MXU_DIM):
            z_s[:, c0:c0 + MXU_DIM] = _dot(h, w1_ref[:, conv_w + c0:conv_w + c0 + MXU_DIM])
            yield
        ba = _dot(h, w2_ref[...])
        bt_s[...] = _sigmoid(ba[:, :LANES])
        a = ba[:, LANES:] + dtb_ref[...]
        softplus = jnp.maximum(a, 0.0) + jnp.log1p(jnp.exp(-jnp.abs(a)))
        g = -jnp.exp(alog_ref[...]) * softplus
        g_hi, g_mid, g_lo = _split3(g)
        gcs = _dot(tri_ref[...], jnp.concatenate([g_hi, g_mid, g_lo], axis=1))
        gc_s[...] = gcs[:, :LANES] + gcs[:, LANES:2 * LANES] + gcs[:, 2 * LANES:]
        yield

    def conv_block(s, cols):
        if not rm:
            acc = None
            for j in range(CONV_WIDTH):
                lo = CONV_PAD - hist + j
                term = cbuf[s, lo:lo + tls, cols] * cw_ref[j:j + 1, cols]
                acc = term if acc is None else acc + term
            return acc
        xb = cbuf[:, cols]
        wrapped = []
        for k in range(hist):
            r = SUBLANES - hist + k
            full = jnp.concatenate([hist_s[k, :, cols], xb[r * rres:(r + 1) * rres]], axis=0)
            wrapped.append(full[SUBLANES - 1:SUBLANES - 1 + rres])
        ext = jnp.concatenate(wrapped + [xb], axis=0)
        acc = None
        for j in range(CONV_WIDTH):
            term = ext[j * rres:j * rres + tls] * cw_ref[j:j + 1, cols]
            acc = term if acc is None else acc + term
        return acc

    def phase_b():
        for blk in range(conv_w // LANES):
            cols = slice(blk * LANES, (blk + 1) * LANES)
            seg, hb = divmod(blk, DN_HEADS)
            dst = slice(hb * LANES, (hb + 1) * LANES)
            for s in range(ns):
                acc = conv_block(s, cols)
                yv = acc * _sigmoid(acc)
                rs = slice(s * tls, (s + 1) * tls)
                if seg == 2:
                    v_s[rs, dst] = yv
                else:
                    inv = lax.rsqrt(jnp.sum(yv * yv, axis=-1, keepdims=True) + EPS)
                    if seg == 0:
                        qn_s[rs, dst] = yv * inv * (DN_DIM ** -0.5)
                    else:
                        kn_s[rs, dst] = yv * inv
            yield

    ga, gb = phase_a(), phase_b()
    for _ in range(qk_w // MXU_DIM):
        next(ga)
    _run_interleaved(ga, gb)

    if rm:
        for k in range(hist):
            r = SUBLANES - hist + k
            last = cbuf[(r + 1) * rres - SUBLANES:(r + 1) * rres, :]
            hist_s[k] = last
            cout_ref[0, k:k + 1, :] = last[SUBLANES - 1:SUBLANES]
    else:
        cout_ref[...] = cbuf[:, CONV_PAD + tls - hist:CONV_PAD + tls, :]

    hg = MXU_DIM // chunk
    n_groups = DN_HEADS // hg
    n_levels = int(round(math.log2(chunk)))
    ri = lax.broadcasted_iota(jnp.int32, (MXU_DIM, MXU_DIM), 0)
    ci = lax.broadcasted_iota(jnp.int32, (MXU_DIM, MXU_DIM), 1)

    def time_of(pos):
        p = pos % chunk
        return SUBLANES * (p % SUBLANES) + p // SUBLANES if rm else p

    tri_mask = (ri // chunk == ci // chunk) & (time_of(ri) >= time_of(ci))
    diag = ri == ci
    eye = jnp.where(diag, 1.0, 0.0).astype(F32)
    lane = lax.broadcasted_iota(jnp.int32, (MXU_DIM, LANES), 1)
    n_chunks = tls // chunk

    def hcols(hd):
        return slice(hd * LANES, (hd + 1) * LANES)

    def crows(ref, s, c, cols):
        if rm:
            return jnp.concatenate(
                [ref[r * rres + SUBLANES * c:r * rres + SUBLANES * (c + 1), cols] for r in range(SUBLANES)], axis=0)
        f0 = s * tls + c * chunk
        return ref[f0:f0 + chunk, cols]

    def crow_last(ref, s, c, cols):
        r = (SUBLANES - 1) * rres + SUBLANES * c + SUBLANES - 1 if rm else s * tls + (c + 1) * chunk - 1
        return ref[r:r + 1, cols]

    def phase_c(groups):
        st = []
        for (s, c, grp) in groups:
            f0 = s * tls + c * chunk
            rsl = slice(f0, f0 + chunk)
            heads = list(range(grp * hg, (grp + 1) * hg))
            k_h, q_h, rhs_rows, gc_cols, beta_cols = [], [], [], [], []
            for hd in heads:
                beta_h = crows(bt_s, s, c, slice(hd, hd + 1))
                gc_h = crows(gc_s, s, c, slice(hd, hd + 1))
                gl_h = crow_last(gc_s, s, c, slice(hd, hd + 1))
                eg = jnp.exp(gc_h)
                kf = crows(kn_s, s, c, hcols(hd))
                qf = crows(qn_s, s, c, hcols(hd))
                vb = (crows(v_s, s, c, hcols(hd)) * beta_h).astype(BF16)
                kbg = (kf * (beta_h * eg)).astype(BF16)
                qg_s[rsl, hcols(hd)] = (qf * eg).astype(BF16)
                kd_s[rsl, hcols(hd)] = (kf * jnp.exp(gl_h - gc_h)).astype(BF16)
                k_h.append(kf.astype(BF16))
                q_h.append(qf.astype(BF16))
                rhs_rows.append(jnp.concatenate([vb, kbg], axis=1))
                gc_cols.append(gc_h)
                beta_cols.append(beta_h)
            gc_c = jnp.concatenate(gc_cols, axis=0)
            hi, mid, lo = (p.astype(F32) for p in _split3(gc_c))
            lmat = jnp.where(lane < 3, 1.0,
                             jnp.where(lane == 3, hi, jnp.where(lane == 4, mid, jnp.where(lane == 5, lo, 0.0))))
            rmat = jnp.where(lane == 0, -hi,
                             jnp.where(lane == 1, -mid,
                                       jnp.where(lane == 2, -lo, jnp.where(lane < 6, 1.0, 0.0))))
            k_st = jnp.concatenate(k_h, axis=0)
            q_st = jnp.concatenate(q_h, axis=0)
            st.append(dict(
                rsl=rsl, heads=heads, gi=(s * n_chunks + c) * n_groups + grp,
                rhs=jnp.concatenate(rhs_rows, axis=0),
                beta=jnp.concatenate(beta_cols, axis=0),
                gd=_dot_nt(lmat.astype(BF16), rmat.astype(BF16)),
                kq=_dot_nt(jnp.concatenate([k_st, q_st], axis=0), k_st)))
        yield
        for d in st:
            decay = jnp.exp(jnp.where(tri_mask, d.pop('gd'), -jnp.inf))
            kq = d.pop('kq')
            m = jnp.where(diag, 0.0, kq[:MXU_DIM] * decay * d.pop('beta'))
            intra_s[d['gi']] = (kq[MXU_DIM:] * decay).astype(BF16)
            mb = m.astype(BF16)
            d['p'] = eye - m
            d['q'] = _dot(mb, mb)
        yield
        for lvl in range(1, n_levels):
            for d in st:
                qb = d['q'].astype(BF16)
                if lvl < n_levels - 1:
                    pq = _dot(jnp.concatenate([d['p'].astype(BF16), qb], axis=0), qb)
                    d['p'] = d['p'] + pq[:MXU_DIM]
                    d['q'] = pq[MXU_DIM:]
                else:
                    d['p'] = d['p'] + _dot(d['p'].astype(BF16), qb)
            yield
        for d in st:
            uw = _dot(d['p'].astype(BF16), d['rhs'])
            for i, hd in enumerate(d['heads']):
                hs = slice(i * chunk, (i + 1) * chunk)
                u_s[d['rsl'], hcols(hd)] = uw[hs, :LANES]
                w_s[d['rsl'], hcols(hd)] = uw[hs, LANES:].astype(BF16)
        yield

    def phase_d(s, c):
        f0 = s * tls + c * chunk
        rsl = slice(f0, f0 + chunk)
        both = []
        for p in range(n_pairs):
            pc = slice(p * MXU_DIM, (p + 1) * MXU_DIM)
            lhs = jnp.concatenate([w_s[rsl, pc], qg_s[rsl, pc]], axis=0)
            both.append(_dot(lhs, sb_s[s, p]))
        yield
        vnb, outer = [], []
        for p in range(n_pairs):
            pc = slice(p * MXU_DIM, (p + 1) * MXU_DIM)
            v_new = (u_s[rsl, pc] - both[p][:chunk]).astype(BF16)
            for i in range(2):
                hd = 2 * p + i
                vh = v_new[:, i * LANES:(i + 1) * LANES]
                vnb.append(vh)
                outer.append(_dot_tn(kd_s[rsl, hcols(hd)], vh))
        for grp in range(n_groups):
            heads = list(range(grp * hg, (grp + 1) * hg))
            v_st = jnp.concatenate([vnb[hd] for hd in heads], axis=0)
            qs_st = jnp.concatenate(
                [both[hd // 2][chunk:, (hd % 2) * LANES:(hd % 2 + 1) * LANES] for hd in heads], axis=0)
            o_st = qs_st + _dot(intra_s[(s * n_chunks + c) * n_groups + grp], v_st)
            for i, hd in enumerate(heads):
                o_s[rsl, hcols(hd)] = o_st[i * chunk:(i + 1) * chunk]
        yield
        for hd in range(DN_HEADS):
            gl = crow_last(gc_s, s, c, slice(hd, hd + 1))
            s_new = s_ref[s, hd] * jnp.exp(gl) + outer[hd]
            s_ref[s, hd] = s_new
            d0 = (hd % 2) * DN_DIM
            sb_s[s, hd // 2, d0:d0 + DN_DIM, d0:d0 + DN_DIM] = s_new.astype(BF16)
        yield

    def phase_e(r0, r1):
        rsl = slice(r0, r1)
        for hd in range(DN_HEADS):
            o = o_s[rsl, hcols(hd)]
            if rm:
                zz = jnp.concatenate([crows(z_s, 0, c, hcols(hd)) for c in range(r0 // chunk, r1 // chunk)], axis=0)
            else:
                zz = z_s[rsl, hcols(hd)]
            on_s[rsl, hcols(hd)] = (o * _rms_scale(o) * ng_ref[...] * (zz * _sigmoid(zz))).astype(BF16)
            yield
        y1 = _dot(on_s[rsl, :], wout_ref[...])
        if rm:
            for i, c in enumerate(range(r0 // chunk, r1 // chunk)):
                for r in range(SUBLANES):
                    p0 = i * chunk + r * SUBLANES
                    for j in range(n_slabs):
                        ys_s[pl.ds(j * tls + c * chunk + r, SUBLANES, stride=SUBLANES), :] = (
                            y1[p0:p0 + SUBLANES, j * LANES:(j + 1) * LANES])
            y1 = jnp.concatenate([ys_s[j * tls + r0:j * tls + r1, :] for j in range(n_slabs)], axis=1)
            x2 = x_ref[0, r0:r1, :] + y1
            y_ref[0, r0:r1, :] = x2 * _rms_scale(x2) * fg_ref[...]
        else:
            x2 = x_ref[r0 // tls:r1 // tls].reshape(r1 - r0, D_MODEL) + y1
            yv = x2 * _rms_scale(x2) * fg_ref[...]
            y_ref[r0 // tls:r1 // tls] = yv.reshape((r1 - r0) // tls, tls, D_MODEL)
        yield

    units = [(s, c) for s in range(ns) for c in range(n_chunks)]
    upb = max(1, 4 // n_groups)
    batches = [units[i:i + upb] for i in range(0, len(units), upb)]
    brows = upb * chunk

    def c_of(batch):
        return phase_c([(s, c, grp) for (s, c) in batch for grp in range(n_groups)])

    def d_of(batch):
        if use_state:
            return [phase_d(s, c) for (s, c) in batch]
        return [_chain([phase_d(s, c) for (s, c) in batch])]

    _run_interleaved(c_of(batches[0]))
    for i in range(1, len(batches)):
        extra = [phase_e((i - 2) * brows, (i - 1) * brows)] if i >= 2 else []
        _run_interleaved(c_of(batches[i]), *d_of(batches[i - 1]), *extra)
    last = len(batches) - 1
    extra = [phase_e((last - 1) * brows, last * brows)] if last >= 1 else []
    _run_interleaved(*d_of(batches[last]), *extra)
    _run_interleaved(phase_e(last * brows, (last + 1) * brows))


def _block_tri(rows, chunk, residue_major):
    q = np.arange(rows)
    tm = SUBLANES * (q % (rows // SUBLANES)) + q // (rows // SUBLANES) if residue_major else q
    m = (tm[:, None] // chunk == tm[None, :] // chunk) & (tm[:, None] >= tm[None, :])
    return jnp.asarray(m.astype(np.float32), BF16)


def _dn_layer(x, g, w1, w2, conv_w, alog, dtb, ng, w_out, fg, *, ns, tls, chunk, state=None):
    nb, seq, _ = x.shape
    use_state = state is not None
    grid = (nb // ns, seq // tls)
    rows = ns * tls
    hist = CONV_WIDTH - 1
    conv_dim = 3 * DN_HEADS * DN_DIM
    n_stacks = rows * DN_HEADS // MXU_DIM
    const2 = lambda b, t: (0, 0)
    tri = _block_tri(rows, chunk, not use_state)
    args = [x, g, w1, w2, conv_w, alog, dtb, ng, w_out, fg, tri]
    x_spec = pl.BlockSpec((ns, tls, D_MODEL), lambda b, t: (b, t, 0))
    in_specs = [x_spec]
    in_specs += [pl.BlockSpec(a.shape, const2, pipeline_mode=pl.Buffered(1)) for a in args[1:]]
    if use_state:
        in_specs += [
            pl.BlockSpec((ns, hist, conv_dim), lambda b, t: (b, 0, 0)),
            pl.BlockSpec((ns, DN_HEADS, DN_DIM, DN_DIM), lambda b, t: (b, 0, 0, 0)),
        ]
        args += list(state)
    out_shape = (
        jax.ShapeDtypeStruct((nb, seq, D_MODEL), F32),
        jax.ShapeDtypeStruct((nb, hist, conv_dim), F32),
        jax.ShapeDtypeStruct((nb, DN_HEADS, DN_DIM, DN_DIM), F32),
    )
    out_specs = (
        x_spec,
        pl.BlockSpec((ns, hist, conv_dim), lambda b, t: (b, 0, 0)),
        pl.BlockSpec((ns, DN_HEADS, DN_DIM, DN_DIM), lambda b, t: (b, 0, 0, 0)),
    )
    if use_state:
        cbuf = pltpu.VMEM((ns, CONV_PAD + tls, conv_dim), F32)
    else:
        cbuf = pltpu.VMEM((tls, conv_dim), F32)
    scratch = [
        cbuf,
        pltpu.VMEM((rows, D_MODEL), F32),
        pltpu.VMEM((rows, D_MODEL), F32),
        pltpu.VMEM((rows, D_MODEL), F32),
        pltpu.VMEM((rows, D_MODEL), F32),
        pltpu.VMEM((rows, LANES), F32),
        pltpu.VMEM((rows, LANES), F32),
        pltpu.VMEM((rows, D_MODEL), F32),
        pltpu.VMEM((rows, D_MODEL), BF16),
        pltpu.VMEM((rows, D_MODEL), BF16),
        pltpu.VMEM((rows, D_MODEL), BF16),
        pltpu.VMEM((rows, D_MODEL), F32),
        pltpu.VMEM((rows, D_MODEL), BF16),
        pltpu.VMEM((n_stacks, MXU_DIM, MXU_DIM), BF16),
        pltpu.VMEM((ns, DN_HEADS // 2, MXU_DIM, MXU_DIM), BF16),
    ]
    if not use_state:
        scratch.append(pltpu.VMEM((hist, SUBLANES, conv_dim), F32))
        scratch.append(pltpu.VMEM((D_MODEL // LANES * tls, LANES), F32))
        scratch.append(pltpu.VMEM((D_MODEL // LANES * tls, LANES), F32))
    return pl.pallas_call(
        functools.partial(_dn_kernel, ns=ns, tls=tls, chunk=chunk, use_state=use_state),
        grid=grid, in_specs=in_specs, out_specs=out_specs, out_shape=out_shape,
        scratch_shapes=scratch,
        compiler_params=pltpu.CompilerParams(
            dimension_semantics=("arbitrary", "arbitrary"),
            vmem_limit_bytes=VMEM_LIMIT_BYTES),
        name="dn_state" if use_state else "dn_prompt",
    )(*args)


def _dup_heads(t):
    lead = t.shape[:-1]
    t4 = t.reshape(lead + (N_KV_HEADS, 1, HEAD_DIM))
    return jnp.broadcast_to(t4, lead + (N_KV_HEADS, 2, HEAD_DIM)).reshape(lead + (2 * N_KV_HEADS * HEAD_DIM,))


def _undup_heads(t):
    lead = t.shape[:-1]
    return t.reshape(lead + (N_KV_HEADS, 2, HEAD_DIM))[..., 0, :]


def _pad_lanes(v):
    return jnp.zeros((1, LANES), F32).at[0, :v.shape[0]].set(v)


def kernel(x_prompt, x_sample, cache_k, cache_v, state_conv, state_ssm, norm_g, final_norm_g,
           attn_w_in, attn_sinks, attn_w_out, dn_w_in, dn_conv_w, dn_a_log, dn_dt_bias,
           dn_norm_g, dn_w_out):
    attn_width = N_HEADS * HEAD_DIM
    kv_width = N_KV_HEADS * HEAD_DIM
    w = attn_w_in[0]
    wq = w[:, :attn_width] * (HEAD_DIM ** -0.5 * LOG2E)
    wk = _dup_heads(w[:, attn_width:attn_width + kv_width])
    wv = _dup_heads(w[:, attn_width + kv_width:attn_width + 2 * kv_width])
    wg = w[:, attn_width + 2 * kv_width:]
    a_win = jnp.concatenate([wq, wk, wv, wg], axis=1).astype(BF16)
    a_wout = attn_w_out[0].astype(BF16)
    g0 = norm_g[0].reshape(1, D_MODEL)
    sinks = attn_sinks[0]

    conv_dim = 3 * DN_HEADS * DN_DIM
    dw = dn_w_in[0]
    d_w1 = dw[:, :conv_dim + DN_HEADS * DN_DIM].astype(BF16)
    tail = dw[:, conv_dim + DN_HEADS * DN_DIM:]
    d_w2 = jnp.zeros((D_MODEL, 2 * LANES), F32)
    d_w2 = d_w2.at[:, :DN_HEADS].set(tail[:, :DN_HEADS]).at[:, LANES:LANES + DN_HEADS].set(tail[:, DN_HEADS:])
    d_w2 = d_w2.astype(BF16)
    d_wout = dn_w_out[0].astype(BF16)
    g1 = norm_g[1].reshape(1, D_MODEL)
    fg = final_norm_g.reshape(1, D_MODEL)
    alog = _pad_lanes(dn_a_log[0])
    dtb = _pad_lanes(dn_dt_bias[0])
    ng = dn_norm_g[0].reshape(1, DN_DIM)
    dn_args = (g1, d_w1, d_w2, dn_conv_w[0], alog, dtb, ng, d_wout, fg)

    tl_p = 512
    x1p, kp, vp = _attn_layer(x_prompt, g0, a_win, a_wout, sinks, ns=1, tls=tl_p, chunk=CHUNK)
    yp, cp, sp = _dn_layer(x1p, *dn_args, ns=1, tls=tl_p, chunk=CHUNK)

    dec_seq = x_sample.shape[1]
    ns_s = 8
    ck = _dup_heads(cache_k[0].reshape(cache_k.shape[1], WINDOW, kv_width))
    cv = _dup_heads(cache_v[0].reshape(cache_v.shape[1], WINDOW, kv_width))
    x1s, ks, vs = _attn_layer(x_sample, g0, a_win, a_wout, sinks, ns=ns_s, tls=dec_seq,
                              chunk=min(CHUNK, dec_seq), cache=(ck, cv))
    ys, cs, ss = _dn_layer(x1s, *dn_args, ns=ns_s, tls=dec_seq, chunk=min(CHUNK, dec_seq),
                           state=(state_conv[0], state_ssm[0]))

    def kv_out(t):
        return _undup_heads(t).reshape(t.shape[0], WINDOW, N_KV_HEADS, HEAD_DIM)[None]

    return (yp, ys, kv_out(kp), kv_out(vp), cp[None], sp[None],
            kv_out(ks), kv_out(vs), cs[None], ss[None])
```

```python
import functools
import math

import numpy as np
import jax
import jax.numpy as jnp
from jax import lax
from jax.experimental import pallas as pl
from jax.experimental.pallas import tpu as pltpu

F32 = jnp.float32
BF16 = jnp.bfloat16

D_MODEL = 1024
EPS = 1e-6
CHUNK = 64
WINDOW = 128
N_HEADS = 16
N_KV_HEADS = 4
HEAD_DIM = 64
DN_HEADS = 8
DN_DIM = 128
CONV_WIDTH = 4
CONV_PAD = 8
SUBLANES = 8
LANES = 128
MXU_DIM = 256
VMEM_LIMIT_BYTES = 56 * 1024 * 1024
ATTN_TILE_ROWS = 1024
DN_TILE_ROWS = 512
RUN_SEQS_PER_STEP = 8
LOG2E = 1.4426950408889634

_NT = (((1,), (1,)), ((), ()))
_TN = (((0,), (0,)), ((), ()))


def _dot(a, b):
    return jnp.dot(a, b, preferred_element_type=F32)


def _dot_nt(a, b):
    return lax.dot_general(a, b, _NT, preferred_element_type=F32)


def _dot_tn(a, b):
    return lax.dot_general(a, b, _TN, preferred_element_type=F32)


def _sigmoid(x):
    return 1.0 / (1.0 + jnp.exp2(x * -LOG2E))


def _rms_scale(x):
    return lax.rsqrt(jnp.mean(x * x, axis=-1, keepdims=True) + EPS)


def _run_interleaved(*gens):
    live = list(gens)
    while live:
        for g in list(live):
            try:
                next(g)
            except StopIteration:
                live.remove(g)


def _chain(gens):
    for g in gens:
        yield from g


def _attn_kernel(*refs, ns, tls, chunk, use_cache):
    if use_cache:
        (x_ref, g_ref, win_ref, wout_ref, bias_ref, ck_ref, cv_ref,
         x1_ref, kout_ref, vout_ref,
         q_s, gate_s, klo_s, khi_s, v_s, og_s) = refs
    else:
        (x_ref, g_ref, win_ref, wout_ref, bias_ref,
         x1_ref, kout_ref, vout_ref,
         q_s, gate_s, klo_s, khi_s, v_s, og_s) = refs
    t = pl.program_id(1)
    rows = ns * tls
    keys = WINDOW + chunk
    kvn = N_KV_HEADS * HEAD_DIM
    kvw = 2 * kvn

    half_lo = lax.broadcasted_iota(jnp.int32, (1, LANES), 1) < HEAD_DIM

    x = x_ref[...].reshape(rows, D_MODEL)
    h = (x * _rms_scale(x) * g_ref[...]).astype(BF16)
    q_s[...] = _dot(h, win_ref[:, 0:1024]).astype(BF16)
    kd = _dot(h, win_ref[:, 1024:1024 + kvn])
    vd = _dot(h, win_ref[:, 1024 + kvn:1024 + 2 * kvn])
    gate_s[...] = _dot(h, win_ref[:, 1024 + 2 * kvn:])

    def spread(k2, v2, nrows, row0):
        for b in range(N_KV_HEADS // 2):
            kb = k2[:, b * LANES:(b + 1) * LANES]
            vb = v2[:, b * LANES:(b + 1) * LANES]
            kr = pltpu.roll(kb, HEAD_DIM, 1)
            vr = pltpu.roll(vb, HEAD_DIM, 1)
            forms = (
                (klo_s, jnp.where(half_lo, kb, 0.0), jnp.where(half_lo, kr, 0.0)),
                (khi_s, jnp.where(half_lo, 0.0, kr), jnp.where(half_lo, 0.0, kb)),
                (v_s, jnp.where(half_lo, vb, vr), jnp.where(half_lo, vr, vb)),
            )
            for ref, even, odd in forms:
                for j, val in ((2 * b, even), (2 * b + 1, odd)):
                    ref[:, row0:row0 + nrows, j * LANES:(j + 1) * LANES] = (
                        val.astype(BF16).reshape(ns, nrows, LANES))

    if use_cache:
        spread(ck_ref[...].reshape(ns * WINDOW, kvn), cv_ref[...].reshape(ns * WINDOW, kvn), WINDOW, 0)
    else:
        @pl.when(t == 0)
        def _():
            zeros = jnp.zeros((ns, WINDOW, kvw), BF16)
            klo_s[:, 0:WINDOW, :] = zeros
            khi_s[:, 0:WINDOW, :] = zeros
            v_s[:, 0:WINDOW, :] = zeros

    spread(kd, vd, tls, WINDOW)

    kd3 = kd.reshape(ns, tls, kvn)
    vd3 = vd.reshape(ns, tls, kvn)
    if use_cache:
        kout_ref[:, 0:WINDOW - tls, :] = ck_ref[:, tls:, :]
        vout_ref[:, 0:WINDOW - tls, :] = cv_ref[:, tls:, :]
        kout_ref[:, WINDOW - tls:, :] = kd3
        vout_ref[:, WINDOW - tls:, :] = vd3
    else:
        @pl.when(t == pl.num_programs(1) - 1)
        def _():
            kout_ref[...] = kd3[:, tls - WINDOW:, :]
            vout_ref[...] = vd3[:, tls - WINDOW:, :]

    pad = MXU_DIM - keys
    col = lax.broadcasted_iota(jnp.int32, (chunk, MXU_DIM), 1)
    pair_lo = lax.broadcasted_iota(jnp.int32, (chunk, LANES), 1) < HEAD_DIM
    ones_blk = jnp.ones((keys, LANES), BF16)
    k_pad = jnp.zeros((pad, LANES), BF16)
    v_pad = jnp.concatenate([jnp.zeros((pad, LANES), BF16), jnp.ones((pad, LANES), BF16)], axis=1)
    n_chunks = tls // chunk

    def attend(s, c):
        r0 = c * chunk
        f0 = s * tls + r0
        masked = (not use_cache) and r0 < WINDOW
        if masked:
            n_inv = jnp.maximum(0, WINDOW - (t * tls + r0))
            valid = col >= n_inv
        scores = []
        for j in range(N_KV_HEADS):
            qj = q_s[f0:f0 + chunk, j * 256:(j + 1) * 256]
            lhs = jnp.concatenate([qj[:, :LANES], qj[:, LANES:]], axis=0)
            klo = jnp.concatenate([klo_s[s, r0:r0 + keys, j * LANES:(j + 1) * LANES], k_pad], axis=0)
            khi = jnp.concatenate([khi_s[s, r0:r0 + keys, j * LANES:(j + 1) * LANES], k_pad], axis=0)
            s_even = _dot_nt(lhs, klo)
            s_odd = _dot_nt(lhs, khi)
            scores.append((s_even[:chunk], s_odd[:chunk], s_even[chunk:], s_odd[chunk:]))
        yield
        ovs = []
        for j in range(N_KV_HEADS):
            ps = []
            for g in range(4):
                sg = scores[j][g] + bias_ref[4 * j + g]
                if masked:
                    sg = jnp.where(valid, sg, -jnp.inf)
                m = jnp.max(sg, axis=-1, keepdims=True)
                ps.append(jnp.exp2(sg - m).astype(BF16))
            p_all = jnp.concatenate(ps, axis=0)
            vj = v_s[s, r0:r0 + keys, j * LANES:(j + 1) * LANES]
            vaug = jnp.concatenate([jnp.concatenate([vj, ones_blk], axis=1), v_pad], axis=0)
            ovs.append(_dot(p_all, vaug))
        yield
        for j in range(N_KV_HEADS):
            outs = []
            for g in range(4):
                blk = ovs[j][g * chunk:(g + 1) * chunk]
                outs.append(blk[:, :LANES] / blk[:, LANES:])
            o01 = jnp.where(pair_lo, outs[0], outs[1])
            o23 = jnp.where(pair_lo, outs[2], outs[3])
            gt = gate_s[f0:f0 + chunk, j * 256:(j + 1) * 256]
            og = jnp.concatenate([o01, o23], axis=1) * (gt * _sigmoid(gt))
            og_s[f0:f0 + chunk, j * 256:(j + 1) * 256] = og.astype(BF16)
        yield

    units = [(s, c) for s in range(ns) for c in range(n_chunks)]
    for i in range(0, len(units), 2):
        _run_interleaved(*[attend(s, c) for s, c in units[i:i + 2]])

    if not use_cache:
        klo_s[:, 0:WINDOW, :] = klo_s[:, tls:tls + WINDOW, :]
        khi_s[:, 0:WINDOW, :] = khi_s[:, tls:tls + WINDOW, :]
        v_s[:, 0:WINDOW, :] = v_s[:, tls:tls + WINDOW, :]

    y = _dot(og_s[...], wout_ref[...])
    x1_ref[...] = (x + y).reshape(ns, tls, D_MODEL)


def _attn_bias(chunk, sinks):
    keys = WINDOW + chunk
    slopes = (2.0 ** (-8.0 * np.arange(1, N_HEADS + 1, dtype=np.float32) / N_HEADS)).astype(np.float32)
    i = np.arange(chunk, dtype=np.float32)[:, None]
    j = np.arange(keys, dtype=np.float32)[None, :]
    dist = np.abs(i + WINDOW - j).astype(np.float32)
    alibi = jnp.asarray(-(slopes[:, None, None] * dist[None]), F32) * LOG2E
    sink_col = jnp.broadcast_to((sinks.astype(F32) * LOG2E)[:, None, None], (N_HEADS, chunk, 1))
    rest = jnp.full((N_HEADS, chunk, MXU_DIM - keys - 1), -jnp.inf, F32)
    return jnp.concatenate([alibi, sink_col, rest], axis=2)


def _attn_layer(x, g, w_in_prep, w_out, sinks, *, ns, tls, chunk, cache=None):
    nb, seq, _ = x.shape
    use_cache = cache is not None
    grid = (nb // ns, seq // tls)
    kvn = N_KV_HEADS * HEAD_DIM
    kvw = 2 * kvn
    rows = ns * tls
    const2 = lambda b, t: (0, 0)
    in_specs = [
        pl.BlockSpec((ns, tls, D_MODEL), lambda b, t: (b, t, 0)),
        pl.BlockSpec((1, D_MODEL), const2),
        pl.BlockSpec(w_in_prep.shape, const2, pipeline_mode=pl.Buffered(1)),
        pl.BlockSpec(w_out.shape, const2, pipeline_mode=pl.Buffered(1)),
        pl.BlockSpec((N_HEADS, chunk, MXU_DIM), lambda b, t: (0, 0, 0), pipeline_mode=pl.Buffered(1)),
    ]
    args = [x, g, w_in_prep, w_out, _attn_bias(chunk, sinks)]
    if use_cache:
        in_specs += [pl.BlockSpec((ns, WINDOW, kvn), lambda b, t: (b, 0, 0))] * 2
        args += list(cache)
    out_shape = (
        jax.ShapeDtypeStruct((nb, seq, D_MODEL), F32),
        jax.ShapeDtypeStruct((nb, WINDOW, kvn), F32),
        jax.ShapeDtypeStruct((nb, WINDOW, kvn), F32),
    )
    out_specs = (
        pl.BlockSpec((ns, tls, D_MODEL), lambda b, t: (b, t, 0)),
        pl.BlockSpec((ns, WINDOW, kvn), lambda b, t: (b, 0, 0)),
        pl.BlockSpec((ns, WINDOW, kvn), lambda b, t: (b, 0, 0)),
    )
    scratch = [
        pltpu.VMEM((rows, D_MODEL), BF16),
        pltpu.VMEM((rows, D_MODEL), F32),
        pltpu.VMEM((ns, WINDOW + tls, kvw), BF16),
        pltpu.VMEM((ns, WINDOW + tls, kvw), BF16),
        pltpu.VMEM((ns, WINDOW + tls, kvw), BF16),
        pltpu.VMEM((rows, D_MODEL), BF16),
    ]
    return pl.pallas_call(
        functools.partial(_attn_kernel, ns=ns, tls=tls, chunk=chunk, use_cache=use_cache),
        grid=grid, in_specs=in_specs, out_specs=out_specs, out_shape=out_shape,
        scratch_shapes=scratch,
        compiler_params=pltpu.CompilerParams(
            dimension_semantics=("arbitrary", "arbitrary"),
            vmem_limit_bytes=VMEM_LIMIT_BYTES),
        name="attn_cache" if use_cache else "attn_prompt",
    )(*args)


def _split3(x):
    hi = x.astype(BF16)
    r1 = x - hi.astype(F32)
    mid = r1.astype(BF16)
    lo = (r1 - mid.astype(F32)).astype(BF16)
    return hi, mid, lo


def _dn_kernel(*refs, ns, tls, chunk, use_state):
    rm = not use_state
    if use_state:
        (x_ref, g_ref, w1_ref, w2_ref, cw_ref, alog_ref, dtb_ref, ng_ref, wout_ref, fg_ref,
         tri_ref, cst_ref, sst_ref,
         y_ref, cout_ref, s_ref,
         cbuf, qn_s, kn_s, v_s, z_s, bt_s, gc_s, o_s, on_s,
         qg_s, kd_s, u_s, w_s, intra_s, sb_s) = refs
    else:
        (x_ref, g_ref, w1_ref, w2_ref, cw_ref, alog_ref, dtb_ref, ng_ref, wout_ref, fg_ref,
         tri_ref,
         y_ref, cout_ref, s_ref,
         cbuf, qn_s, kn_s, v_s, z_s, bt_s, gc_s, o_s, on_s,
         qg_s, kd_s, u_s, w_s, intra_s, sb_s, hist_s, xs_s, ys_s) = refs
        assert ns == 1 and chunk == SUBLANES * SUBLANES
    t = pl.program_id(1)
    rows = ns * tls
    hist = CONV_WIDTH - 1
    qk_w = DN_HEADS * DN_DIM
    conv_w = 3 * qk_w
    n_pairs = DN_HEADS // 2
    rres = tls // SUBLANES

    if use_state:
        s_ref[...] = sst_ref[...]
        cbuf[:, CONV_PAD - hist:CONV_PAD, :] = cst_ref[...]
        sb_s[...] = jnp.zeros(sb_s.shape, BF16)
        for s in range(ns):
            for hd in range(DN_HEADS):
                d0 = (hd % 2) * DN_DIM
                sb_s[s, hd // 2, d0:d0 + DN_DIM, d0:d0 + DN_DIM] = sst_ref[s, hd].astype(BF16)
    else:
        @pl.when(t == 0)
        def _():
            s_ref[...] = jnp.zeros(s_ref.shape, F32)
            sb_s[...] = jnp.zeros(sb_s.shape, BF16)
            hist_s[...] = jnp.zeros(hist_s.shape, F32)

    n_slabs = D_MODEL // LANES
    if rm:
        for j in range(n_slabs):
            xs_s[j * tls:(j + 1) * tls, :] = x_ref[0, :, j * LANES:(j + 1) * LANES]
        x = jnp.concatenate(
            [jnp.concatenate([xs_s[pl.ds(j * tls + r, rres, stride=SUBLANES), :] for j in range(n_slabs)], axis=1)
             for r in range(SUBLANES)], axis=0)
    else:
        x = x_ref[...].reshape(rows, D_MODEL)
    h = (x * _rms_scale(x) * g_ref[...]).astype(BF16)

    def phase_a():
        for c0 in range(0, conv_w, MXU_DIM):
            raw = _dot(h, w1_ref[:, c0:c0 + MXU_DIM])
            if rm:
                cbuf[:, c0:c0 + MXU_DIM] = raw
            else:
                cbuf[:, CONV_PAD:, c0:c0 + MXU_DIM] = raw.reshape(ns, tls, MXU_DIM)
            yield
        for c0 in range(0, qk_w, MXU_DIM):
            z_s[:, c0:c0 + MXU_DIM] = _dot(h, w1_ref[:, conv_w + c0:conv_w + c0 + MXU_DIM])
            yield
        ba = _dot(h, w2_ref[...])
        bt_s[...] = _sigmoid(ba[:, :LANES])
        a = ba[:, LANES:] + dtb_ref[...]
        softplus = jnp.maximum(a, 0.0) + jnp.log1p(jnp.exp(-jnp.abs(a)))
        g = -jnp.exp(alog_ref[...]) * softplus
        g_hi, g_mid, g_lo = _split3(g)
        gcs = _dot(tri_ref[...], jnp.concatenate([g_hi, g_mid, g_lo], axis=1))
        gc_s[...] = gcs[:, :LANES] + gcs[:, LANES:2 * LANES] + gcs[:, 2 * LANES:]
        yield

    def conv_block(s, cols):
        if not rm:
            acc = None
            for j in range(CONV_WIDTH):
                lo = CONV_PAD - hist + j
                term = cbuf[s, lo:lo + tls, cols] * cw_ref[j:j + 1, cols]
                acc = term if acc is None else acc + term
            return acc
        xb = cbuf[:, cols]
        wrapped = []
        for k in range(hist):
            r = SUBLANES - hist + k
            full = jnp.concatenate([hist_s[k, :, cols], xb[r * rres:(r + 1) * rres]], axis=0)
            wrapped.append(full[SUBLANES - 1:SUBLANES - 1 + rres])
        ext = jnp.concatenate(wrapped + [xb], axis=0)
        acc = None
        for j in range(CONV_WIDTH):
            term = ext[j * rres:j * rres + tls] * cw_ref[j:j + 1, cols]
            acc = term if acc is None else acc + term
        return acc

    def phase_b():
        for blk in range(conv_w // LANES):
            cols = slice(blk * LANES, (blk + 1) * LANES)
            seg, hb = divmod(blk, DN_HEADS)
            dst = slice(hb * LANES, (hb + 1) * LANES)
            for s in range(ns):
                acc = conv_block(s, cols)
                yv = acc * _sigmoid(acc)
                rs = slice(s * tls, (s + 1) * tls)
                if seg == 2:
                    v_s[rs, dst] = yv
                else:
                    inv = lax.rsqrt(jnp.sum(yv * yv, axis=-1, keepdims=True) + EPS)
                    if seg == 0:
                        qn_s[rs, dst] = yv * inv * (DN_DIM ** -0.5)
                    else:
                        kn_s[rs, dst] = yv * inv
            yield

    ga, gb = phase_a(), phase_b()
    for _ in range(qk_w // MXU_DIM):
        next(ga)
    _run_interleaved(ga, gb)

    if rm:
        for k in range(hist):
            r = SUBLANES - hist + k
            last = cbuf[(r + 1) * rres - SUBLANES:(r + 1) * rres, :]
            hist_s[k] = last
            cout_ref[0, k:k + 1, :] = last[SUBLANES - 1:SUBLANES]
    else:
        cout_ref[...] = cbuf[:, CONV_PAD + tls - hist:CONV_PAD + tls, :]

    hg = MXU_DIM // chunk
    n_groups = DN_HEADS // hg
    n_levels = int(round(math.log2(chunk)))
    ri = lax.broadcasted_iota(jnp.int32, (MXU_DIM, MXU_DIM), 0)
    ci = lax.broadcasted_iota(jnp.int32, (MXU_DIM, MXU_DIM), 1)

    def time_of(pos):
        p = pos % chunk
        return SUBLANES * (p % SUBLANES) + p // SUBLANES if rm else p

    tri_mask = (ri // chunk == ci // chunk) & (time_of(ri) >= time_of(ci))
    diag = ri == ci
    eye = jnp.where(diag, 1.0, 0.0).astype(F32)
    lane = lax.broadcasted_iota(jnp.int32, (MXU_DIM, LANES), 1)
    ones_rows = jnp.ones((2 * SUBLANES, LANES), BF16)
    n_chunks = tls // chunk

    def hcols(hd):
        return slice(hd * LANES, (hd + 1) * LANES)

    def crows(ref, s, c, cols):
        if rm:
            return jnp.concatenate(
                [ref[r * rres + SUBLANES * c:r * rres + SUBLANES * (c + 1), cols] for r in range(SUBLANES)], axis=0)
        f0 = s * tls + c * chunk
        return ref[f0:f0 + chunk, cols]

    def crow_last(ref, s, c, cols):
        r = (SUBLANES - 1) * rres + SUBLANES * c + SUBLANES - 1 if rm else s * tls + (c + 1) * chunk - 1
        return ref[r:r + 1, cols]

    def phase_c(groups):
        st = []
        for (s, c, grp) in groups:
            f0 = s * tls + c * chunk
            rsl = slice(f0, f0 + chunk)
            heads = list(range(grp * hg, (grp + 1) * hg))
            k_h, q_h, rhs_rows, gc_cols, beta_cols = [], [], [], [], []
            for hd in heads:
                beta_h = crows(bt_s, s, c, slice(hd, hd + 1))
                gc_h = crows(gc_s, s, c, slice(hd, hd + 1))
                gl_h = crow_last(gc_s, s, c, slice(hd, hd + 1))
                eg = jnp.exp(gc_h)
                kf = crows(kn_s, s, c, hcols(hd))
                qf = crows(qn_s, s, c, hcols(hd))
                vb = (crows(v_s, s, c, hcols(hd)) * beta_h).astype(BF16)
                kbg = (kf * (beta_h * eg)).astype(BF16)
                qg_s[rsl, hcols(hd)] = (qf * eg).astype(BF16)
                kd_s[rsl, hcols(hd)] = (kf * jnp.exp(gl_h - gc_h)).astype(BF16)
                k_h.append(kf.astype(BF16))
                q_h.append(qf.astype(BF16))
                rhs_rows.append(jnp.concatenate([vb, kbg], axis=1))
                gc_cols.append(gc_h)
                beta_cols.append(beta_h)
            gc_c = jnp.concatenate(gc_cols, axis=0)
            hi, mid, lo = (p.astype(F32) for p in _split3(gc_c))
            rmat = jnp.where(lane == 0, hi, jnp.where(lane == 1, mid, jnp.where(lane == 2, lo, 0.0)))
            gc_row = _dot_nt(ones_rows, rmat.astype(BF16))[0:1]
            k_st = jnp.concatenate(k_h, axis=0)
            q_st = jnp.concatenate(q_h, axis=0)
            st.append(dict(
                rsl=rsl, heads=heads, gi=(s * n_chunks + c) * n_groups + grp,
                rhs=jnp.concatenate(rhs_rows, axis=0),
                beta=jnp.concatenate(beta_cols, axis=0),
                gd=gc_c - gc_row,
                kq=_dot_nt(jnp.concatenate([k_st, q_st], axis=0), k_st)))
        yield
        for d in st:
            decay = jnp.exp(jnp.where(tri_mask, d.pop('gd'), -jnp.inf))
            kq = d.pop('kq')
            m = jnp.where(diag, 0.0, kq[:MXU_DIM] * decay * d.pop('beta'))
            intra_s[d['gi']] = (kq[MXU_DIM:] * decay).astype(BF16)
            mb = m.astype(BF16)
            d['p'] = eye - m
            d['q'] = _dot(mb, mb)
        yield
        for lvl in range(1, n_levels):
            for d in st:
                qb = d['q'].astype(BF16)
                if lvl < n_levels - 1:
                    pq = _dot(jnp.concatenate([d['p'].astype(BF16), qb], axis=0), qb)
                    d['p'] = d['p'] + pq[:MXU_DIM]
                    d['q'] = pq[MXU_DIM:]
                else:
                    d['p'] = d['p'] + _dot(d['p'].astype(BF16), qb)
            yield
        for d in st:
            uw = _dot(d['p'].astype(BF16), d['rhs'])
            for i, hd in enumerate(d['heads']):
                hs = slice(i * chunk, (i + 1) * chunk)
                u_s[d['rsl'], hcols(hd)] = uw[hs, :LANES]
                w_s[d['rsl'], hcols(hd)] = uw[hs, LANES:].astype(BF16)
        yield

    def phase_d(s, c):
        f0 = s * tls + c * chunk
        rsl = slice(f0, f0 + chunk)
        both = []
        for p in range(n_pairs):
            pc = slice(p * MXU_DIM, (p + 1) * MXU_DIM)
            lhs = jnp.concatenate([w_s[rsl, pc], qg_s[rsl, pc]], axis=0)
            both.append(_dot(lhs, sb_s[s, p]))
        yield
        vnb, outer = [], []
        for p in range(n_pairs):
            pc = slice(p * MXU_DIM, (p + 1) * MXU_DIM)
            v_new = (u_s[rsl, pc] - both[p][:chunk]).astype(BF16)
            for i in range(2):
                hd = 2 * p + i
                vh = v_new[:, i * LANES:(i + 1) * LANES]
                vnb.append(vh)
                outer.append(_dot_tn(kd_s[rsl, hcols(hd)], vh))
        for grp in range(n_groups):
            heads = list(range(grp * hg, (grp + 1) * hg))
            v_st = jnp.concatenate([vnb[hd] for hd in heads], axis=0)
            qs_st = jnp.concatenate(
                [both[hd // 2][chunk:, (hd % 2) * LANES:(hd % 2 + 1) * LANES] for hd in heads], axis=0)
            o_st = qs_st + _dot(intra_s[(s * n_chunks + c) * n_groups + grp], v_st)
            for i, hd in enumerate(heads):
                o_s[rsl, hcols(hd)] = o_st[i * chunk:(i + 1) * chunk]
        yield
        for hd in range(DN_HEADS):
            gl = crow_last(gc_s, s, c, slice(hd, hd + 1))
            s_new = s_ref[s, hd] * jnp.exp(gl) + outer[hd]
            s_ref[s, hd] = s_new
            d0 = (hd % 2) * DN_DIM
            sb_s[s, hd // 2, d0:d0 + DN_DIM, d0:d0 + DN_DIM] = s_new.astype(BF16)
        yield

    def phase_e(r0, r1):
        rsl = slice(r0, r1)
        for hd in range(DN_HEADS):
            o = o_s[rsl, hcols(hd)]
            if rm:
                zz = jnp.concatenate([crows(z_s, 0, c, hcols(hd)) for c in range(r0 // chunk, r1 // chunk)], axis=0)
            else:
                zz = z_s[rsl, hcols(hd)]
            on_s[rsl, hcols(hd)] = (o * _rms_scale(o) * ng_ref[...] * (zz * _sigmoid(zz))).astype(BF16)
            yield
        y1 = _dot(on_s[rsl, :], wout_ref[...])
        if rm:
            for i, c in enumerate(range(r0 // chunk, r1 // chunk)):
                for r in range(SUBLANES):
                    p0 = i * chunk + r * SUBLANES
                    for j in range(n_slabs):
                        ys_s[pl.ds(j * tls + c * chunk + r, SUBLANES, stride=SUBLANES), :] = (
                            y1[p0:p0 + SUBLANES, j * LANES:(j + 1) * LANES])
            y1 = jnp.concatenate([ys_s[j * tls + r0:j * tls + r1, :] for j in range(n_slabs)], axis=1)
            x2 = x_ref[0, r0:r1, :] + y1
            y_ref[0, r0:r1, :] = x2 * _rms_scale(x2) * fg_ref[...]
        else:
            x2 = x_ref[r0 // tls:r1 // tls].reshape(r1 - r0, D_MODEL) + y1
            yv = x2 * _rms_scale(x2) * fg_ref[...]
            y_ref[r0 // tls:r1 // tls] = yv.reshape((r1 - r0) // tls, tls, D_MODEL)
        yield

    units = [(s, c) for s in range(ns) for c in range(n_chunks)]
    upb = max(1, 4 // n_groups)
    batches = [units[i:i + upb] for i in range(0, len(units), upb)]
    brows = upb * chunk

    def c_of(batch):
        return phase_c([(s, c, grp) for (s, c) in batch for grp in range(n_groups)])

    def d_of(batch):
        if use_state:
            return [phase_d(s, c) for (s, c) in batch]
        return [_chain([phase_d(s, c) for (s, c) in batch])]

    _run_interleaved(c_of(batches[0]))
    for i in range(1, len(batches)):
        extra = [phase_e((i - 2) * brows, (i - 1) * brows)] if i >= 2 else []
        _run_interleaved(c_of(batches[i]), *d_of(batches[i - 1]), *extra)
    last = len(batches) - 1
    extra = [phase_e((last - 1) * brows, last * brows)] if last >= 1 else []
    _run_interleaved(*d_of(batches[last]), *extra)
    _run_interleaved(phase_e(last * brows, (last + 1) * brows))


def _block_tri(rows, chunk, residue_major):
    q = np.arange(rows)
    tm = SUBLANES * (q % (rows // SUBLANES)) + q // (rows // SUBLANES) if residue_major else q
    m = (tm[:, None] // chunk == tm[None, :] // chunk) & (tm[:, None] >= tm[None, :])
    return jnp.asarray(m.astype(np.float32), BF16)


def _dn_layer(x, g, w1, w2, conv_w, alog, dtb, ng, w_out, fg, *, ns, tls, chunk, state=None):
    nb, seq, _ = x.shape
    use_state = state is not None
    grid = (nb // ns, seq // tls)
    rows = ns * tls
    hist = CONV_WIDTH - 1
    conv_dim = 3 * DN_HEADS * DN_DIM
    n_stacks = rows * DN_HEADS // MXU_DIM
    const2 = lambda b, t: (0, 0)
    tri = _block_tri(rows, chunk, not use_state)
    args = [x, g, w1, w2, conv_w, alog, dtb, ng, w_out, fg, tri]
    x_spec = pl.BlockSpec((ns, tls, D_MODEL), lambda b, t: (b, t, 0))
    in_specs = [x_spec]
    in_specs += [pl.BlockSpec(a.shape, const2, pipeline_mode=pl.Buffered(1)) for a in args[1:]]
    if use_state:
        in_specs += [
            pl.BlockSpec((ns, hist, conv_dim), lambda b, t: (b, 0, 0)),
            pl.BlockSpec((ns, DN_HEADS, DN_DIM, DN_DIM), lambda b, t: (b, 0, 0, 0)),
        ]
        args += list(state)
    out_shape = (
        jax.ShapeDtypeStruct((nb, seq, D_MODEL), F32),
        jax.ShapeDtypeStruct((nb, hist, conv_dim), F32),
        jax.ShapeDtypeStruct((nb, DN_HEADS, DN_DIM, DN_DIM), F32),
    )
    out_specs = (
        x_spec,
        pl.BlockSpec((ns, hist, conv_dim), lambda b, t: (b, 0, 0)),
        pl.BlockSpec((ns, DN_HEADS, DN_DIM, DN_DIM), lambda b, t: (b, 0, 0, 0)),
    )
    if use_state:
        cbuf = pltpu.VMEM((ns, CONV_PAD + tls, conv_dim), F32)
    else:
        cbuf = pltpu.VMEM((tls, conv_dim), F32)
    scratch = [
        cbuf,
        pltpu.VMEM((rows, D_MODEL), F32),
        pltpu.VMEM((rows, D_MODEL), F32),
        pltpu.VMEM((rows, D_MODEL), F32),
        pltpu.VMEM((rows, D_MODEL), F32),
        pltpu.VMEM((rows, LANES), F32),
        pltpu.VMEM((rows, LANES), F32),
        pltpu.VMEM((rows, D_MODEL), F32),
        pltpu.VMEM((rows, D_MODEL), BF16),
        pltpu.VMEM((rows, D_MODEL), BF16),
        pltpu.VMEM((rows, D_MODEL), BF16),
        pltpu.VMEM((rows, D_MODEL), F32),
        pltpu.VMEM((rows, D_MODEL), BF16),
        pltpu.VMEM((n_stacks, MXU_DIM, MXU_DIM), BF16),
        pltpu.VMEM((ns, DN_HEADS // 2, MXU_DIM, MXU_DIM), BF16),
    ]
    if not use_state:
        scratch.append(pltpu.VMEM((hist, SUBLANES, conv_dim), F32))
        scratch.append(pltpu.VMEM((D_MODEL // LANES * tls, LANES), F32))
        scratch.append(pltpu.VMEM((D_MODEL // LANES * tls, LANES), F32))
    return pl.pallas_call(
        functools.partial(_dn_kernel, ns=ns, tls=tls, chunk=chunk, use_state=use_state),
        grid=grid, in_specs=in_specs, out_specs=out_specs, out_shape=out_shape,
        scratch_shapes=scratch,
        compiler_params=pltpu.CompilerParams(
            dimension_semantics=("arbitrary", "arbitrary"),
            vmem_limit_bytes=VMEM_LIMIT_BYTES),
        name="dn_state" if use_state else "dn_prompt",
    )(*args)


def _pad_lanes(v):
    return jnp.pad(v.astype(F32)[None, :], ((0, 0), (0, LANES - v.shape[0])))


def kernel(x_prompt, x_sample, cache_k, cache_v, state_conv, state_ssm, norm_g, final_norm_g,
           attn_w_in, attn_sinks, attn_w_out, dn_w_in, dn_conv_w, dn_a_log, dn_dt_bias,
           dn_norm_g, dn_w_out):
    attn_width = N_HEADS * HEAD_DIM
    kv_width = N_KV_HEADS * HEAD_DIM
    col_scale = jnp.where(jnp.arange(attn_w_in.shape[-1]) < attn_width, HEAD_DIM ** -0.5 * LOG2E, 1.0).astype(F32)
    a_win = (attn_w_in[0] * col_scale).astype(BF16)
    a_wout = attn_w_out[0].astype(BF16)
    g0 = norm_g[0].reshape(1, D_MODEL)
    sinks = attn_sinks[0]

    wide = 3 * DN_HEADS * DN_DIM + DN_HEADS * DN_DIM
    dw = dn_w_in[0]
    d_w1 = dw[:, :wide].astype(BF16)
    lane_pad = ((0, 0), (0, LANES - DN_HEADS))
    d_w2 = jnp.concatenate([jnp.pad(dw[:, wide:wide + DN_HEADS], lane_pad),
                            jnp.pad(dw[:, wide + DN_HEADS:], lane_pad)], axis=1).astype(BF16)
    d_wout = dn_w_out[0].astype(BF16)
    g1 = norm_g[1].reshape(1, D_MODEL)
    fg = final_norm_g.reshape(1, D_MODEL)
    alog = _pad_lanes(dn_a_log[0])
    dtb = _pad_lanes(dn_dt_bias[0])
    ng = dn_norm_g[0].reshape(1, DN_DIM)
    dn_args = (g1, d_w1, d_w2, dn_conv_w[0], alog, dtb, ng, d_wout, fg)

    x1p, kp, vp = _attn_layer(x_prompt, g0, a_win, a_wout, sinks, ns=1, tls=ATTN_TILE_ROWS, chunk=CHUNK)
    yp, cp, sp = _dn_layer(x1p, *dn_args, ns=1, tls=DN_TILE_ROWS, chunk=CHUNK)

    n_run, dec_seq = x_sample.shape[:2]
    ck = cache_k[0].reshape(n_run, WINDOW, kv_width)
    cv = cache_v[0].reshape(n_run, WINDOW, kv_width)
    x1s, ks, vs = _attn_layer(x_sample, g0, a_win, a_wout, sinks, ns=RUN_SEQS_PER_STEP, tls=dec_seq,
                              chunk=min(CHUNK, dec_seq), cache=(ck, cv))
    ys, cs, ss = _dn_layer(x1s, *dn_args, ns=RUN_SEQS_PER_STEP, tls=dec_seq, chunk=min(CHUNK, dec_seq),
                           state=(state_conv[0], state_ssm[0]))

    def kv_out(t):
        return t.reshape(1, t.shape[0], WINDOW, N_KV_HEADS, HEAD_DIM)

    return (yp, ys, kv_out(kp), kv_out(vp), cp[None], sp[None],
            kv_out(ks), kv_out(vs), cs[None], ss[None])
```

```python
import functools
import math

import numpy as np
import jax
import jax.numpy as jnp
from jax import lax
from jax.experimental import pallas as pl
from jax.experimental.pallas import tpu as pltpu

F32 = jnp.float32
BF16 = jnp.bfloat16

D_MODEL = 1024
EPS = 1e-6
CHUNK = 64
WINDOW = 128
N_HEADS = 16
N_KV_HEADS = 4
HEAD_DIM = 64
DN_HEADS = 8
DN_DIM = 128
CONV_WIDTH = 4
CONV_PAD = 8
SUBLANES = 8
LANES = 128
MXU_DIM = 256
VMEM_LIMIT_BYTES = 56 * 1024 * 1024
ATTN_TILE_ROWS = 1024
DN_TILE_ROWS = 512
RUN_SEQS_PER_STEP = 8
LOG2E = 1.4426950408889634

_NT = (((1,), (1,)), ((), ()))
_TN = (((0,), (0,)), ((), ()))


def _dot(a, b):
    return jnp.dot(a, b, preferred_element_type=F32)


def _dot_nt(a, b):
    return lax.dot_general(a, b, _NT, preferred_element_type=F32)


def _dot_tn(a, b):
    return lax.dot_general(a, b, _TN, preferred_element_type=F32)


def _sigmoid(x):
    return 1.0 / (1.0 + jnp.exp2(x * -LOG2E))


def _rms_scale(x):
    return lax.rsqrt(jnp.mean(x * x, axis=-1, keepdims=True) + EPS)


def _run_interleaved(*gens):
    live = list(gens)
    while live:
        for g in list(live):
            try:
                next(g)
            except StopIteration:
                live.remove(g)


def _chain(gens):
    for g in gens:
        yield from g


def _attn_kernel(*refs, ns, tls, chunk, use_cache):
    if use_cache:
        (x_ref, g_ref, win_ref, wout_ref, bias_ref, ck_ref, cv_ref,
         x1_ref, kout_ref, vout_ref,
         q_s, gate_s, klo_s, khi_s, v_s, og_s) = refs
    else:
        (x_ref, g_ref, win_ref, wout_ref, bias_ref,
         x1_ref, kout_ref, vout_ref,
         q_s, gate_s, klo_s, khi_s, v_s, og_s) = refs
    t = pl.program_id(1)
    rows = ns * tls
    keys = WINDOW + chunk
    kvn = N_KV_HEADS * HEAD_DIM
    kvw = 2 * kvn

    half_lo = lax.broadcasted_iota(jnp.int32, (1, LANES), 1) < HEAD_DIM

    x = x_ref[...].reshape(rows, D_MODEL)
    h = (x * _rms_scale(x) * g_ref[...]).astype(BF16)
    q_s[...] = _dot(h, win_ref[:, 0:1024]).astype(BF16)
    kd = _dot(h, win_ref[:, 1024:1024 + kvn])
    vd = _dot(h, win_ref[:, 1024 + kvn:1024 + 2 * kvn])
    gate_s[...] = _dot(h, win_ref[:, 1024 + 2 * kvn:])

    def spread(k2, v2, nrows, row0):
        for b in range(N_KV_HEADS // 2):
            kb = k2[:, b * LANES:(b + 1) * LANES]
            vb = v2[:, b * LANES:(b + 1) * LANES]
            kr = pltpu.roll(kb, HEAD_DIM, 1)
            vr = pltpu.roll(vb, HEAD_DIM, 1)
            forms = (
                (klo_s, jnp.where(half_lo, kb, 0.0), jnp.where(half_lo, kr, 0.0)),
                (khi_s, jnp.where(half_lo, 0.0, kr), jnp.where(half_lo, 0.0, kb)),
                (v_s, jnp.where(half_lo, vb, vr), jnp.where(half_lo, vr, vb)),
            )
            for ref, even, odd in forms:
                for j, val in ((2 * b, even), (2 * b + 1, odd)):
                    ref[:, row0:row0 + nrows, j * LANES:(j + 1) * LANES] = (
                        val.astype(BF16).reshape(ns, nrows, LANES))

    if use_cache:
        spread(ck_ref[...].reshape(ns * WINDOW, kvn), cv_ref[...].reshape(ns * WINDOW, kvn), WINDOW, 0)
    else:
        @pl.when(t == 0)
        def _():
            zeros = jnp.zeros((ns, WINDOW, kvw), BF16)
            klo_s[:, 0:WINDOW, :] = zeros
            khi_s[:, 0:WINDOW, :] = zeros
            v_s[:, 0:WINDOW, :] = zeros

    spread(kd, vd, tls, WINDOW)

    kd3 = kd.reshape(ns, tls, kvn)
    vd3 = vd.reshape(ns, tls, kvn)
    if use_cache:
        kout_ref[:, 0:WINDOW - tls, :] = ck_ref[:, tls:, :]
        vout_ref[:, 0:WINDOW - tls, :] = cv_ref[:, tls:, :]
        kout_ref[:, WINDOW - tls:, :] = kd3
        vout_ref[:, WINDOW - tls:, :] = vd3
    else:
        @pl.when(t == pl.num_programs(1) - 1)
        def _():
            kout_ref[...] = kd3[:, tls - WINDOW:, :]
            vout_ref[...] = vd3[:, tls - WINDOW:, :]

    pad = MXU_DIM - keys
    col = lax.broadcasted_iota(jnp.int32, (chunk, MXU_DIM), 1)
    pair_lo = lax.broadcasted_iota(jnp.int32, (chunk, LANES), 1) < HEAD_DIM
    ones_blk = jnp.ones((keys, LANES), BF16)
    k_pad = jnp.zeros((pad, LANES), BF16)
    v_pad = jnp.concatenate([jnp.zeros((pad, LANES), BF16), jnp.ones((pad, LANES), BF16)], axis=1)
    n_chunks = tls // chunk

    def attend(s, c):
        r0 = c * chunk
        f0 = s * tls + r0
        masked = (not use_cache) and r0 < WINDOW
        if masked:
            n_inv = jnp.maximum(0, WINDOW - (t * tls + r0))
            valid = col >= n_inv
        scores = []
        for j in range(N_KV_HEADS):
            qj = q_s[f0:f0 + chunk, j * 256:(j + 1) * 256]
            lhs = jnp.concatenate([qj[:, :LANES], qj[:, LANES:]], axis=0)
            klo = jnp.concatenate([klo_s[s, r0:r0 + keys, j * LANES:(j + 1) * LANES], k_pad], axis=0)
            khi = jnp.concatenate([khi_s[s, r0:r0 + keys, j * LANES:(j + 1) * LANES], k_pad], axis=0)
            s_even = _dot_nt(lhs, klo)
            s_odd = _dot_nt(lhs, khi)
            scores.append((s_even[:chunk], s_odd[:chunk], s_even[chunk:], s_odd[chunk:]))
        yield
        ovs = []
        for j in range(N_KV_HEADS):
            ps = []
            for g in range(4):
                sg = scores[j][g] + bias_ref[4 * j + g]
                if masked:
                    sg = jnp.where(valid, sg, -jnp.inf)
                m = jnp.max(sg, axis=-1, keepdims=True)
                ps.append(jnp.exp2(sg - m).astype(BF16))
            p_all = jnp.concatenate(ps, axis=0)
            vj = v_s[s, r0:r0 + keys, j * LANES:(j + 1) * LANES]
            vaug = jnp.concatenate([jnp.concatenate([vj, ones_blk], axis=1), v_pad], axis=0)
            ovs.append(_dot(p_all, vaug))
        yield
        for j in range(N_KV_HEADS):
            outs = []
            for g in range(4):
                blk = ovs[j][g * chunk:(g + 1) * chunk]
                outs.append(blk[:, :LANES] / blk[:, LANES:])
            o01 = jnp.where(pair_lo, outs[0], outs[1])
            o23 = jnp.where(pair_lo, outs[2], outs[3])
            gt = gate_s[f0:f0 + chunk, j * 256:(j + 1) * 256]
            og = jnp.concatenate([o01, o23], axis=1) * (gt * _sigmoid(gt))
            og_s[f0:f0 + chunk, j * 256:(j + 1) * 256] = og.astype(BF16)
        yield

    units = [(s, c) for s in range(ns) for c in range(n_chunks)]
    for i in range(0, len(units), 2):
        _run_interleaved(*[attend(s, c) for s, c in units[i:i + 2]])

    if not use_cache:
        klo_s[:, 0:WINDOW, :] = klo_s[:, tls:tls + WINDOW, :]
        khi_s[:, 0:WINDOW, :] = khi_s[:, tls:tls + WINDOW, :]
        v_s[:, 0:WINDOW, :] = v_s[:, tls:tls + WINDOW, :]

    y = _dot(og_s[...], wout_ref[...])
    x1_ref[...] = (x + y).reshape(ns, tls, D_MODEL)


def _attn_bias(chunk, sinks):
    keys = WINDOW + chunk
    slopes = (2.0 ** (-8.0 * np.arange(1, N_HEADS + 1, dtype=np.float32) / N_HEADS)).astype(np.float32)
    i = np.arange(chunk, dtype=np.float32)[:, None]
    j = np.arange(keys, dtype=np.float32)[None, :]
    dist = np.abs(i + WINDOW - j).astype(np.float32)
    alibi = jnp.asarray(-(slopes[:, None, None] * dist[None]), F32) * LOG2E
    sink_col = jnp.broadcast_to((sinks.astype(F32) * LOG2E)[:, None, None], (N_HEADS, chunk, 1))
    rest = jnp.full((N_HEADS, chunk, MXU_DIM - keys - 1), -jnp.inf, F32)
    return jnp.concatenate([alibi, sink_col, rest], axis=2)


def _attn_layer(x, g, w_in_prep, w_out, sinks, *, ns, tls, chunk, cache=None):
    nb, seq, _ = x.shape
    use_cache = cache is not None
    grid = (nb // ns, seq // tls)
    kvn = N_KV_HEADS * HEAD_DIM
    kvw = 2 * kvn
    rows = ns * tls
    const2 = lambda b, t: (0, 0)
    in_specs = [
        pl.BlockSpec((ns, tls, D_MODEL), lambda b, t: (b, t, 0)),
        pl.BlockSpec((1, D_MODEL), const2),
        pl.BlockSpec(w_in_prep.shape, const2, pipeline_mode=pl.Buffered(1)),
        pl.BlockSpec(w_out.shape, const2, pipeline_mode=pl.Buffered(1)),
        pl.BlockSpec((N_HEADS, chunk, MXU_DIM), lambda b, t: (0, 0, 0), pipeline_mode=pl.Buffered(1)),
    ]
    args = [x, g, w_in_prep, w_out, _attn_bias(chunk, sinks)]
    if use_cache:
        in_specs += [pl.BlockSpec((ns, WINDOW, kvn), lambda b, t: (b, 0, 0))] * 2
        args += list(cache)
    out_shape = (
        jax.ShapeDtypeStruct((nb, seq, D_MODEL), F32),
        jax.ShapeDtypeStruct((nb, WINDOW, kvn), F32),
        jax.ShapeDtypeStruct((nb, WINDOW, kvn), F32),
    )
    out_specs = (
        pl.BlockSpec((ns, tls, D_MODEL), lambda b, t: (b, t, 0)),
        pl.BlockSpec((ns, WINDOW, kvn), lambda b, t: (b, 0, 0)),
        pl.BlockSpec((ns, WINDOW, kvn), lambda b, t: (b, 0, 0)),
    )
    scratch = [
        pltpu.VMEM((rows, D_MODEL), BF16),
        pltpu.VMEM((rows, D_MODEL), F32),
        pltpu.VMEM((ns, WINDOW + tls, kvw), BF16),
        pltpu.VMEM((ns, WINDOW + tls, kvw), BF16),
        pltpu.VMEM((ns, WINDOW + tls, kvw), BF16),
        pltpu.VMEM((rows, D_MODEL), BF16),
    ]
    return pl.pallas_call(
        functools.partial(_attn_kernel, ns=ns, tls=tls, chunk=chunk, use_cache=use_cache),
        grid=grid, in_specs=in_specs, out_specs=out_specs, out_shape=out_shape,
        scratch_shapes=scratch,
        compiler_params=pltpu.CompilerParams(
            dimension_semantics=("arbitrary", "arbitrary"),
            vmem_limit_bytes=VMEM_LIMIT_BYTES),
        name="attn_cache" if use_cache else "attn_prompt",
    )(*args)


def _split3(x):
    hi = x.astype(BF16)
    r1 = x - hi.astype(F32)
    mid = r1.astype(BF16)
    lo = (r1 - mid.astype(F32)).astype(BF16)
    return hi, mid, lo


def _dn_kernel(*refs, ns, tls, chunk, use_state):
    rm = not use_state
    if use_state:
        (x_ref, g_ref, w1_ref, w2_ref, cw_ref, alog_ref, dtb_ref, ng_ref, wout_ref, fg_ref,
         tri_ref, cst_ref, sst_ref,
         y_ref, cout_ref, s_ref,
         cbuf, qn_s, kn_s, v_s, z_s, bt_s, gc_s, o_s, on_s,
         qg_s, kd_s, u_s, w_s, intra_s, sb_s) = refs
    else:
        (x_ref, g_ref, w1_ref, w2_ref, cw_ref, alog_ref, dtb_ref, ng_ref, wout_ref, fg_ref,
         tri_ref,
         y_ref, cout_ref, s_ref,
         cbuf, qn_s, kn_s, v_s, z_s, bt_s, gc_s, o_s, on_s,
         qg_s, kd_s, u_s, w_s, intra_s, sb_s, hist_s, xs_s, ys_s) = refs
        assert ns == 1 and chunk == SUBLANES * SUBLANES
    t = pl.program_id(1)
    rows = ns * tls
    hist = CONV_WIDTH - 1
    qk_w = DN_HEADS * DN_DIM
    conv_w = 3 * qk_w
    n_pairs = DN_HEADS // 2
    rres = tls // SUBLANES

    if use_state:
        s_ref[...] = sst_ref[...]
        cbuf[:, CONV_PAD - hist:CONV_PAD, :] = cst_ref[...]
        sb_s[...] = jnp.zeros(sb_s.shape, BF16)
        for s in range(ns):
            for hd in range(DN_HEADS):
                d0 = (hd % 2) * DN_DIM
                sb_s[s, hd // 2, d0:d0 + DN_DIM, d0:d0 + DN_DIM] = sst_ref[s, hd].astype(BF16)
    else:
        @pl.when(t == 0)
        def _():
            s_ref[...] = jnp.zeros(s_ref.shape, F32)
            sb_s[...] = jnp.zeros(sb_s.shape, BF16)
            hist_s[...] = jnp.zeros(hist_s.shape, F32)

    n_slabs = D_MODEL // LANES
    if rm:
        for j in range(n_slabs):
            xs_s[j * tls:(j + 1) * tls, :] = x_ref[0, :, j * LANES:(j + 1) * LANES]
        x = jnp.concatenate(
            [jnp.concatenate([xs_s[pl.ds(j * tls + r, rres, stride=SUBLANES), :] for j in range(n_slabs)], axis=1)
             for r in range(SUBLANES)], axis=0)
    else:
        x = x_ref[...].reshape(rows, D_MODEL)
    h = (x * _rms_scale(x) * g_ref[...]).astype(BF16)

    def phase_a():
        for c0 in range(0, conv_w, MXU_DIM):
            raw = _dot(h, w1_ref[:, c0:c0 + MXU_DIM])
            if rm:
                cbuf[:, c0:c0 + MXU_DIM] = raw
            else:
                cbuf[:, CONV_PAD:, c0:c0 + MXU_DIM] = raw.reshape(ns, tls, MXU_DIM)
            yield
        ba = _dot(h, w2_ref[...])
        bt_s[...] = _sigmoid(ba[:, :LANES])
        a = ba[:, LANES:] + dtb_ref[...]
        softplus = jnp.maximum(a, 0.0) + jnp.log1p(jnp.exp(-jnp.abs(a)))
        g = -jnp.exp(alog_ref[...]) * softplus
        g_hi, g_mid, g_lo = _split3(g)
        gcs = _dot(tri_ref[...], jnp.concatenate([g_hi, g_mid, g_lo], axis=1))
        gc_s[...] = gcs[:, :LANES] + gcs[:, LANES:2 * LANES] + gcs[:, 2 * LANES:]
        yield
        for c0 in range(0, qk_w, MXU_DIM):
            z_s[:, c0:c0 + MXU_DIM] = _dot(h, w1_ref[:, conv_w + c0:conv_w + c0 + MXU_DIM])
            yield

    def conv_block(s, cols):
        if not rm:
            acc = None
            for j in range(CONV_WIDTH):
                lo = CONV_PAD - hist + j
                term = cbuf[s, lo:lo + tls, cols] * cw_ref[j:j + 1, cols]
                acc = term if acc is None else acc + term
            return acc
        xb = cbuf[:, cols]
        wrapped = []
        for k in range(hist):
            r = SUBLANES - hist + k
            full = jnp.concatenate([hist_s[k, :, cols], xb[r * rres:(r + 1) * rres]], axis=0)
            wrapped.append(full[SUBLANES - 1:SUBLANES - 1 + rres])
        ext = jnp.concatenate(wrapped + [xb], axis=0)
        acc = None
        for j in range(CONV_WIDTH):
            term = ext[j * rres:j * rres + tls] * cw_ref[j:j + 1, cols]
            acc = term if acc is None else acc + term
        return acc

    def phase_b():
        for blk in range(conv_w // LANES):
            cols = slice(blk * LANES, (blk + 1) * LANES)
            seg, hb = divmod(blk, DN_HEADS)
            dst = slice(hb * LANES, (hb + 1) * LANES)
            for s in range(ns):
                acc = conv_block(s, cols)
                yv = acc * _sigmoid(acc)
                rs = slice(s * tls, (s + 1) * tls)
                if seg == 2:
                    v_s[rs, dst] = yv
                else:
                    inv = lax.rsqrt(jnp.sum(yv * yv, axis=-1, keepdims=True) + EPS)
                    if seg == 0:
                        qn_s[rs, dst] = yv * (inv * (DN_DIM ** -0.5))
                    else:
                        kn_s[rs, dst] = yv * inv
            if blk % 2 == 1:
                yield

    ga, gb = phase_a(), phase_b()
    for _ in range(qk_w // MXU_DIM):
        next(ga)
    _run_interleaved(gb, ga)

    if rm:
        for k in range(hist):
            r = SUBLANES - hist + k
            last = cbuf[(r + 1) * rres - SUBLANES:(r + 1) * rres, :]
            hist_s[k] = last
            cout_ref[0, k:k + 1, :] = last[SUBLANES - 1:SUBLANES]
    else:
        cout_ref[...] = cbuf[:, CONV_PAD + tls - hist:CONV_PAD + tls, :]

    hg = MXU_DIM // chunk
    n_groups = DN_HEADS // hg
    n_levels = int(round(math.log2(chunk)))
    ri = lax.broadcasted_iota(jnp.int32, (MXU_DIM, MXU_DIM), 0)
    ci = lax.broadcasted_iota(jnp.int32, (MXU_DIM, MXU_DIM), 1)

    def time_of(pos):
        p = pos % chunk
        return SUBLANES * (p % SUBLANES) + p // SUBLANES if rm else p

    tri_mask = (ri // chunk == ci // chunk) & (time_of(ri) >= time_of(ci))
    diag = ri == ci
    eye = jnp.where(diag, 1.0, 0.0).astype(F32)
    lane = lax.broadcasted_iota(jnp.int32, (MXU_DIM, LANES), 1)
    ones_rows = jnp.ones((2 * SUBLANES, LANES), BF16)
    n_chunks = tls // chunk

    def hcols(hd):
        return slice(hd * LANES, (hd + 1) * LANES)

    def crows(ref, s, c, cols):
        if rm:
            return jnp.concatenate(
                [ref[r * rres + SUBLANES * c:r * rres + SUBLANES * (c + 1), cols] for r in range(SUBLANES)], axis=0)
        f0 = s * tls + c * chunk
        return ref[f0:f0 + chunk, cols]

    def crow_last(ref, s, c, cols):
        r = (SUBLANES - 1) * rres + SUBLANES * c + SUBLANES - 1 if rm else s * tls + (c + 1) * chunk - 1
        return ref[r:r + 1, cols]

    def phase_c(groups):
        st = []
        for (s, c, grp) in groups:
            f0 = s * tls + c * chunk
            rsl = slice(f0, f0 + chunk)
            heads = list(range(grp * hg, (grp + 1) * hg))
            k_h, q_h, rhs_rows, gc_cols, beta_cols = [], [], [], [], []
            for hd in heads:
                beta_h = crows(bt_s, s, c, slice(hd, hd + 1))
                gc_h = crows(gc_s, s, c, slice(hd, hd + 1))
                gl_h = crow_last(gc_s, s, c, slice(hd, hd + 1))
                eg = jnp.exp(gc_h)
                kf = crows(kn_s, s, c, hcols(hd))
                qf = crows(qn_s, s, c, hcols(hd))
                vb = (crows(v_s, s, c, hcols(hd)) * beta_h).astype(BF16)
                kbg = (kf * (beta_h * eg)).astype(BF16)
                qg_s[rsl, hcols(hd)] = (qf * eg).astype(BF16)
                kd_s[rsl, hcols(hd)] = (kf * jnp.exp(gl_h - gc_h)).astype(BF16)
                k_h.append(kf.astype(BF16))
                q_h.append(qf.astype(BF16))
                rhs_rows.append(jnp.concatenate([vb, kbg], axis=1))
                gc_cols.append(gc_h)
                beta_cols.append(beta_h)
            gc_c = jnp.concatenate(gc_cols, axis=0)
            hi, mid, lo = (p.astype(F32) for p in _split3(gc_c))
            rmat = jnp.where(lane == 0, hi, jnp.where(lane == 1, mid, jnp.where(lane == 2, lo, 0.0)))
            gc_row = _dot_nt(ones_rows, rmat.astype(BF16))[0:1]
            k_st = jnp.concatenate(k_h, axis=0)
            q_st = jnp.concatenate(q_h, axis=0)
            st.append(dict(
                rsl=rsl, heads=heads, gi=(s * n_chunks + c) * n_groups + grp,
                rhs=jnp.concatenate(rhs_rows, axis=0),
                beta=jnp.concatenate(beta_cols, axis=0),
                gd=gc_c - gc_row,
                kq=_dot_nt(jnp.concatenate([k_st, q_st], axis=0), k_st)))
        yield
        for d in st:
            decay = jnp.exp(jnp.where(tri_mask, d.pop('gd'), -jnp.inf))
            kq = d.pop('kq')
            m = jnp.where(diag, 0.0, kq[:MXU_DIM] * decay * d.pop('beta'))
            intra_s[d['gi']] = (kq[MXU_DIM:] * decay).astype(BF16)
            mb = m.astype(BF16)
            d['p'] = eye - m
            d['q'] = _dot(mb, mb)
        yield
        for lvl in range(1, n_levels):
            for d in st:
                qb = d['q'].astype(BF16)
                if lvl < n_levels - 1:
                    pq = _dot(jnp.concatenate([d['p'].astype(BF16), qb], axis=0), qb)
                    d['p'] = d['p'] + pq[:MXU_DIM]
                    d['q'] = pq[MXU_DIM:]
                else:
                    d['p'] = d['p'] + _dot(d['p'].astype(BF16), qb)
            yield
        for d in st:
            uw = _dot(d['p'].astype(BF16), d['rhs'])
            for i, hd in enumerate(d['heads']):
                hs = slice(i * chunk, (i + 1) * chunk)
                u_s[d['rsl'], hcols(hd)] = uw[hs, :LANES]
                w_s[d['rsl'], hcols(hd)] = uw[hs, LANES:].astype(BF16)
        yield

    def phase_d(s, c):
        f0 = s * tls + c * chunk
        rsl = slice(f0, f0 + chunk)
        both = []
        for p in range(n_pairs):
            pc = slice(p * MXU_DIM, (p + 1) * MXU_DIM)
            lhs = jnp.concatenate([w_s[rsl, pc], qg_s[rsl, pc]], axis=0)
            both.append(_dot(lhs, sb_s[s, p]))
        yield
        vnb, outer = [], []
        for p in range(n_pairs):
            pc = slice(p * MXU_DIM, (p + 1) * MXU_DIM)
            v_new = (u_s[rsl, pc] - both[p][:chunk]).astype(BF16)
            for i in range(2):
                hd = 2 * p + i
                vh = v_new[:, i * LANES:(i + 1) * LANES]
                vnb.append(vh)
                outer.append(_dot_tn(kd_s[rsl, hcols(hd)], vh))
        for grp in range(n_groups):
            heads = list(range(grp * hg, (grp + 1) * hg))
            v_st = jnp.concatenate([vnb[hd] for hd in heads], axis=0)
            qs_st = jnp.concatenate(
                [both[hd // 2][chunk:, (hd % 2) * LANES:(hd % 2 + 1) * LANES] for hd in heads], axis=0)
            o_st = qs_st + _dot(intra_s[(s * n_chunks + c) * n_groups + grp], v_st)
            for i, hd in enumerate(heads):
                o_s[rsl, hcols(hd)] = o_st[i * chunk:(i + 1) * chunk]
        yield
        for hd in range(DN_HEADS):
            gl = crow_last(gc_s, s, c, slice(hd, hd + 1))
            s_new = s_ref[s, hd] * jnp.exp(gl) + outer[hd]
            s_ref[s, hd] = s_new
            d0 = (hd % 2) * DN_DIM
            sb_s[s, hd // 2, d0:d0 + DN_DIM, d0:d0 + DN_DIM] = s_new.astype(BF16)
        yield

    def phase_e(r0, r1):
        rsl = slice(r0, r1)
        for hd in range(DN_HEADS):
            o = o_s[rsl, hcols(hd)]
            if rm:
                zz = jnp.concatenate([crows(z_s, 0, c, hcols(hd)) for c in range(r0 // chunk, r1 // chunk)], axis=0)
            else:
                zz = z_s[rsl, hcols(hd)]
            on_s[rsl, hcols(hd)] = (o * _rms_scale(o) * ng_ref[...] * (zz * _sigmoid(zz))).astype(BF16)
            yield
        y1 = _dot(on_s[rsl, :], wout_ref[...])
        if rm:
            for i, c in enumerate(range(r0 // chunk, r1 // chunk)):
                for r in range(SUBLANES):
                    p0 = i * chunk + r * SUBLANES
                    for j in range(n_slabs):
                        ys_s[pl.ds(j * tls + c * chunk + r, SUBLANES, stride=SUBLANES), :] = (
                            y1[p0:p0 + SUBLANES, j * LANES:(j + 1) * LANES])
            y1 = jnp.concatenate([ys_s[j * tls + r0:j * tls + r1, :] for j in range(n_slabs)], axis=1)
            x2 = x_ref[0, r0:r1, :] + y1
            y_ref[0, r0:r1, :] = x2 * _rms_scale(x2) * fg_ref[...]
        else:
            x2 = x_ref[r0 // tls:r1 // tls].reshape(r1 - r0, D_MODEL) + y1
            yv = x2 * _rms_scale(x2) * fg_ref[...]
            y_ref[r0 // tls:r1 // tls] = yv.reshape((r1 - r0) // tls, tls, D_MODEL)
        yield

    units = [(s, c) for s in range(ns) for c in range(n_chunks)]
    upb = max(1, 4 // n_groups)
    batches = [units[i:i + upb] for i in range(0, len(units), upb)]
    brows = upb * chunk

    def c_of(batch):
        return phase_c([(s, c, grp) for (s, c) in batch for grp in range(n_groups)])

    def d_of(batch):
        if use_state:
            return [phase_d(s, c) for (s, c) in batch]
        return [_chain([phase_d(s, c) for (s, c) in batch])]

    _run_interleaved(c_of(batches[0]), ga)
    for i in range(1, len(batches)):
        extra = [phase_e((i - 2) * brows, (i - 1) * brows)] if i >= 2 else []
        _run_interleaved(c_of(batches[i]), *d_of(batches[i - 1]), *extra)
    last = len(batches) - 1
    extra = [phase_e((last - 1) * brows, last * brows)] if last >= 1 else []
    _run_interleaved(*d_of(batches[last]), *extra)
    _run_interleaved(phase_e(last * brows, (last + 1) * brows))


def _block_tri(rows, chunk, residue_major):
    q = np.arange(rows)
    tm = SUBLANES * (q % (rows // SUBLANES)) + q // (rows // SUBLANES) if residue_major else q
    m = (tm[:, None] // chunk == tm[None, :] // chunk) & (tm[:, None] >= tm[None, :])
    return jnp.asarray(m.astype(np.float32), BF16)


def _dn_layer(x, g, w1, w2, conv_w, alog, dtb, ng, w_out, fg, *, ns, tls, chunk, state=None):
    nb, seq, _ = x.shape
    use_state = state is not None
    grid = (nb // ns, seq // tls)
    rows = ns * tls
    hist = CONV_WIDTH - 1
    conv_dim = 3 * DN_HEADS * DN_DIM
    n_stacks = rows * DN_HEADS // MXU_DIM
    const2 = lambda b, t: (0, 0)
    tri = _block_tri(rows, chunk, not use_state)
    args = [x, g, w1, w2, conv_w, alog, dtb, ng, w_out, fg, tri]
    x_spec = pl.BlockSpec((ns, tls, D_MODEL), lambda b, t: (b, t, 0))
    in_specs = [x_spec]
    in_specs += [pl.BlockSpec(a.shape, const2, pipeline_mode=pl.Buffered(1)) for a in args[1:]]
    if use_state:
        in_specs += [
            pl.BlockSpec((ns, hist, conv_dim), lambda b, t: (b, 0, 0)),
            pl.BlockSpec((ns, DN_HEADS, DN_DIM, DN_DIM), lambda b, t: (b, 0, 0, 0)),
        ]
        args += list(state)
    out_shape = (
        jax.ShapeDtypeStruct((nb, seq, D_MODEL), F32),
        jax.ShapeDtypeStruct((nb, hist, conv_dim), F32),
        jax.ShapeDtypeStruct((nb, DN_HEADS, DN_DIM, DN_DIM), F32),
    )
    out_specs = (
        x_spec,
        pl.BlockSpec((ns, hist, conv_dim), lambda b, t: (b, 0, 0)),
        pl.BlockSpec((ns, DN_HEADS, DN_DIM, DN_DIM), lambda b, t: (b, 0, 0, 0)),
    )
    if use_state:
        cbuf = pltpu.VMEM((ns, CONV_PAD + tls, conv_dim), F32)
    else:
        cbuf = pltpu.VMEM((tls, conv_dim), F32)
    scratch = [
        cbuf,
        pltpu.VMEM((rows, D_MODEL), F32),
        pltpu.VMEM((rows, D_MODEL), F32),
        pltpu.VMEM((rows, D_MODEL), F32),
        pltpu.VMEM((rows, D_MODEL), F32),
        pltpu.VMEM((rows, LANES), F32),
        pltpu.VMEM((rows, LANES), F32),
        pltpu.VMEM((rows, D_MODEL), F32),
        pltpu.VMEM((rows, D_MODEL), BF16),
        pltpu.VMEM((rows, D_MODEL), BF16),
        pltpu.VMEM((rows, D_MODEL), BF16),
        pltpu.VMEM((rows, D_MODEL), F32),
        pltpu.VMEM((rows, D_MODEL), BF16),
        pltpu.VMEM((n_stacks, MXU_DIM, MXU_DIM), BF16),
        pltpu.VMEM((ns, DN_HEADS // 2, MXU_DIM, MXU_DIM), BF16),
    ]
    if not use_state:
        scratch.append(pltpu.VMEM((hist, SUBLANES, conv_dim), F32))
        scratch.append(pltpu.VMEM((D_MODEL // LANES * tls, LANES), F32))
        scratch.append(pltpu.VMEM((D_MODEL // LANES * tls, LANES), F32))
    return pl.pallas_call(
        functools.partial(_dn_kernel, ns=ns, tls=tls, chunk=chunk, use_state=use_state),
        grid=grid, in_specs=in_specs, out_specs=out_specs, out_shape=out_shape,
        scratch_shapes=scratch,
        compiler_params=pltpu.CompilerParams(
            dimension_semantics=("arbitrary", "arbitrary"),
            vmem_limit_bytes=VMEM_LIMIT_BYTES),
        name="dn_state" if use_state else "dn_prompt",
    )(*args)


def _pad_lanes(v):
    return jnp.pad(v.astype(F32)[None, :], ((0, 0), (0, LANES - v.shape[0])))


def kernel(x_prompt, x_sample, cache_k, cache_v, state_conv, state_ssm, norm_g, final_norm_g,
           attn_w_in, attn_sinks, attn_w_out, dn_w_in, dn_conv_w, dn_a_log, dn_dt_bias,
           dn_norm_g, dn_w_out):
    attn_width = N_HEADS * HEAD_DIM
    kv_width = N_KV_HEADS * HEAD_DIM
    col_scale = jnp.where(jnp.arange(attn_w_in.shape[-1]) < attn_width, HEAD_DIM ** -0.5 * LOG2E, 1.0).astype(F32)
    a_win = (attn_w_in[0] * col_scale).astype(BF16)
    a_wout = attn_w_out[0].astype(BF16)
    g0 = norm_g[0].reshape(1, D_MODEL)
    sinks = attn_sinks[0]

    wide = 3 * DN_HEADS * DN_DIM + DN_HEADS * DN_DIM
    dw = dn_w_in[0]
    d_w1 = dw.astype(BF16)
    lane_pad = ((0, 0), (0, LANES - DN_HEADS))
    d_w2 = jnp.concatenate([jnp.pad(dw[:, wide:wide + DN_HEADS], lane_pad),
                            jnp.pad(dw[:, wide + DN_HEADS:], lane_pad)], axis=1).astype(BF16)
    d_wout = dn_w_out[0].astype(BF16)
    g1 = norm_g[1].reshape(1, D_MODEL)
    fg = final_norm_g.reshape(1, D_MODEL)
    alog = _pad_lanes(dn_a_log[0])
    dtb = _pad_lanes(dn_dt_bias[0])
    ng = dn_norm_g[0].reshape(1, DN_DIM)
    dn_args = (g1, d_w1, d_w2, dn_conv_w[0], alog, dtb, ng, d_wout, fg)

    x1p, kp, vp = _attn_layer(x_prompt, g0, a_win, a_wout, sinks, ns=1, tls=ATTN_TILE_ROWS, chunk=CHUNK)
    yp, cp, sp = _dn_layer(x1p, *dn_args, ns=1, tls=DN_TILE_ROWS, chunk=CHUNK)

    n_run, dec_seq = x_sample.shape[:2]
    ck = cache_k[0].reshape(n_run, WINDOW, kv_width)
    cv = cache_v[0].reshape(n_run, WINDOW, kv_width)
    x1s, ks, vs = _attn_layer(x_sample, g0, a_win, a_wout, sinks, ns=RUN_SEQS_PER_STEP, tls=dec_seq,
                              chunk=min(CHUNK, dec_seq), cache=(ck, cv))
    ys, cs, ss = _dn_layer(x1s, *dn_args, ns=RUN_SEQS_PER_STEP, tls=dec_seq, chunk=min(CHUNK, dec_seq),
                           state=(state_conv[0], state_ssm[0]))

    def kv_out(t):
        return t.reshape(1, t.shape[0], WINDOW, N_KV_HEADS, HEAD_DIM)

    return (yp, ys, kv_out(kp), kv_out(vp), cp[None], sp[None],
            kv_out(ks), kv_out(vs), cs[None], ss[None])
```

```python
import functools
import math

import numpy as np
import jax
import jax.numpy as jnp
from jax import lax
from jax.experimental import pallas as pl
from jax.experimental.pallas import tpu as pltpu

F32 = jnp.float32
BF16 = jnp.bfloat16

D_MODEL = 1024
EPS = 1e-6
CHUNK = 64
WINDOW = 128
N_HEADS = 16
N_KV_HEADS = 4
HEAD_DIM = 64
DN_HEADS = 8
DN_DIM = 128
CONV_WIDTH = 4
CONV_PAD = 8
SUBLANES = 8
LANES = 128
MXU_DIM = 256
VMEM_LIMIT_BYTES = 56 * 1024 * 1024
ATTN_TILE_ROWS = 1024
DN_TILE_ROWS = 512
RUN_SEQS_PER_STEP = 8
ATTN_LOCKSTEP = 4
LOG2E = 1.4426950408889634

_NT = (((1,), (1,)), ((), ()))
_TN = (((0,), (0,)), ((), ()))


def _dot(a, b):
    return jnp.dot(a, b, preferred_element_type=F32)


def _dot_nt(a, b):
    return lax.dot_general(a, b, _NT, preferred_element_type=F32)


def _dot_tn(a, b):
    return lax.dot_general(a, b, _TN, preferred_element_type=F32)


def _sigmoid(x):
    return 1.0 / (1.0 + jnp.exp2(x * -LOG2E))


def _rms_scale(x):
    return lax.rsqrt(jnp.mean(x * x, axis=-1, keepdims=True) + EPS)


def _run_interleaved(*gens):
    live = list(gens)
    while live:
        for g in list(live):
            try:
                next(g)
            except StopIteration:
                live.remove(g)


def _chain(gens):
    for g in gens:
        yield from g


def _attn_kernel(*refs, ns, tls, chunk, use_cache):
    if use_cache:
        (x_ref, g_ref, win_ref, wout_ref, bias_ref, ck_ref, cv_ref,
         x1_ref, kout_ref, vout_ref,
         q_s, gate_s, klo_s, khi_s, v_s, og_s) = refs
    else:
        (x_ref, g_ref, win_ref, wout_ref, bias_ref,
         x1_ref, kout_ref, vout_ref,
         q_s, gate_s, klo_s, khi_s, v_s, og_s) = refs
    t = pl.program_id(1)
    rows = ns * tls
    keys = WINDOW + chunk
    kvn = N_KV_HEADS * HEAD_DIM
    kvw = 2 * kvn

    half_lo = lax.broadcasted_iota(jnp.int32, (1, LANES), 1) < HEAD_DIM

    x = x_ref[...].reshape(rows, D_MODEL)
    h = (x * _rms_scale(x) * g_ref[...]).astype(BF16)
    q_s[...] = _dot(h, win_ref[:, 0:1024]).astype(BF16)
    kd = _dot(h, win_ref[:, 1024:1024 + kvn])
    vd = _dot(h, win_ref[:, 1024 + kvn:1024 + 2 * kvn])

    def gate_proj():
        c0 = 1024 + 2 * kvn
        for s0 in range(0, N_HEADS * HEAD_DIM, MXU_DIM):
            gate_s[:, s0:s0 + MXU_DIM] = _dot(h, win_ref[:, c0 + s0:c0 + s0 + MXU_DIM])
            yield

    def spread(k2, v2, nrows, row0):
        for b in range(N_KV_HEADS // 2):
            kb = k2[:, b * LANES:(b + 1) * LANES]
            vb = v2[:, b * LANES:(b + 1) * LANES]
            kr = pltpu.roll(kb, HEAD_DIM, 1)
            vr = pltpu.roll(vb, HEAD_DIM, 1)
            forms = (
                (klo_s, jnp.where(half_lo, kb, 0.0), jnp.where(half_lo, kr, 0.0)),
                (khi_s, jnp.where(half_lo, 0.0, kr), jnp.where(half_lo, 0.0, kb)),
                (v_s, jnp.where(half_lo, vb, vr), jnp.where(half_lo, vr, vb)),
            )
            for ref, even, odd in forms:
                for j, val in ((2 * b, even), (2 * b + 1, odd)):
                    ref[:, row0:row0 + nrows, j * LANES:(j + 1) * LANES] = (
                        val.astype(BF16).reshape(ns, nrows, LANES))
                yield

    if use_cache:
        _run_interleaved(spread(ck_ref[...].reshape(ns * WINDOW, kvn), cv_ref[...].reshape(ns * WINDOW, kvn),
                                WINDOW, 0))
    else:
        @pl.when(t == 0)
        def _():
            zeros = jnp.zeros((ns, WINDOW, kvw), BF16)
            klo_s[:, 0:WINDOW, :] = zeros
            khi_s[:, 0:WINDOW, :] = zeros
            v_s[:, 0:WINDOW, :] = zeros

    _run_interleaved(spread(kd, vd, tls, WINDOW), gate_proj())

    kd3 = kd.reshape(ns, tls, kvn)
    vd3 = vd.reshape(ns, tls, kvn)
    if use_cache:
        kout_ref[:, 0:WINDOW - tls, :] = ck_ref[:, tls:, :]
        vout_ref[:, 0:WINDOW - tls, :] = cv_ref[:, tls:, :]
        kout_ref[:, WINDOW - tls:, :] = kd3
        vout_ref[:, WINDOW - tls:, :] = vd3
    else:
        @pl.when(t == pl.num_programs(1) - 1)
        def _():
            kout_ref[...] = kd3[:, tls - WINDOW:, :]
            vout_ref[...] = vd3[:, tls - WINDOW:, :]

    pad = MXU_DIM - keys
    col = lax.broadcasted_iota(jnp.int32, (chunk, MXU_DIM), 1)
    pair_lo = lax.broadcasted_iota(jnp.int32, (chunk, LANES), 1) < HEAD_DIM
    ones_blk = jnp.ones((keys, LANES), BF16)
    k_pad = jnp.zeros((pad, LANES), BF16)
    v_pad = jnp.concatenate([jnp.zeros((pad, LANES), BF16), jnp.ones((pad, LANES), BF16)], axis=1)
    n_chunks = tls // chunk

    def attend(s, c):
        r0 = c * chunk
        f0 = s * tls + r0
        masked = (not use_cache) and r0 < WINDOW
        if masked:
            n_inv = jnp.maximum(0, WINDOW - (t * tls + r0))
            valid = col >= n_inv
        scores = []
        for j in range(N_KV_HEADS):
            qj = q_s[f0:f0 + chunk, j * 256:(j + 1) * 256]
            lhs = jnp.concatenate([qj[:, :LANES], qj[:, LANES:]], axis=0)
            klo = jnp.concatenate([klo_s[s, r0:r0 + keys, j * LANES:(j + 1) * LANES], k_pad], axis=0)
            khi = jnp.concatenate([khi_s[s, r0:r0 + keys, j * LANES:(j + 1) * LANES], k_pad], axis=0)
            s_even = _dot_nt(lhs, klo)
            s_odd = _dot_nt(lhs, khi)
            scores.append((s_even[:chunk], s_odd[:chunk], s_even[chunk:], s_odd[chunk:]))
        yield
        ovs = []
        for j in range(N_KV_HEADS):
            ps = []
            for g in range(4):
                sg = scores[j][g] + bias_ref[4 * j + g]
                if masked:
                    sg = jnp.where(valid, sg, -jnp.inf)
                m = jnp.max(sg, axis=-1, keepdims=True)
                ps.append(jnp.exp2(sg - m).astype(BF16))
            p_all = jnp.concatenate(ps, axis=0)
            vj = v_s[s, r0:r0 + keys, j * LANES:(j + 1) * LANES]
            vaug = jnp.concatenate([jnp.concatenate([vj, ones_blk], axis=1), v_pad], axis=0)
            ovs.append(_dot(p_all, vaug))
        yield
        for j in range(N_KV_HEADS):
            outs = []
            for g in range(4):
                blk = ovs[j][g * chunk:(g + 1) * chunk]
                outs.append(blk[:, :LANES] / blk[:, LANES:])
            o01 = jnp.where(pair_lo, outs[0], outs[1])
            o23 = jnp.where(pair_lo, outs[2], outs[3])
            gt = gate_s[f0:f0 + chunk, j * 256:(j + 1) * 256]
            og = jnp.concatenate([o01, o23], axis=1) * (gt * _sigmoid(gt))
            og_s[f0:f0 + chunk, j * 256:(j + 1) * 256] = og.astype(BF16)
        yield

    units = [(s, c) for s in range(ns) for c in range(n_chunks)]
    for i in range(0, len(units), ATTN_LOCKSTEP):
        _run_interleaved(*[attend(s, c) for s, c in units[i:i + ATTN_LOCKSTEP]])

    if not use_cache:
        klo_s[:, 0:WINDOW, :] = klo_s[:, tls:tls + WINDOW, :]
        khi_s[:, 0:WINDOW, :] = khi_s[:, tls:tls + WINDOW, :]
        v_s[:, 0:WINDOW, :] = v_s[:, tls:tls + WINDOW, :]

    y = _dot(og_s[...], wout_ref[...])
    x1_ref[...] = (x + y).reshape(ns, tls, D_MODEL)


def _attn_bias(chunk, sinks):
    keys = WINDOW + chunk
    slopes = (2.0 ** (-8.0 * np.arange(1, N_HEADS + 1, dtype=np.float32) / N_HEADS)).astype(np.float32)
    i = np.arange(chunk, dtype=np.float32)[:, None]
    j = np.arange(keys, dtype=np.float32)[None, :]
    dist = np.abs(i + WINDOW - j).astype(np.float32)
    alibi = jnp.asarray(-(slopes[:, None, None] * dist[None]), F32) * LOG2E
    sink_col = jnp.broadcast_to((sinks.astype(F32) * LOG2E)[:, None, None], (N_HEADS, chunk, 1))
    rest = jnp.full((N_HEADS, chunk, MXU_DIM - keys - 1), -jnp.inf, F32)
    return jnp.concatenate([alibi, sink_col, rest], axis=2)


def _attn_layer(x, g, w_in_prep, w_out, sinks, *, ns, tls, chunk, cache=None):
    nb, seq, _ = x.shape
    use_cache = cache is not None
    grid = (nb // ns, seq // tls)
    kvn = N_KV_HEADS * HEAD_DIM
    kvw = 2 * kvn
    rows = ns * tls
    const2 = lambda b, t: (0, 0)
    in_specs = [
        pl.BlockSpec((ns, tls, D_MODEL), lambda b, t: (b, t, 0)),
        pl.BlockSpec((1, D_MODEL), const2),
        pl.BlockSpec(w_in_prep.shape, const2, pipeline_mode=pl.Buffered(1)),
        pl.BlockSpec(w_out.shape, const2, pipeline_mode=pl.Buffered(1)),
        pl.BlockSpec((N_HEADS, chunk, MXU_DIM), lambda b, t: (0, 0, 0), pipeline_mode=pl.Buffered(1)),
    ]
    args = [x, g, w_in_prep, w_out, _attn_bias(chunk, sinks)]
    if use_cache:
        in_specs += [pl.BlockSpec((ns, WINDOW, kvn), lambda b, t: (b, 0, 0))] * 2
        args += list(cache)
    out_shape = (
        jax.ShapeDtypeStruct((nb, seq, D_MODEL), F32),
        jax.ShapeDtypeStruct((nb, WINDOW, kvn), F32),
        jax.ShapeDtypeStruct((nb, WINDOW, kvn), F32),
    )
    out_specs = (
        pl.BlockSpec((ns, tls, D_MODEL), lambda b, t: (b, t, 0)),
        pl.BlockSpec((ns, WINDOW, kvn), lambda b, t: (b, 0, 0)),
        pl.BlockSpec((ns, WINDOW, kvn), lambda b, t: (b, 0, 0)),
    )
    scratch = [
        pltpu.VMEM((rows, D_MODEL), BF16),
        pltpu.VMEM((rows, D_MODEL), F32),
        pltpu.VMEM((ns, WINDOW + tls, kvw), BF16),
        pltpu.VMEM((ns, WINDOW + tls, kvw), BF16),
        pltpu.VMEM((ns, WINDOW + tls, kvw), BF16),
        pltpu.VMEM((rows, D_MODEL), BF16),
    ]
    return pl.pallas_call(
        functools.partial(_attn_kernel, ns=ns, tls=tls, chunk=chunk, use_cache=use_cache),
        grid=grid, in_specs=in_specs, out_specs=out_specs, out_shape=out_shape,
        scratch_shapes=scratch,
        compiler_params=pltpu.CompilerParams(
            dimension_semantics=("arbitrary", "arbitrary"),
            vmem_limit_bytes=VMEM_LIMIT_BYTES),
        name="attn_cache" if use_cache else "attn_prompt",
    )(*args)


def _split3(x):
    hi = x.astype(BF16)
    r1 = x - hi.astype(F32)
    mid = r1.astype(BF16)
    lo = (r1 - mid.astype(F32)).astype(BF16)
    return hi, mid, lo


def _dn_kernel(*refs, ns, tls, chunk, use_state):
    rm = not use_state
    if use_state:
        (x_ref, g_ref, w1_ref, w2_ref, cw_ref, alog_ref, dtb_ref, ng_ref, wout_ref, fg_ref,
         tri_ref, cst_ref, sst_ref,
         y_ref, cout_ref, s_ref,
         cbuf, qn_s, kn_s, v_s, z_s, bt_s, gc_s, o_s, on_s,
         qg_s, kd_s, u_s, w_s, intra_s, sb_s) = refs
    else:
        (x_ref, g_ref, w1_ref, w2_ref, cw_ref, alog_ref, dtb_ref, ng_ref, wout_ref, fg_ref,
         tri_ref,
         y_ref, cout_ref, s_ref,
         cbuf, qn_s, kn_s, v_s, z_s, bt_s, gc_s, o_s, on_s,
         qg_s, kd_s, u_s, w_s, intra_s, sb_s, hist_s, xs_s, ys_s) = refs
        assert ns == 1 and chunk == SUBLANES * SUBLANES
    t = pl.program_id(1)
    rows = ns * tls
    hist = CONV_WIDTH - 1
    qk_w = DN_HEADS * DN_DIM
    conv_w = 3 * qk_w
    n_pairs = DN_HEADS // 2
    rres = tls // SUBLANES

    if use_state:
        s_ref[...] = sst_ref[...]
        cbuf[:, CONV_PAD - hist:CONV_PAD, :] = cst_ref[...]
        sb_s[...] = jnp.zeros(sb_s.shape, BF16)
        for s in range(ns):
            for hd in range(DN_HEADS):
                d0 = (hd % 2) * DN_DIM
                sb_s[s, hd // 2, d0:d0 + DN_DIM, d0:d0 + DN_DIM] = sst_ref[s, hd].astype(BF16)
    else:
        @pl.when(t == 0)
        def _():
            s_ref[...] = jnp.zeros(s_ref.shape, F32)
            sb_s[...] = jnp.zeros(sb_s.shape, BF16)
            hist_s[...] = jnp.zeros(hist_s.shape, F32)

    n_slabs = D_MODEL // LANES
    if rm:
        for j in range(n_slabs):
            xs_s[j * tls:(j + 1) * tls, :] = x_ref[0, :, j * LANES:(j + 1) * LANES]
        x = jnp.concatenate(
            [jnp.concatenate([xs_s[pl.ds(j * tls + r, rres, stride=SUBLANES), :] for j in range(n_slabs)], axis=1)
             for r in range(SUBLANES)], axis=0)
    else:
        x = x_ref[...].reshape(rows, D_MODEL)
    h = (x * _rms_scale(x) * g_ref[...]).astype(BF16)

    def phase_a():
        for c0 in range(0, conv_w, MXU_DIM):
            raw = _dot(h, w1_ref[:, c0:c0 + MXU_DIM])
            if rm:
                cbuf[:, c0:c0 + MXU_DIM] = raw
            else:
                cbuf[:, CONV_PAD:, c0:c0 + MXU_DIM] = raw.reshape(ns, tls, MXU_DIM)
            yield
        ba = _dot(h, w2_ref[...])
        bt_s[...] = _sigmoid(ba[:, :LANES])
        a = ba[:, LANES:] + dtb_ref[...]
        softplus = jnp.maximum(a, 0.0) + jnp.log1p(jnp.exp(-jnp.abs(a)))
        g = -jnp.exp(alog_ref[...]) * softplus
        g_hi, g_mid, g_lo = _split3(g)
        gcs = _dot(tri_ref[...], jnp.concatenate([g_hi, g_mid, g_lo], axis=1))
        gc_s[...] = gcs[:, :LANES] + gcs[:, LANES:2 * LANES] + gcs[:, 2 * LANES:]
        yield
        for c0 in range(0, qk_w, MXU_DIM):
            z_s[:, c0:c0 + MXU_DIM] = _dot(h, w1_ref[:, conv_w + c0:conv_w + c0 + MXU_DIM])
            yield

    def conv_block(s, cols):
        if not rm:
            acc = None
            for j in range(CONV_WIDTH):
                lo = CONV_PAD - hist + j
                term = cbuf[s, lo:lo + tls, cols] * cw_ref[j:j + 1, cols]
                acc = term if acc is None else acc + term
            return acc
        xb = cbuf[:, cols]
        wrapped = []
        for k in range(hist):
            r = SUBLANES - hist + k
            full = jnp.concatenate([hist_s[k, :, cols], xb[r * rres:(r + 1) * rres]], axis=0)
            wrapped.append(full[SUBLANES - 1:SUBLANES - 1 + rres])
        ext = jnp.concatenate(wrapped + [xb], axis=0)
        acc = None
        for j in range(CONV_WIDTH):
            term = ext[j * rres:j * rres + tls] * cw_ref[j:j + 1, cols]
            acc = term if acc is None else acc + term
        return acc

    def phase_b():
        for blk in range(conv_w // LANES):
            cols = slice(blk * LANES, (blk + 1) * LANES)
            seg, hb = divmod(blk, DN_HEADS)
            dst = slice(hb * LANES, (hb + 1) * LANES)
            for s in range(ns):
                acc = conv_block(s, cols)
                yv = acc * _sigmoid(acc)
                rs = slice(s * tls, (s + 1) * tls)
                if seg == 2:
                    v_s[rs, dst] = yv
                else:
                    inv = lax.rsqrt(jnp.sum(yv * yv, axis=-1, keepdims=True) + EPS)
                    if seg == 0:
                        qn_s[rs, dst] = yv * (inv * (DN_DIM ** -0.5))
                    else:
                        kn_s[rs, dst] = yv * inv
            if blk % 2 == 1:
                yield

    ga, gb = phase_a(), phase_b()
    for _ in range(qk_w // MXU_DIM):
        next(ga)
    _run_interleaved(gb, ga)

    if rm:
        for k in range(hist):
            r = SUBLANES - hist + k
            last = cbuf[(r + 1) * rres - SUBLANES:(r + 1) * rres, :]
            hist_s[k] = last
            cout_ref[0, k:k + 1, :] = last[SUBLANES - 1:SUBLANES]
    else:
        cout_ref[...] = cbuf[:, CONV_PAD + tls - hist:CONV_PAD + tls, :]

    hg = MXU_DIM // chunk
    n_groups = DN_HEADS // hg
    n_levels = int(round(math.log2(chunk)))
    ri = lax.broadcasted_iota(jnp.int32, (MXU_DIM, MXU_DIM), 0)
    ci = lax.broadcasted_iota(jnp.int32, (MXU_DIM, MXU_DIM), 1)

    def time_of(pos):
        p = pos % chunk
        return SUBLANES * (p % SUBLANES) + p // SUBLANES if rm else p

    tri_mask = (ri // chunk == ci // chunk) & (time_of(ri) >= time_of(ci))
    diag = ri == ci
    eye = jnp.where(diag, 1.0, 0.0).astype(F32)
    lane = lax.broadcasted_iota(jnp.int32, (MXU_DIM, LANES), 1)
    ones_rows = jnp.ones((2 * SUBLANES, LANES), BF16)
    n_chunks = tls // chunk

    def hcols(hd):
        return slice(hd * LANES, (hd + 1) * LANES)

    def crows(ref, s, c, cols):
        if rm:
            return jnp.concatenate(
                [ref[r * rres + SUBLANES * c:r * rres + SUBLANES * (c + 1), cols] for r in range(SUBLANES)], axis=0)
        f0 = s * tls + c * chunk
        return ref[f0:f0 + chunk, cols]

    def crow_last(ref, s, c, cols):
        r = (SUBLANES - 1) * rres + SUBLANES * c + SUBLANES - 1 if rm else s * tls + (c + 1) * chunk - 1
        return ref[r:r + 1, cols]

    def phase_c(groups):
        st = []
        for (s, c, grp) in groups:
            f0 = s * tls + c * chunk
            rsl = slice(f0, f0 + chunk)
            heads = list(range(grp * hg, (grp + 1) * hg))
            k_h, q_h, rhs_rows, gc_cols, beta_cols = [], [], [], [], []
            for hd in heads:
                beta_h = crows(bt_s, s, c, slice(hd, hd + 1))
                gc_h = crows(gc_s, s, c, slice(hd, hd + 1))
                gl_h = crow_last(gc_s, s, c, slice(hd, hd + 1))
                eg = jnp.exp(gc_h)
                kf = crows(kn_s, s, c, hcols(hd))
                qf = crows(qn_s, s, c, hcols(hd))
                vb = (crows(v_s, s, c, hcols(hd)) * beta_h).astype(BF16)
                kbg = (kf * (beta_h * eg)).astype(BF16)
                qg_s[rsl, hcols(hd)] = (qf * eg).astype(BF16)
                kd_s[rsl, hcols(hd)] = (kf * jnp.exp(gl_h - gc_h)).astype(BF16)
                k_h.append(kf.astype(BF16))
                q_h.append(qf.astype(BF16))
                rhs_rows.append(jnp.concatenate([vb, kbg], axis=1))
                gc_cols.append(gc_h)
                beta_cols.append(beta_h)
            gc_c = jnp.concatenate(gc_cols, axis=0)
            hi, mid, lo = (p.astype(F32) for p in _split3(gc_c))
            rmat = jnp.where(lane == 0, hi, jnp.where(lane == 1, mid, jnp.where(lane == 2, lo, 0.0)))
            gc_row = _dot_nt(ones_rows, rmat.astype(BF16))[0:1]
            k_st = jnp.concatenate(k_h, axis=0)
            q_st = jnp.concatenate(q_h, axis=0)
            st.append(dict(
                rsl=rsl, heads=heads, gi=(s * n_chunks + c) * n_groups + grp,
                rhs=jnp.concatenate(rhs_rows, axis=0),
                beta=jnp.concatenate(beta_cols, axis=0),
                gd=gc_c - gc_row,
                kq=_dot_nt(jnp.concatenate([k_st, q_st], axis=0), k_st)))
        yield
        for d in st:
            decay = jnp.exp(jnp.where(tri_mask, d.pop('gd'), -jnp.inf))
            kq = d.pop('kq')
            m = jnp.where(diag, 0.0, kq[:MXU_DIM] * decay * d.pop('beta'))
            intra_s[d['gi']] = (kq[MXU_DIM:] * decay).astype(BF16)
            mb = m.astype(BF16)
            d['p'] = eye - m
            d['q'] = _dot(mb, mb).astype(BF16)
        yield
        for lvl in range(1, n_levels):
            for d in st:
                qb = d['q']
                if lvl < n_levels - 1:
                    pq = _dot(jnp.concatenate([d['p'].astype(BF16), qb], axis=0), qb)
                    d['p'] = d['p'] + pq[:MXU_DIM]
                    d['q'] = pq[MXU_DIM:].astype(BF16)
                else:
                    d['p'] = d['p'] + _dot(d['p'].astype(BF16), qb)
            yield
        for d in st:
            uw = _dot(d['p'].astype(BF16), d['rhs'])
            for i, hd in enumerate(d['heads']):
                hs = slice(i * chunk, (i + 1) * chunk)
                u_s[d['rsl'], hcols(hd)] = uw[hs, :LANES]
                w_s[d['rsl'], hcols(hd)] = uw[hs, LANES:].astype(BF16)
        yield

    def phase_d(s, c):
        f0 = s * tls + c * chunk
        rsl = slice(f0, f0 + chunk)
        both = []
        for p in range(n_pairs):
            pc = slice(p * MXU_DIM, (p + 1) * MXU_DIM)
            lhs = jnp.concatenate([w_s[rsl, pc], qg_s[rsl, pc]], axis=0)
            both.append(_dot(lhs, sb_s[s, p]))
        yield
        vnb, outer = [], []
        for p in range(n_pairs):
            pc = slice(p * MXU_DIM, (p + 1) * MXU_DIM)
            v_new = (u_s[rsl, pc] - both[p][:chunk]).astype(BF16)
            for i in range(2):
                hd = 2 * p + i
                vh = v_new[:, i * LANES:(i + 1) * LANES]
                vnb.append(vh)
                outer.append(_dot_tn(kd_s[rsl, hcols(hd)], vh))
        for grp in range(n_groups):
            heads = list(range(grp * hg, (grp + 1) * hg))
            v_st = jnp.concatenate([vnb[hd] for hd in heads], axis=0)
            qs_st = jnp.concatenate(
                [both[hd // 2][chunk:, (hd % 2) * LANES:(hd % 2 + 1) * LANES] for hd in heads], axis=0)
            o_st = qs_st + _dot(intra_s[(s * n_chunks + c) * n_groups + grp], v_st)
            for i, hd in enumerate(heads):
                o_s[rsl, hcols(hd)] = o_st[i * chunk:(i + 1) * chunk]
        yield
        for hd in range(DN_HEADS):
            gl = crow_last(gc_s, s, c, slice(hd, hd + 1))
            s_new = s_ref[s, hd] * jnp.exp(gl) + outer[hd]
            s_ref[s, hd] = s_new
            d0 = (hd % 2) * DN_DIM
            sb_s[s, hd // 2, d0:d0 + DN_DIM, d0:d0 + DN_DIM] = s_new.astype(BF16)
        yield

    def phase_e(r0, r1):
        rsl = slice(r0, r1)
        for hd in range(DN_HEADS):
            o = o_s[rsl, hcols(hd)]
            if rm:
                zz = jnp.concatenate([crows(z_s, 0, c, hcols(hd)) for c in range(r0 // chunk, r1 // chunk)], axis=0)
            else:
                zz = z_s[rsl, hcols(hd)]
            on_s[rsl, hcols(hd)] = (o * _rms_scale(o) * ng_ref[...] * (zz * _sigmoid(zz))).astype(BF16)
            yield
        y1 = _dot(on_s[rsl, :], wout_ref[...])
        if rm:
            for i, c in enumerate(range(r0 // chunk, r1 // chunk)):
                for r in range(SUBLANES):
                    p0 = i * chunk + r * SUBLANES
                    for j in range(n_slabs):
                        ys_s[pl.ds(j * tls + c * chunk + r, SUBLANES, stride=SUBLANES), :] = (
                            y1[p0:p0 + SUBLANES, j * LANES:(j + 1) * LANES])
            y1 = jnp.concatenate([ys_s[j * tls + r0:j * tls + r1, :] for j in range(n_slabs)], axis=1)
            x2 = x_ref[0, r0:r1, :] + y1
            y_ref[0, r0:r1, :] = x2 * _rms_scale(x2) * fg_ref[...]
        else:
            x2 = x_ref[r0 // tls:r1 // tls].reshape(r1 - r0, D_MODEL) + y1
            yv = x2 * _rms_scale(x2) * fg_ref[...]
            y_ref[r0 // tls:r1 // tls] = yv.reshape((r1 - r0) // tls, tls, D_MODEL)
        yield

    units = [(s, c) for s in range(ns) for c in range(n_chunks)]
    upb = max(1, 4 // n_groups)
    batches = [units[i:i + upb] for i in range(0, len(units), upb)]
    brows = upb * chunk

    def c_of(batch):
        return phase_c([(s, c, grp) for (s, c) in batch for grp in range(n_groups)])

    def d_of(batch):
        if use_state:
            return [phase_d(s, c) for (s, c) in batch]
        return [_chain([phase_d(s, c) for (s, c) in batch])]

    _run_interleaved(c_of(batches[0]), ga)
    for i in range(1, len(batches)):
        extra = [phase_e((i - 2) * brows, (i - 1) * brows)] if i >= 2 else []
        _run_interleaved(c_of(batches[i]), *d_of(batches[i - 1]), *extra)
    last = len(batches) - 1
    extra = [phase_e((last - 1) * brows, last * brows)] if last >= 1 else []
    _run_interleaved(*d_of(batches[last]), *extra)
    _run_interleaved(phase_e(last * brows, (last + 1) * brows))


def _block_tri(rows, chunk, residue_major):
    q = np.arange(rows)
    tm = SUBLANES * (q % (rows // SUBLANES)) + q // (rows // SUBLANES) if residue_major else q
    m = (tm[:, None] // chunk == tm[None, :] // chunk) & (tm[:, None] >= tm[None, :])
    return jnp.asarray(m.astype(np.float32), BF16)


def _dn_layer(x, g, w1, w2, conv_w, alog, dtb, ng, w_out, fg, *, ns, tls, chunk, state=None):
    nb, seq, _ = x.shape
    use_state = state is not None
    grid = (nb // ns, seq // tls)
    rows = ns * tls
    hist = CONV_WIDTH - 1
    conv_dim = 3 * DN_HEADS * DN_DIM
    n_stacks = rows * DN_HEADS // MXU_DIM
    const2 = lambda b, t: (0, 0)
    tri = _block_tri(rows, chunk, not use_state)
    args = [x, g, w1, w2, conv_w, alog, dtb, ng, w_out, fg, tri]
    x_spec = pl.BlockSpec((ns, tls, D_MODEL), lambda b, t: (b, t, 0))
    in_specs = [x_spec]
    in_specs += [pl.BlockSpec(a.shape, const2, pipeline_mode=pl.Buffered(1)) for a in args[1:]]
    if use_state:
        in_specs += [
            pl.BlockSpec((None, ns, hist, conv_dim), lambda b, t: (0, b, 0, 0)),
            pl.BlockSpec((None, ns, DN_HEADS, DN_DIM, DN_DIM), lambda b, t: (0, b, 0, 0, 0)),
        ]
        args += list(state)
    out_shape = (
        jax.ShapeDtypeStruct((nb, seq, D_MODEL), F32),
        jax.ShapeDtypeStruct((nb, hist, conv_dim), F32),
        jax.ShapeDtypeStruct((nb, DN_HEADS, DN_DIM, DN_DIM), F32),
    )
    out_specs = (
        x_spec,
        pl.BlockSpec((ns, hist, conv_dim), lambda b, t: (b, 0, 0)),
        pl.BlockSpec((ns, DN_HEADS, DN_DIM, DN_DIM), lambda b, t: (b, 0, 0, 0)),
    )
    if use_state:
        cbuf = pltpu.VMEM((ns, CONV_PAD + tls, conv_dim), F32)
    else:
        cbuf = pltpu.VMEM((tls, conv_dim), F32)
    scratch = [
        cbuf,
        pltpu.VMEM((rows, D_MODEL), F32),
        pltpu.VMEM((rows, D_MODEL), F32),
        pltpu.VMEM((rows, D_MODEL), F32),
        pltpu.VMEM((rows, D_MODEL), F32),
        pltpu.VMEM((rows, LANES), F32),
        pltpu.VMEM((rows, LANES), F32),
        pltpu.VMEM((rows, D_MODEL), F32),
        pltpu.VMEM((rows, D_MODEL), BF16),
        pltpu.VMEM((rows, D_MODEL), BF16),
        pltpu.VMEM((rows, D_MODEL), BF16),
        pltpu.VMEM((rows, D_MODEL), F32),
        pltpu.VMEM((rows, D_MODEL), BF16),
        pltpu.VMEM((n_stacks, MXU_DIM, MXU_DIM), BF16),
        pltpu.VMEM((ns, DN_HEADS // 2, MXU_DIM, MXU_DIM), BF16),
    ]
    if not use_state:
        scratch.append(pltpu.VMEM((hist, SUBLANES, conv_dim), F32))
        scratch.append(pltpu.VMEM((D_MODEL // LANES * tls, LANES), F32))
        scratch.append(pltpu.VMEM((D_MODEL // LANES * tls, LANES), F32))
    return pl.pallas_call(
        functools.partial(_dn_kernel, ns=ns, tls=tls, chunk=chunk, use_state=use_state),
        grid=grid, in_specs=in_specs, out_specs=out_specs, out_shape=out_shape,
        scratch_shapes=scratch,
        compiler_params=pltpu.CompilerParams(
            dimension_semantics=("arbitrary", "arbitrary"),
            vmem_limit_bytes=VMEM_LIMIT_BYTES),
        name="dn_state" if use_state else "dn_prompt",
    )(*args)


def _pad_lanes(v):
    return jnp.pad(v.astype(F32)[None, :], ((0, 0), (0, LANES - v.shape[0])))


def kernel(x_prompt, x_sample, cache_k, cache_v, state_conv, state_ssm, norm_g, final_norm_g,
           attn_w_in, attn_sinks, attn_w_out, dn_w_in, dn_conv_w, dn_a_log, dn_dt_bias,
           dn_norm_g, dn_w_out):
    attn_width = N_HEADS * HEAD_DIM
    kv_width = N_KV_HEADS * HEAD_DIM
    col_scale = jnp.where(jnp.arange(attn_w_in.shape[-1]) < attn_width, HEAD_DIM ** -0.5 * LOG2E, 1.0).astype(F32)
    a_win = (attn_w_in[0] * col_scale).astype(BF16)
    a_wout = attn_w_out[0].astype(BF16)
    g0 = norm_g[0].reshape(1, D_MODEL)
    sinks = attn_sinks[0]

    wide = 3 * DN_HEADS * DN_DIM + DN_HEADS * DN_DIM
    dw = dn_w_in[0]
    d_w1 = dw.astype(BF16)
    lane_pad = ((0, 0), (0, LANES - DN_HEADS))
    d_w2 = jnp.concatenate([jnp.pad(dw[:, wide:wide + DN_HEADS], lane_pad),
                            jnp.pad(dw[:, wide + DN_HEADS:], lane_pad)], axis=1).astype(BF16)
    d_wout = dn_w_out[0].astype(BF16)
    g1 = norm_g[1].reshape(1, D_MODEL)
    fg = final_norm_g.reshape(1, D_MODEL)
    alog = _pad_lanes(dn_a_log[0])
    dtb = _pad_lanes(dn_dt_bias[0])
    ng = dn_norm_g[0].reshape(1, DN_DIM)
    dn_args = (g1, d_w1, d_w2, dn_conv_w[0], alog, dtb, ng, d_wout, fg)

    x1p, kp, vp = _attn_layer(x_prompt, g0, a_win, a_wout, sinks, ns=1, tls=ATTN_TILE_ROWS, chunk=CHUNK)
    yp, cp, sp = _dn_layer(x1p, *dn_args, ns=1, tls=DN_TILE_ROWS, chunk=CHUNK)

    n_run, dec_seq = x_sample.shape[:2]
    ck = cache_k[0].reshape(n_run, WINDOW, kv_width)
    cv = cache_v[0].reshape(n_run, WINDOW, kv_width)
    x1s, ks, vs = _attn_layer(x_sample, g0, a_win, a_wout, sinks, ns=RUN_SEQS_PER_STEP, tls=dec_seq,
                              chunk=min(CHUNK, dec_seq), cache=(ck, cv))
    ys, cs, ss = _dn_layer(x1s, *dn_args, ns=RUN_SEQS_PER_STEP, tls=dec_seq, chunk=min(CHUNK, dec_seq),
                           state=(state_conv, state_ssm))

    def kv_out(t):
        return t.reshape(1, t.shape[0], WINDOW, N_KV_HEADS, HEAD_DIM)

    return (yp, ys, kv_out(kp), kv_out(vp), cp[None], sp[None],
            kv_out(ks), kv_out(vs), cs[None], ss[None])
```

```python
import functools
import math

import numpy as np
import jax
import jax.numpy as jnp
from jax import lax
from jax.experimental import pallas as pl
from jax.experimental.pallas import tpu as pltpu

F32 = jnp.float32
BF16 = jnp.bfloat16

D_MODEL = 1024
EPS = 1e-6
CHUNK = 64
WINDOW = 128
N_HEADS = 16
N_KV_HEADS = 4
HEAD_DIM = 64
DN_HEADS = 8
DN_DIM = 128
CONV_WIDTH = 4
CONV_PAD = 8
SUBLANES = 8
LANES = 128
MXU_DIM = 256
VMEM_LIMIT_BYTES = 56 * 1024 * 1024
ATTN_TILE_ROWS = 1024
DN_TILE_ROWS = 512
RUN_SEQS_PER_STEP = 8
ATTN_LOCKSTEP = 4
LOG2E = 1.4426950408889634

_NT = (((1,), (1,)), ((), ()))
_TN = (((0,), (0,)), ((), ()))


def _dot(a, b):
    return jnp.dot(a, b, preferred_element_type=F32)


def _dot_nt(a, b):
    return lax.dot_general(a, b, _NT, preferred_element_type=F32)


def _dot_tn(a, b):
    return lax.dot_general(a, b, _TN, preferred_element_type=F32)


def _sigmoid(x):
    return 1.0 / (1.0 + jnp.exp2(x * -LOG2E))


def _rms_scale(x):
    return lax.rsqrt(jnp.mean(x * x, axis=-1, keepdims=True) + EPS)


def _run_interleaved(*gens):
    live = list(gens)
    while live:
        for g in list(live):
            try:
                next(g)
            except StopIteration:
                live.remove(g)


def _chain(gens):
    for g in gens:
        yield from g


def _attn_kernel(*refs, ns, tls, chunk, use_cache):
    if use_cache:
        (x_ref, g_ref, win_ref, wout_ref, bias_ref, ck_ref, cv_ref,
         x1_ref, kout_ref, vout_ref,
         q_s, gate_s, klo_s, khi_s, v_s, og_s) = refs
    else:
        (x_ref, g_ref, win_ref, wout_ref, bias_ref,
         x1_ref, kout_ref, vout_ref,
         q_s, gate_s, klo_s, khi_s, v_s, og_s) = refs
    t = pl.program_id(1)
    rows = ns * tls
    keys = WINDOW + chunk
    kvn = N_KV_HEADS * HEAD_DIM
    kvw = 2 * kvn

    half_lo = lax.broadcasted_iota(jnp.int32, (1, LANES), 1) < HEAD_DIM

    x = x_ref[...].reshape(rows, D_MODEL)
    h = (x * _rms_scale(x) * g_ref[...]).astype(BF16)
    q_s[...] = _dot(h, win_ref[:, 0:1024]).astype(BF16)
    kd = _dot(h, win_ref[:, 1024:1024 + kvn])
    vd = _dot(h, win_ref[:, 1024 + kvn:1024 + 2 * kvn])

    def gate_proj():
        c0 = 1024 + 2 * kvn
        for s0 in range(0, N_HEADS * HEAD_DIM, MXU_DIM):
            gate_s[:, s0:s0 + MXU_DIM] = _dot(h, win_ref[:, c0 + s0:c0 + s0 + MXU_DIM])
            yield

    def spread(k2, v2, nrows, row0):
        for b in range(N_KV_HEADS // 2):
            kb = k2[:, b * LANES:(b + 1) * LANES]
            vb = v2[:, b * LANES:(b + 1) * LANES]
            kr = pltpu.roll(kb, HEAD_DIM, 1)
            vr = pltpu.roll(vb, HEAD_DIM, 1)
            forms = (
                (klo_s, jnp.where(half_lo, kb, 0.0), jnp.where(half_lo, kr, 0.0)),
                (khi_s, jnp.where(half_lo, 0.0, kr), jnp.where(half_lo, 0.0, kb)),
                (v_s, jnp.where(half_lo, vb, vr), jnp.where(half_lo, vr, vb)),
            )
            for ref, even, odd in forms:
                for j, val in ((2 * b, even), (2 * b + 1, odd)):
                    ref[:, row0:row0 + nrows, j * LANES:(j + 1) * LANES] = (
                        val.astype(BF16).reshape(ns, nrows, LANES))
                yield

    if use_cache:
        _run_interleaved(spread(ck_ref[...].reshape(ns * WINDOW, kvn), cv_ref[...].reshape(ns * WINDOW, kvn),
                                WINDOW, 0))
    else:
        @pl.when(t == 0)
        def _():
            zeros = jnp.zeros((ns, WINDOW, kvw), BF16)
            klo_s[:, 0:WINDOW, :] = zeros
            khi_s[:, 0:WINDOW, :] = zeros
            v_s[:, 0:WINDOW, :] = zeros

    _run_interleaved(spread(kd, vd, tls, WINDOW), gate_proj())

    kd3 = kd.reshape(ns, tls, kvn)
    vd3 = vd.reshape(ns, tls, kvn)
    if use_cache:
        kout_ref[:, 0:WINDOW - tls, :] = ck_ref[:, tls:, :]
        vout_ref[:, 0:WINDOW - tls, :] = cv_ref[:, tls:, :]
        kout_ref[:, WINDOW - tls:, :] = kd3
        vout_ref[:, WINDOW - tls:, :] = vd3
    else:
        @pl.when(t == pl.num_programs(1) - 1)
        def _():
            kout_ref[...] = kd3[:, tls - WINDOW:, :]
            vout_ref[...] = vd3[:, tls - WINDOW:, :]

    pad = MXU_DIM - keys
    col = lax.broadcasted_iota(jnp.int32, (chunk, MXU_DIM), 1)
    pair_lo = lax.broadcasted_iota(jnp.int32, (chunk, LANES), 1) < HEAD_DIM
    ones_blk = jnp.ones((keys, LANES), BF16)
    k_pad = jnp.zeros((pad, LANES), BF16)
    v_pad = jnp.concatenate([jnp.zeros((pad, LANES), BF16), jnp.ones((pad, LANES), BF16)], axis=1)
    n_chunks = tls // chunk

    def attend(s, c):
        r0 = c * chunk
        f0 = s * tls + r0
        masked = (not use_cache) and r0 < WINDOW
        if masked:
            n_inv = jnp.maximum(0, WINDOW - (t * tls + r0))
            valid = col >= n_inv
        scores = []
        for j in range(N_KV_HEADS):
            qj = q_s[f0:f0 + chunk, j * 256:(j + 1) * 256]
            lhs = jnp.concatenate([qj[:, :LANES], qj[:, LANES:]], axis=0)
            klo = jnp.concatenate([klo_s[s, r0:r0 + keys, j * LANES:(j + 1) * LANES], k_pad], axis=0)
            khi = jnp.concatenate([khi_s[s, r0:r0 + keys, j * LANES:(j + 1) * LANES], k_pad], axis=0)
            s_even = _dot_nt(lhs, klo)
            s_odd = _dot_nt(lhs, khi)
            scores.append((s_even[:chunk], s_odd[:chunk], s_even[chunk:], s_odd[chunk:]))
        yield
        ovs = []
        for j in range(N_KV_HEADS):
            ps = []
            for g in range(4):
                sg = scores[j][g] + bias_ref[4 * j + g]
                if masked:
                    sg = jnp.where(valid, sg, -jnp.inf)
                m = jnp.max(sg, axis=-1, keepdims=True)
                ps.append(jnp.exp2(sg - m).astype(BF16))
            p_all = jnp.concatenate(ps, axis=0)
            vj = v_s[s, r0:r0 + keys, j * LANES:(j + 1) * LANES]
            vaug = jnp.concatenate([jnp.concatenate([vj, ones_blk], axis=1), v_pad], axis=0)
            ovs.append(_dot(p_all, vaug))
        yield
        for j in range(N_KV_HEADS):
            outs = []
            for g in range(4):
                blk = ovs[j][g * chunk:(g + 1) * chunk]
                outs.append(blk[:, :LANES] / blk[:, LANES:])
            o01 = jnp.where(pair_lo, outs[0], outs[1])
            o23 = jnp.where(pair_lo, outs[2], outs[3])
            gt = gate_s[f0:f0 + chunk, j * 256:(j + 1) * 256]
            og = jnp.concatenate([o01, o23], axis=1) * (gt * _sigmoid(gt))
            og_s[f0:f0 + chunk, j * 256:(j + 1) * 256] = og.astype(BF16)
        yield

    units = [(s, c) for s in range(ns) for c in range(n_chunks)]
    for i in range(0, len(units), ATTN_LOCKSTEP):
        _run_interleaved(*[attend(s, c) for s, c in units[i:i + ATTN_LOCKSTEP]])

    if not use_cache:
        klo_s[:, 0:WINDOW, :] = klo_s[:, tls:tls + WINDOW, :]
        khi_s[:, 0:WINDOW, :] = khi_s[:, tls:tls + WINDOW, :]
        v_s[:, 0:WINDOW, :] = v_s[:, tls:tls + WINDOW, :]

    y = _dot(og_s[...], wout_ref[...])
    x1_ref[...] = (x + y).reshape(ns, tls, D_MODEL)


def _attn_bias(chunk, sinks):
    keys = WINDOW + chunk
    slopes = (2.0 ** (-8.0 * np.arange(1, N_HEADS + 1, dtype=np.float32) / N_HEADS)).astype(np.float32)
    i = np.arange(chunk, dtype=np.float32)[:, None]
    j = np.arange(keys, dtype=np.float32)[None, :]
    dist = np.abs(i + WINDOW - j).astype(np.float32)
    alibi = jnp.asarray(-(slopes[:, None, None] * dist[None]), F32) * LOG2E
    sink_col = jnp.broadcast_to((sinks.astype(F32) * LOG2E)[:, None, None], (N_HEADS, chunk, 1))
    rest = jnp.full((N_HEADS, chunk, MXU_DIM - keys - 1), -jnp.inf, F32)
    return jnp.concatenate([alibi, sink_col, rest], axis=2)


def _attn_layer(x, g, w_in_prep, w_out, sinks, *, ns, tls, chunk, cache=None):
    nb, seq, _ = x.shape
    use_cache = cache is not None
    grid = (nb // ns, seq // tls)
    kvn = N_KV_HEADS * HEAD_DIM
    kvw = 2 * kvn
    rows = ns * tls
    const2 = lambda b, t: (0, 0)
    in_specs = [
        pl.BlockSpec((ns, tls, D_MODEL), lambda b, t: (b, t, 0)),
        pl.BlockSpec((1, D_MODEL), const2),
        pl.BlockSpec(w_in_prep.shape, const2, pipeline_mode=pl.Buffered(1)),
        pl.BlockSpec(w_out.shape, const2, pipeline_mode=pl.Buffered(1)),
        pl.BlockSpec((N_HEADS, chunk, MXU_DIM), lambda b, t: (0, 0, 0), pipeline_mode=pl.Buffered(1)),
    ]
    args = [x, g, w_in_prep, w_out, _attn_bias(chunk, sinks)]
    if use_cache:
        in_specs += [pl.BlockSpec((ns, WINDOW, kvn), lambda b, t: (b, 0, 0))] * 2
        args += list(cache)
    out_shape = (
        jax.ShapeDtypeStruct((nb, seq, D_MODEL), F32),
        jax.ShapeDtypeStruct((nb, WINDOW, kvn), F32),
        jax.ShapeDtypeStruct((nb, WINDOW, kvn), F32),
    )
    out_specs = (
        pl.BlockSpec((ns, tls, D_MODEL), lambda b, t: (b, t, 0)),
        pl.BlockSpec((ns, WINDOW, kvn), lambda b, t: (b, 0, 0)),
        pl.BlockSpec((ns, WINDOW, kvn), lambda b, t: (b, 0, 0)),
    )
    scratch = [
        pltpu.VMEM((rows, D_MODEL), BF16),
        pltpu.VMEM((rows, D_MODEL), F32),
        pltpu.VMEM((ns, WINDOW + tls, kvw), BF16),
        pltpu.VMEM((ns, WINDOW + tls, kvw), BF16),
        pltpu.VMEM((ns, WINDOW + tls, kvw), BF16),
        pltpu.VMEM((rows, D_MODEL), BF16),
    ]
    return pl.pallas_call(
        functools.partial(_attn_kernel, ns=ns, tls=tls, chunk=chunk, use_cache=use_cache),
        grid=grid, in_specs=in_specs, out_specs=out_specs, out_shape=out_shape,
        scratch_shapes=scratch,
        compiler_params=pltpu.CompilerParams(
            dimension_semantics=("arbitrary", "arbitrary"),
            vmem_limit_bytes=VMEM_LIMIT_BYTES),
        name="attn_cache" if use_cache else "attn_prompt",
    )(*args)


def _split3(x):
    hi = x.astype(BF16)
    r1 = x - hi.astype(F32)
    mid = r1.astype(BF16)
    lo = (r1 - mid.astype(F32)).astype(BF16)
    return hi, mid, lo


def _dn_kernel(*refs, ns, tls, chunk, use_state):
    rm = not use_state
    if use_state:
        (x_ref, g_ref, w1_ref, w2_ref, cw_ref, alog_ref, dtb_ref, ng_ref, wout_ref, fg_ref,
         tri_ref, cst_ref, sst_ref,
         y_ref, cout_ref, s_ref,
         cbuf, qn_s, kn_s, v_s, z_s, bt_s, gc_s, o_s, on_s,
         qg_s, kd_s, u_s, w_s, intra_s, sb_s) = refs
    else:
        (x_ref, g_ref, w1_ref, w2_ref, cw_ref, alog_ref, dtb_ref, ng_ref, wout_ref, fg_ref,
         tri_ref,
         y_ref, cout_ref, s_ref,
         cbuf, qn_s, kn_s, v_s, z_s, bt_s, gc_s, o_s, on_s,
         qg_s, kd_s, u_s, w_s, intra_s, sb_s, hist_s, xs_s, ys_s) = refs
        assert ns == 1 and chunk == SUBLANES * SUBLANES
    t = pl.program_id(1)
    rows = ns * tls
    hist = CONV_WIDTH - 1
    qk_w = DN_HEADS * DN_DIM
    conv_w = 3 * qk_w
    n_pairs = DN_HEADS // 2
    rres = tls // SUBLANES

    if use_state:
        s_ref[...] = sst_ref[...]
        cbuf[:, CONV_PAD - hist:CONV_PAD, :] = cst_ref[...]
        sb_s[...] = jnp.zeros(sb_s.shape, BF16)
        for s in range(ns):
            for hd in range(DN_HEADS):
                d0 = (hd % 2) * DN_DIM
                sb_s[s, hd // 2, d0:d0 + DN_DIM, d0:d0 + DN_DIM] = sst_ref[s, hd].astype(BF16)
    else:
        @pl.when(t == 0)
        def _():
            s_ref[...] = jnp.zeros(s_ref.shape, F32)
            sb_s[...] = jnp.zeros(sb_s.shape, BF16)
            hist_s[...] = jnp.zeros(hist_s.shape, F32)

    n_slabs = D_MODEL // LANES
    if rm:
        for j in range(n_slabs):
            xs_s[j * tls:(j + 1) * tls, :] = x_ref[0, :, j * LANES:(j + 1) * LANES]

        def normed_rows(r_lo, r_hi):
            xr = jnp.concatenate(
                [jnp.concatenate([xs_s[pl.ds(j * tls + r, rres, stride=SUBLANES), :] for j in range(n_slabs)], axis=1)
                 for r in range(r_lo, r_hi)], axis=0)
            return (xr * _rms_scale(xr) * g_ref[...]).astype(BF16)

        n_first = qk_w // MXU_DIM
        h_a = normed_rows(0, SUBLANES // 2)
        raw_a = [_dot(h_a, w1_ref[:, c * MXU_DIM:(c + 1) * MXU_DIM]) for c in range(n_first)]
        h_b = normed_rows(SUBLANES // 2, SUBLANES)
        for c in range(n_first):
            raw_b = _dot(h_b, w1_ref[:, c * MXU_DIM:(c + 1) * MXU_DIM])
            cbuf[:, c * MXU_DIM:(c + 1) * MXU_DIM] = jnp.concatenate([raw_a[c], raw_b], axis=0)
        h = jnp.concatenate([h_a, h_b], axis=0)
    else:
        n_first = 0
        x = x_ref[...].reshape(rows, D_MODEL)
        h = (x * _rms_scale(x) * g_ref[...]).astype(BF16)

    def phase_a():
        for c0 in range(n_first * MXU_DIM, conv_w, MXU_DIM):
            raw = _dot(h, w1_ref[:, c0:c0 + MXU_DIM])
            if rm:
                cbuf[:, c0:c0 + MXU_DIM] = raw
            else:
                cbuf[:, CONV_PAD:, c0:c0 + MXU_DIM] = raw.reshape(ns, tls, MXU_DIM)
            yield
        ba = _dot(h, w2_ref[...])
        bt_s[...] = _sigmoid(ba[:, :LANES])
        a = ba[:, LANES:] + dtb_ref[...]
        softplus = jnp.maximum(a, 0.0) + jnp.log1p(jnp.exp(-jnp.abs(a)))
        g = -jnp.exp(alog_ref[...]) * softplus
        g_hi, g_mid, g_lo = _split3(g)
        gcs = _dot(tri_ref[...], jnp.concatenate([g_hi, g_mid, g_lo], axis=1))
        gc_s[...] = gcs[:, :LANES] + gcs[:, LANES:2 * LANES] + gcs[:, 2 * LANES:]
        yield
        for c0 in range(0, qk_w, MXU_DIM):
            z_s[:, c0:c0 + MXU_DIM] = _dot(h, w1_ref[:, conv_w + c0:conv_w + c0 + MXU_DIM])
            yield

    def conv_block(s, cols):
        if not rm:
            acc = None
            for j in range(CONV_WIDTH):
                lo = CONV_PAD - hist + j
                term = cbuf[s, lo:lo + tls, cols] * cw_ref[j:j + 1, cols]
                acc = term if acc is None else acc + term
            return acc
        xb = cbuf[:, cols]
        wrapped = []
        for k in range(hist):
            r = SUBLANES - hist + k
            full = jnp.concatenate([hist_s[k, :, cols], xb[r * rres:(r + 1) * rres]], axis=0)
            wrapped.append(full[SUBLANES - 1:SUBLANES - 1 + rres])
        ext = jnp.concatenate(wrapped + [xb], axis=0)
        acc = None
        for j in range(CONV_WIDTH):
            term = ext[j * rres:j * rres + tls] * cw_ref[j:j + 1, cols]
            acc = term if acc is None else acc + term
        return acc

    def phase_b():
        for blk in range(conv_w // LANES):
            cols = slice(blk * LANES, (blk + 1) * LANES)
            seg, hb = divmod(blk, DN_HEADS)
            dst = slice(hb * LANES, (hb + 1) * LANES)
            for s in range(ns):
                acc = conv_block(s, cols)
                yv = acc * _sigmoid(acc)
                rs = slice(s * tls, (s + 1) * tls)
                if seg == 2:
                    v_s[rs, dst] = yv
                else:
                    inv = lax.rsqrt(jnp.sum(yv * yv, axis=-1, keepdims=True) + EPS)
                    if seg == 0:
                        qn_s[rs, dst] = yv * (inv * (DN_DIM ** -0.5))
                    else:
                        kn_s[rs, dst] = yv * inv
            if blk % 2 == 1:
                yield

    ga, gb = phase_a(), phase_b()
    for _ in range(qk_w // MXU_DIM - n_first):
        next(ga)
    _run_interleaved(gb, ga)

    if rm:
        for k in range(hist):
            r = SUBLANES - hist + k
            last = cbuf[(r + 1) * rres - SUBLANES:(r + 1) * rres, :]
            hist_s[k] = last
            cout_ref[0, k:k + 1, :] = last[SUBLANES - 1:SUBLANES]
    else:
        cout_ref[...] = cbuf[:, CONV_PAD + tls - hist:CONV_PAD + tls, :]

    hg = MXU_DIM // chunk
    n_groups = DN_HEADS // hg
    n_levels = int(round(math.log2(chunk)))
    ri = lax.broadcasted_iota(jnp.int32, (MXU_DIM, MXU_DIM), 0)
    ci = lax.broadcasted_iota(jnp.int32, (MXU_DIM, MXU_DIM), 1)

    def time_of(pos):
        p = pos % chunk
        return SUBLANES * (p % SUBLANES) + p // SUBLANES if rm else p

    tri_mask = (ri // chunk == ci // chunk) & (time_of(ri) >= time_of(ci))
    diag = ri == ci
    eye = jnp.where(diag, 1.0, 0.0).astype(F32)
    lane = lax.broadcasted_iota(jnp.int32, (MXU_DIM, LANES), 1)
    ones_rows = jnp.ones((2 * SUBLANES, LANES), BF16)
    n_chunks = tls // chunk

    def hcols(hd):
        return slice(hd * LANES, (hd + 1) * LANES)

    def crows(ref, s, c, cols):
        if rm:
            return jnp.concatenate(
                [ref[r * rres + SUBLANES * c:r * rres + SUBLANES * (c + 1), cols] for r in range(SUBLANES)], axis=0)
        f0 = s * tls + c * chunk
        return ref[f0:f0 + chunk, cols]

    def crow_last(ref, s, c, cols):
        r = (SUBLANES - 1) * rres + SUBLANES * c + SUBLANES - 1 if rm else s * tls + (c + 1) * chunk - 1
        return ref[r:r + 1, cols]

    def phase_c(groups):
        st = []
        for (s, c, grp) in groups:
            f0 = s * tls + c * chunk
            rsl = slice(f0, f0 + chunk)
            heads = list(range(grp * hg, (grp + 1) * hg))
            k_h, q_h, rhs_rows, gc_cols, beta_cols = [], [], [], [], []
            for hd in heads:
                beta_h = crows(bt_s, s, c, slice(hd, hd + 1))
                gc_h = crows(gc_s, s, c, slice(hd, hd + 1))
                gl_h = crow_last(gc_s, s, c, slice(hd, hd + 1))
                eg = jnp.exp(gc_h)
                kf = crows(kn_s, s, c, hcols(hd))
                qf = crows(qn_s, s, c, hcols(hd))
                vb = (crows(v_s, s, c, hcols(hd)) * beta_h).astype(BF16)
                kbg = (kf * (beta_h * eg)).astype(BF16)
                qg_s[rsl, hcols(hd)] = (qf * eg).astype(BF16)
                kd_s[rsl, hcols(hd)] = (kf * jnp.exp(gl_h - gc_h)).astype(BF16)
                k_h.append(kf.astype(BF16))
                q_h.append(qf.astype(BF16))
                rhs_rows.append(jnp.concatenate([vb, kbg], axis=1))
                gc_cols.append(gc_h)
                beta_cols.append(beta_h)
            gc_c = jnp.concatenate(gc_cols, axis=0)
            hi, mid, lo = (p.astype(F32) for p in _split3(gc_c))
            rmat = jnp.where(lane == 0, hi, jnp.where(lane == 1, mid, jnp.where(lane == 2, lo, 0.0)))
            gc_row = _dot_nt(ones_rows, rmat.astype(BF16))[0:1]
            k_st = jnp.concatenate(k_h, axis=0)
            q_st = jnp.concatenate(q_h, axis=0)
            st.append(dict(
                rsl=rsl, heads=heads, gi=(s * n_chunks + c) * n_groups + grp,
                rhs=jnp.concatenate(rhs_rows, axis=0),
                beta=jnp.concatenate(beta_cols, axis=0),
                gd=gc_c - gc_row,
                kq=_dot_nt(jnp.concatenate([k_st, q_st], axis=0), k_st)))
        yield
        for d in st:
            decay = jnp.exp(jnp.where(tri_mask, d.pop('gd'), -jnp.inf))
            kq = d.pop('kq')
            m = jnp.where(diag, 0.0, kq[:MXU_DIM] * decay * d.pop('beta'))
            intra_s[d['gi']] = (kq[MXU_DIM:] * decay).astype(BF16)
            mb = m.astype(BF16)
            d['p'] = eye - m
            d['q'] = _dot(mb, mb).astype(BF16)
        yield
        for lvl in range(1, n_levels):
            for d in st:
                qb = d['q']
                if lvl < n_levels - 1:
                    pq = _dot(jnp.concatenate([d['p'].astype(BF16), qb], axis=0), qb)
                    d['p'] = d['p'] + pq[:MXU_DIM]
                    d['q'] = pq[MXU_DIM:].astype(BF16)
                else:
                    d['p'] = d['p'] + _dot(d['p'].astype(BF16), qb)
            yield
        for d in st:
            uw = _dot(d['p'].astype(BF16), d['rhs'])
            for i, hd in enumerate(d['heads']):
                hs = slice(i * chunk, (i + 1) * chunk)
                u_s[d['rsl'], hcols(hd)] = uw[hs, :LANES]
                w_s[d['rsl'], hcols(hd)] = uw[hs, LANES:].astype(BF16)
        yield

    def phase_d(s, c):
        f0 = s * tls + c * chunk
        rsl = slice(f0, f0 + chunk)
        both = []
        for p in range(n_pairs):
            pc = slice(p * MXU_DIM, (p + 1) * MXU_DIM)
            lhs = jnp.concatenate([w_s[rsl, pc], qg_s[rsl, pc]], axis=0)
            both.append(_dot(lhs, sb_s[s, p]))
        yield
        vnb, outer = [], []
        for p in range(n_pairs):
            pc = slice(p * MXU_DIM, (p + 1) * MXU_DIM)
            v_new = (u_s[rsl, pc] - both[p][:chunk]).astype(BF16)
            for i in range(2):
                hd = 2 * p + i
                vh = v_new[:, i * LANES:(i + 1) * LANES]
                vnb.append(vh)
                outer.append(_dot_tn(kd_s[rsl, hcols(hd)], vh))
        for grp in range(n_groups):
            heads = list(range(grp * hg, (grp + 1) * hg))
            v_st = jnp.concatenate([vnb[hd] for hd in heads], axis=0)
            qs_st = jnp.concatenate(
                [both[hd // 2][chunk:, (hd % 2) * LANES:(hd % 2 + 1) * LANES] for hd in heads], axis=0)
            o_st = qs_st + _dot(intra_s[(s * n_chunks + c) * n_groups + grp], v_st)
            for i, hd in enumerate(heads):
                o_s[rsl, hcols(hd)] = o_st[i * chunk:(i + 1) * chunk]
        yield
        for hd in range(DN_HEADS):
            gl = crow_last(gc_s, s, c, slice(hd, hd + 1))
            s_new = s_ref[s, hd] * jnp.exp(gl) + outer[hd]
            s_ref[s, hd] = s_new
            d0 = (hd % 2) * DN_DIM
            sb_s[s, hd // 2, d0:d0 + DN_DIM, d0:d0 + DN_DIM] = s_new.astype(BF16)
        yield

    def phase_e(r0, r1):
        rsl = slice(r0, r1)
        for hd in range(DN_HEADS):
            o = o_s[rsl, hcols(hd)]
            if rm:
                zz = jnp.concatenate([crows(z_s, 0, c, hcols(hd)) for c in range(r0 // chunk, r1 // chunk)], axis=0)
            else:
                zz = z_s[rsl, hcols(hd)]
            on_s[rsl, hcols(hd)] = (o * _rms_scale(o) * ng_ref[...] * (zz * _sigmoid(zz))).astype(BF16)
            yield
        y1 = _dot(on_s[rsl, :], wout_ref[...])
        if rm:
            for i, c in enumerate(range(r0 // chunk, r1 // chunk)):
                for r in range(SUBLANES):
                    p0 = i * chunk + r * SUBLANES
                    for j in range(n_slabs):
                        ys_s[pl.ds(j * tls + c * chunk + r, SUBLANES, stride=SUBLANES), :] = (
                            y1[p0:p0 + SUBLANES, j * LANES:(j + 1) * LANES])
            y1 = jnp.concatenate([ys_s[j * tls + r0:j * tls + r1, :] for j in range(n_slabs)], axis=1)
            x2 = x_ref[0, r0:r1, :] + y1
            y_ref[0, r0:r1, :] = x2 * _rms_scale(x2) * fg_ref[...]
        else:
            x2 = x_ref[r0 // tls:r1 // tls].reshape(r1 - r0, D_MODEL) + y1
            yv = x2 * _rms_scale(x2) * fg_ref[...]
            y_ref[r0 // tls:r1 // tls] = yv.reshape((r1 - r0) // tls, tls, D_MODEL)
        yield

    units = [(s, c) for s in range(ns) for c in range(n_chunks)]
    upb = max(1, 4 // n_groups)
    batches = [units[i:i + upb] for i in range(0, len(units), upb)]
    brows = upb * chunk

    def c_of(batch):
        return phase_c([(s, c, grp) for (s, c) in batch for grp in range(n_groups)])

    def d_of(batch):
        if use_state:
            return [phase_d(s, c) for (s, c) in batch]
        return [_chain([phase_d(s, c) for (s, c) in batch])]

    _run_interleaved(c_of(batches[0]), ga)
    def e_of(i):
        return phase_e(i * brows, (i + 1) * brows)

    for i in range(1, len(batches)):
        extra = [e_of(i - 3)] if i >= 3 else []
        _run_interleaved(c_of(batches[i]), *d_of(batches[i - 1]), *extra)
    last = len(batches) - 1
    _run_interleaved(*d_of(batches[last]), *[e_of(i) for i in range(max(0, last - 2), last)])
    _run_interleaved(e_of(last))


def _block_tri(rows, chunk, residue_major):
    q = np.arange(rows)
    tm = SUBLANES * (q % (rows // SUBLANES)) + q // (rows // SUBLANES) if residue_major else q
    m = (tm[:, None] // chunk == tm[None, :] // chunk) & (tm[:, None] >= tm[None, :])
    return jnp.asarray(m.astype(np.float32), BF16)


def _dn_layer(x, g, w1, w2, conv_w, alog, dtb, ng, w_out, fg, *, ns, tls, chunk, state=None):
    nb, seq, _ = x.shape
    use_state = state is not None
    grid = (nb // ns, seq // tls)
    rows = ns * tls
    hist = CONV_WIDTH - 1
    conv_dim = 3 * DN_HEADS * DN_DIM
    n_stacks = rows * DN_HEADS // MXU_DIM
    const2 = lambda b, t: (0, 0)
    tri = _block_tri(rows, chunk, not use_state)
    args = [x, g, w1, w2, conv_w, alog, dtb, ng, w_out, fg, tri]
    x_spec = pl.BlockSpec((ns, tls, D_MODEL), lambda b, t: (b, t, 0))
    in_specs = [x_spec]
    in_specs += [pl.BlockSpec(a.shape, const2, pipeline_mode=pl.Buffered(1)) for a in args[1:]]
    if use_state:
        in_specs += [
            pl.BlockSpec((None, ns, hist, conv_dim), lambda b, t: (0, b, 0, 0)),
            pl.BlockSpec((None, ns, DN_HEADS, DN_DIM, DN_DIM), lambda b, t: (0, b, 0, 0, 0)),
        ]
        args += list(state)
    out_shape = (
        jax.ShapeDtypeStruct((nb, seq, D_MODEL), F32),
        jax.ShapeDtypeStruct((nb, hist, conv_dim), F32),
        jax.ShapeDtypeStruct((nb, DN_HEADS, DN_DIM, DN_DIM), F32),
    )
    out_specs = (
        x_spec,
        pl.BlockSpec((ns, hist, conv_dim), lambda b, t: (b, 0, 0)),
        pl.BlockSpec((ns, DN_HEADS, DN_DIM, DN_DIM), lambda b, t: (b, 0, 0, 0)),
    )
    if use_state:
        cbuf = pltpu.VMEM((ns, CONV_PAD + tls, conv_dim), F32)
    else:
        cbuf = pltpu.VMEM((tls, conv_dim), F32)
    scratch = [
        cbuf,
        pltpu.VMEM((rows, D_MODEL), F32),
        pltpu.VMEM((rows, D_MODEL), F32),
        pltpu.VMEM((rows, D_MODEL), F32),
        pltpu.VMEM((rows, D_MODEL), F32),
        pltpu.VMEM((rows, LANES), F32),
        pltpu.VMEM((rows, LANES), F32),
        pltpu.VMEM((rows, D_MODEL), F32),
        pltpu.VMEM((rows, D_MODEL), BF16),
        pltpu.VMEM((rows, D_MODEL), BF16),
        pltpu.VMEM((rows, D_MODEL), BF16),
        pltpu.VMEM((rows, D_MODEL), F32),
        pltpu.VMEM((rows, D_MODEL), BF16),
        pltpu.VMEM((n_stacks, MXU_DIM, MXU_DIM), BF16),
        pltpu.VMEM((ns, DN_HEADS // 2, MXU_DIM, MXU_DIM), BF16),
    ]
    if not use_state:
        scratch.append(pltpu.VMEM((hist, SUBLANES, conv_dim), F32))
        scratch.append(pltpu.VMEM((D_MODEL // LANES * tls, LANES), F32))
        scratch.append(pltpu.VMEM((D_MODEL // LANES * tls, LANES), F32))
    return pl.pallas_call(
        functools.partial(_dn_kernel, ns=ns, tls=tls, chunk=chunk, use_state=use_state),
        grid=grid, in_specs=in_specs, out_specs=out_specs, out_shape=out_shape,
        scratch_shapes=scratch,
        compiler_params=pltpu.CompilerParams(
            dimension_semantics=("arbitrary", "arbitrary"),
            vmem_limit_bytes=VMEM_LIMIT_BYTES),
        name="dn_state" if use_state else "dn_prompt",
    )(*args)


def _pad_lanes(v):
    return jnp.pad(v.astype(F32)[None, :], ((0, 0), (0, LANES - v.shape[0])))


def kernel(x_prompt, x_sample, cache_k, cache_v, state_conv, state_ssm, norm_g, final_norm_g,
           attn_w_in, attn_sinks, attn_w_out, dn_w_in, dn_conv_w, dn_a_log, dn_dt_bias,
           dn_norm_g, dn_w_out):
    attn_width = N_HEADS * HEAD_DIM
    kv_width = N_KV_HEADS * HEAD_DIM
    col_scale = jnp.where(jnp.arange(attn_w_in.shape[-1]) < attn_width, HEAD_DIM ** -0.5 * LOG2E, 1.0).astype(F32)
    a_win = (attn_w_in[0] * col_scale).astype(BF16)
    a_wout = attn_w_out[0].astype(BF16)
    g0 = norm_g[0].reshape(1, D_MODEL)
    sinks = attn_sinks[0]

    wide = 3 * DN_HEADS * DN_DIM + DN_HEADS * DN_DIM
    dw = dn_w_in[0]
    d_w1 = dw.astype(BF16)
    lane_pad = ((0, 0), (0, LANES - DN_HEADS))
    d_w2 = jnp.concatenate([jnp.pad(dw[:, wide:wide + DN_HEADS], lane_pad),
                            jnp.pad(dw[:, wide + DN_HEADS:], lane_pad)], axis=1).astype(BF16)
    d_wout = dn_w_out[0].astype(BF16)
    g1 = norm_g[1].reshape(1, D_MODEL)
    fg = final_norm_g.reshape(1, D_MODEL)
    alog = _pad_lanes(dn_a_log[0])
    dtb = _pad_lanes(dn_dt_bias[0])
    ng = dn_norm_g[0].reshape(1, DN_DIM)
    dn_args = (g1, d_w1, d_w2, dn_conv_w[0], alog, dtb, ng, d_wout, fg)

    x1p, kp, vp = _attn_layer(x_prompt, g0, a_win, a_wout, sinks, ns=1, tls=ATTN_TILE_ROWS, chunk=CHUNK)
    yp, cp, sp = _dn_layer(x1p, *dn_args, ns=1, tls=DN_TILE_ROWS, chunk=CHUNK)

    n_run, dec_seq = x_sample.shape[:2]
    ck = cache_k[0].reshape(n_run, WINDOW, kv_width)
    cv = cache_v[0].reshape(n_run, WINDOW, kv_width)
    x1s, ks, vs = _attn_layer(x_sample, g0, a_win, a_wout, sinks, ns=RUN_SEQS_PER_STEP, tls=dec_seq,
                              chunk=min(CHUNK, dec_seq), cache=(ck, cv))
    ys, cs, ss = _dn_layer(x1s, *dn_args, ns=RUN_SEQS_PER_STEP, tls=dec_seq, chunk=min(CHUNK, dec_seq),
                           state=(state_conv, state_ssm))

    def kv_out(t):
        return t.reshape(1, t.shape[0], WINDOW, N_KV_HEADS, HEAD_DIM)

    return (yp, ys, kv_out(kp), kv_out(vp), cp[None], sp[None],
            kv_out(ks), kv_out(vs), cs[None], ss[None])
```

```python
import functools
import math

import numpy as np
import jax
import jax.numpy as jnp
from jax import lax
from jax.experimental import pallas as pl
from jax.experimental.pallas import tpu as pltpu

F32 = jnp.float32
BF16 = jnp.bfloat16

D_MODEL = 1024
EPS = 1e-6
CHUNK = 64
WINDOW = 128
N_HEADS = 16
N_KV_HEADS = 4
HEAD_DIM = 64
DN_HEADS = 8
DN_DIM = 128
CONV_WIDTH = 4
CONV_PAD = 8
SUBLANES = 8
LANES = 128
MXU_DIM = 256
VMEM_LIMIT_BYTES = 56 * 1024 * 1024
ATTN_TILE_ROWS = 1024
DN_TILE_ROWS = 512
RUN_SEQS_PER_STEP = 8
ATTN_LOCKSTEP = 4
LOG2E = 1.4426950408889634

_NT = (((1,), (1,)), ((), ()))
_TN = (((0,), (0,)), ((), ()))


def _dot(a, b):
    return jnp.dot(a, b, preferred_element_type=F32)


def _dot_nt(a, b):
    return lax.dot_general(a, b, _NT, preferred_element_type=F32)


def _dot_tn(a, b):
    return lax.dot_general(a, b, _TN, preferred_element_type=F32)


def _sigmoid(x):
    return 1.0 / (1.0 + jnp.exp2(x * -LOG2E))


def _rms_scale(x):
    return lax.rsqrt(jnp.mean(x * x, axis=-1, keepdims=True) + EPS)


def _run_interleaved(*gens):
    live = list(gens)
    while live:
        for g in list(live):
            try:
                next(g)
            except StopIteration:
                live.remove(g)


def _chain(gens):
    for g in gens:
        yield from g


def _attn_kernel(*refs, ns, tls, chunk, use_cache):
    if use_cache:
        (x_ref, g_ref, win_ref, wout_ref, bias_ref, ck_ref, cv_ref,
         x1_ref, kout_ref, vout_ref,
         q_s, gate_s, klo_s, khi_s, v_s, og_s) = refs
    else:
        (x_ref, g_ref, win_ref, wout_ref, bias_ref,
         x1_ref, kout_ref, vout_ref,
         q_s, gate_s, klo_s, khi_s, v_s, og_s) = refs
    t = pl.program_id(1)
    rows = ns * tls
    keys = WINDOW + chunk
    kvn = N_KV_HEADS * HEAD_DIM
    kvw = 2 * kvn
    qw = N_HEADS * HEAD_DIM
    gw = qw // N_KV_HEADS

    half_lo = lax.broadcasted_iota(jnp.int32, (1, LANES), 1) < HEAD_DIM

    x = x_ref[...].reshape(rows, D_MODEL)
    h = (x * _rms_scale(x) * g_ref[...]).astype(BF16)
    q_s[...] = _dot(h, win_ref[:, 0:qw]).astype(BF16)
    kd = _dot(h, win_ref[:, qw:qw + kvn])
    vd = _dot(h, win_ref[:, qw + kvn:qw + 2 * kvn])

    def gate_proj():
        c0 = qw + 2 * kvn
        for s0 in range(0, qw, MXU_DIM):
            gate_s[:, s0:s0 + MXU_DIM] = _dot(h, win_ref[:, c0 + s0:c0 + s0 + MXU_DIM])
            yield

    def spread(k2, v2, nrows, row0):
        for b in range(N_KV_HEADS // 2):
            kb = k2[:, b * LANES:(b + 1) * LANES]
            vb = v2[:, b * LANES:(b + 1) * LANES]
            kr = pltpu.roll(kb, HEAD_DIM, 1)
            vr = pltpu.roll(vb, HEAD_DIM, 1)
            forms = (
                (klo_s, jnp.where(half_lo, kb, 0.0), jnp.where(half_lo, kr, 0.0)),
                (khi_s, jnp.where(half_lo, 0.0, kr), jnp.where(half_lo, 0.0, kb)),
                (v_s, jnp.where(half_lo, vb, vr), jnp.where(half_lo, vr, vb)),
            )
            for ref, even, odd in forms:
                for j, val in ((2 * b, even), (2 * b + 1, odd)):
                    ref[:, row0:row0 + nrows, j * LANES:(j + 1) * LANES] = (
                        val.astype(BF16).reshape(ns, nrows, LANES))
                yield

    if use_cache:
        _run_interleaved(spread(ck_ref[...].reshape(ns * WINDOW, kvn), cv_ref[...].reshape(ns * WINDOW, kvn),
                                WINDOW, 0))
    else:
        @pl.when(t == 0)
        def _():
            zeros = jnp.zeros((ns, WINDOW, kvw), BF16)
            klo_s[:, 0:WINDOW, :] = zeros
            khi_s[:, 0:WINDOW, :] = zeros
            v_s[:, 0:WINDOW, :] = zeros

    _run_interleaved(spread(kd, vd, tls, WINDOW), gate_proj())

    kd3 = kd.reshape(ns, tls, kvn)
    vd3 = vd.reshape(ns, tls, kvn)
    if use_cache:
        kout_ref[:, 0:WINDOW - tls, :] = ck_ref[:, tls:, :]
        vout_ref[:, 0:WINDOW - tls, :] = cv_ref[:, tls:, :]
        kout_ref[:, WINDOW - tls:, :] = kd3
        vout_ref[:, WINDOW - tls:, :] = vd3
    else:
        @pl.when(t == pl.num_programs(1) - 1)
        def _():
            kout_ref[...] = kd3[:, tls - WINDOW:, :]
            vout_ref[...] = vd3[:, tls - WINDOW:, :]

    pad = MXU_DIM - keys
    col = lax.broadcasted_iota(jnp.int32, (chunk, MXU_DIM), 1)
    pair_lo = lax.broadcasted_iota(jnp.int32, (chunk, LANES), 1) < HEAD_DIM
    ones_blk = jnp.ones((keys, LANES), BF16)
    k_pad = jnp.zeros((pad, LANES), BF16)
    v_pad = jnp.concatenate([jnp.zeros((pad, LANES), BF16), jnp.ones((pad, LANES), BF16)], axis=1)
    n_chunks = tls // chunk

    def attend(s, c):
        r0 = c * chunk
        f0 = s * tls + r0
        masked = (not use_cache) and r0 < WINDOW
        if masked:
            n_inv = jnp.maximum(0, WINDOW - (t * tls + r0))
            valid = col >= n_inv
        scores = []
        for j in range(N_KV_HEADS):
            qj = q_s[f0:f0 + chunk, j * gw:(j + 1) * gw]
            lhs = jnp.concatenate([qj[:, :LANES], qj[:, LANES:]], axis=0)
            klo = jnp.concatenate([klo_s[s, r0:r0 + keys, j * LANES:(j + 1) * LANES], k_pad], axis=0)
            khi = jnp.concatenate([khi_s[s, r0:r0 + keys, j * LANES:(j + 1) * LANES], k_pad], axis=0)
            s_even = _dot_nt(lhs, klo)
            s_odd = _dot_nt(lhs, khi)
            scores.append((s_even[:chunk], s_odd[:chunk], s_even[chunk:], s_odd[chunk:]))
        yield
        ovs = []
        for j in range(N_KV_HEADS):
            ps = []
            for g in range(4):
                sg = scores[j][g] + bias_ref[4 * j + g]
                if masked:
                    sg = jnp.where(valid, sg, -jnp.inf)
                m = jnp.max(sg, axis=-1, keepdims=True)
                ps.append(jnp.exp2(sg - m).astype(BF16))
            p_all = jnp.concatenate(ps, axis=0)
            vj = v_s[s, r0:r0 + keys, j * LANES:(j + 1) * LANES]
            vaug = jnp.concatenate([jnp.concatenate([vj, ones_blk], axis=1), v_pad], axis=0)
            ovs.append(_dot(p_all, vaug))
        yield
        for j in range(N_KV_HEADS):
            outs = []
            for g in range(4):
                blk = ovs[j][g * chunk:(g + 1) * chunk]
                outs.append(blk[:, :LANES] / blk[:, LANES:])
            o01 = jnp.where(pair_lo, outs[0], outs[1])
            o23 = jnp.where(pair_lo, outs[2], outs[3])
            gt = gate_s[f0:f0 + chunk, j * gw:(j + 1) * gw]
            og = jnp.concatenate([o01, o23], axis=1) * (gt * _sigmoid(gt))
            og_s[f0:f0 + chunk, j * gw:(j + 1) * gw] = og.astype(BF16)
        yield

    units = [(s, c) for s in range(ns) for c in range(n_chunks)]
    for i in range(0, len(units), ATTN_LOCKSTEP):
        _run_interleaved(*[attend(s, c) for s, c in units[i:i + ATTN_LOCKSTEP]])

    if not use_cache:
        klo_s[:, 0:WINDOW, :] = klo_s[:, tls:tls + WINDOW, :]
        khi_s[:, 0:WINDOW, :] = khi_s[:, tls:tls + WINDOW, :]
        v_s[:, 0:WINDOW, :] = v_s[:, tls:tls + WINDOW, :]

    y = _dot(og_s[...], wout_ref[...])
    x1_ref[...] = (x + y).reshape(ns, tls, D_MODEL)


def _attn_bias(chunk, sinks):
    keys = WINDOW + chunk
    slopes = (2.0 ** (-8.0 * np.arange(1, N_HEADS + 1, dtype=np.float32) / N_HEADS)).astype(np.float32)
    i = np.arange(chunk, dtype=np.float32)[:, None]
    j = np.arange(keys, dtype=np.float32)[None, :]
    dist = np.abs(i + WINDOW - j).astype(np.float32)
    alibi = jnp.asarray(-(slopes[:, None, None] * dist[None]), F32) * LOG2E
    sink_col = jnp.broadcast_to((sinks.astype(F32) * LOG2E)[:, None, None], (N_HEADS, chunk, 1))
    rest = jnp.full((N_HEADS, chunk, MXU_DIM - keys - 1), -jnp.inf, F32)
    return jnp.concatenate([alibi, sink_col, rest], axis=2)


def _attn_layer(x, g, w_in_prep, w_out, sinks, *, ns, tls, chunk, cache=None):
    nb, seq, _ = x.shape
    use_cache = cache is not None
    grid = (nb // ns, seq // tls)
    kvn = N_KV_HEADS * HEAD_DIM
    kvw = 2 * kvn
    rows = ns * tls
    const2 = lambda b, t: (0, 0)
    in_specs = [
        pl.BlockSpec((ns, tls, D_MODEL), lambda b, t: (b, t, 0)),
        pl.BlockSpec((1, D_MODEL), const2),
        pl.BlockSpec(w_in_prep.shape, const2, pipeline_mode=pl.Buffered(1)),
        pl.BlockSpec(w_out.shape, const2, pipeline_mode=pl.Buffered(1)),
        pl.BlockSpec((N_HEADS, chunk, MXU_DIM), lambda b, t: (0, 0, 0), pipeline_mode=pl.Buffered(1)),
    ]
    args = [x, g, w_in_prep, w_out, _attn_bias(chunk, sinks)]
    if use_cache:
        in_specs += [pl.BlockSpec((ns, WINDOW, kvn), lambda b, t: (b, 0, 0))] * 2
        args += list(cache)
    out_shape = (
        jax.ShapeDtypeStruct((nb, seq, D_MODEL), F32),
        jax.ShapeDtypeStruct((nb, WINDOW, kvn), F32),
        jax.ShapeDtypeStruct((nb, WINDOW, kvn), F32),
    )
    out_specs = (
        pl.BlockSpec((ns, tls, D_MODEL), lambda b, t: (b, t, 0)),
        pl.BlockSpec((ns, WINDOW, kvn), lambda b, t: (b, 0, 0)),
        pl.BlockSpec((ns, WINDOW, kvn), lambda b, t: (b, 0, 0)),
    )
    scratch = [
        pltpu.VMEM((rows, D_MODEL), BF16),
        pltpu.VMEM((rows, D_MODEL), F32),
        pltpu.VMEM((ns, WINDOW + tls, kvw), BF16),
        pltpu.VMEM((ns, WINDOW + tls, kvw), BF16),
        pltpu.VMEM((ns, WINDOW + tls, kvw), BF16),
        pltpu.VMEM((rows, D_MODEL), BF16),
    ]
    return pl.pallas_call(
        functools.partial(_attn_kernel, ns=ns, tls=tls, chunk=chunk, use_cache=use_cache),
        grid=grid, in_specs=in_specs, out_specs=out_specs, out_shape=out_shape,
        scratch_shapes=scratch,
        compiler_params=pltpu.CompilerParams(
            dimension_semantics=("arbitrary", "arbitrary"),
            vmem_limit_bytes=VMEM_LIMIT_BYTES),
        name="attn_cache" if use_cache else "attn_prompt",
    )(*args)


def _split3(x):
    hi = x.astype(BF16)
    r1 = x - hi.astype(F32)
    mid = r1.astype(BF16)
    lo = (r1 - mid.astype(F32)).astype(BF16)
    return hi, mid, lo


def _dn_kernel(*refs, ns, tls, chunk, use_state):
    rm = not use_state
    if use_state:
        (x_ref, g_ref, w1_ref, w2_ref, cw_ref, alog_ref, dtb_ref, ng_ref, wout_ref, fg_ref,
         tri_ref, cst_ref, sst_ref,
         y_ref, cout_ref, s_ref,
         cbuf, qn_s, kn_s, v_s, z_s, bt_s, gc_s, o_s, on_s,
         qg_s, kd_s, u_s, w_s, intra_s, sb_s) = refs
    else:
        (x_ref, g_ref, w1_ref, w2_ref, cw_ref, alog_ref, dtb_ref, ng_ref, wout_ref, fg_ref,
         tri_ref,
         y_ref, cout_ref, s_ref,
         cbuf, qn_s, kn_s, v_s, z_s, bt_s, gc_s, o_s, on_s,
         qg_s, kd_s, u_s, w_s, intra_s, sb_s, hist_s, xs_s, ys_s) = refs
        assert ns == 1 and chunk == SUBLANES * SUBLANES
    t = pl.program_id(1)
    rows = ns * tls
    hist = CONV_WIDTH - 1
    qk_w = DN_HEADS * DN_DIM
    conv_w = 3 * qk_w
    n_pairs = DN_HEADS // 2
    rres = tls // SUBLANES

    if use_state:
        s_ref[...] = sst_ref[...]
        cbuf[:, CONV_PAD - hist:CONV_PAD, :] = cst_ref[...]
        sb_s[...] = jnp.zeros(sb_s.shape, BF16)
        for s in range(ns):
            for hd in range(DN_HEADS):
                d0 = (hd % 2) * DN_DIM
                sb_s[s, hd // 2, d0:d0 + DN_DIM, d0:d0 + DN_DIM] = sst_ref[s, hd].astype(BF16)
    else:
        @pl.when(t == 0)
        def _():
            s_ref[...] = jnp.zeros(s_ref.shape, F32)
            sb_s[...] = jnp.zeros(sb_s.shape, BF16)
            hist_s[...] = jnp.zeros(hist_s.shape, F32)

    n_slabs = D_MODEL // LANES
    if rm:
        for j in range(n_slabs):
            xs_s[j * tls:(j + 1) * tls, :] = x_ref[0, :, j * LANES:(j + 1) * LANES]
        x = jnp.concatenate(
            [jnp.concatenate([xs_s[pl.ds(j * tls + r, rres, stride=SUBLANES), :] for j in range(n_slabs)], axis=1)
             for r in range(SUBLANES)], axis=0)
    else:
        x = x_ref[...].reshape(rows, D_MODEL)
    h = (x * _rms_scale(x) * g_ref[...]).astype(BF16)

    def phase_a():
        ba = _dot(h, w2_ref[...])
        bt_s[...] = _sigmoid(ba[:, :LANES])
        a = ba[:, LANES:] + dtb_ref[...]
        softplus = jnp.maximum(a, 0.0) + jnp.log1p(jnp.exp(-jnp.abs(a)))
        g = -jnp.exp(alog_ref[...]) * softplus
        g_hi, g_mid, g_lo = _split3(g)
        gcs = _dot(tri_ref[...], jnp.concatenate([g_hi, g_mid, g_lo], axis=1))
        gc_s[...] = gcs[:, :LANES] + gcs[:, LANES:2 * LANES] + gcs[:, 2 * LANES:]
        yield
        for c0 in range(0, conv_w, MXU_DIM):
            raw = _dot(h, w1_ref[:, c0:c0 + MXU_DIM])
            if rm:
                cbuf[:, c0:c0 + MXU_DIM] = raw
            else:
                cbuf[:, CONV_PAD:, c0:c0 + MXU_DIM] = raw.reshape(ns, tls, MXU_DIM)
            yield
        for c0 in range(0, qk_w, MXU_DIM):
            z_s[:, c0:c0 + MXU_DIM] = _dot(h, w1_ref[:, conv_w + c0:conv_w + c0 + MXU_DIM])
            yield

    def conv_block(s, cols):
        if not rm:
            acc = None
            for j in range(CONV_WIDTH):
                lo = CONV_PAD - hist + j
                term = cbuf[s, lo:lo + tls, cols] * cw_ref[j:j + 1, cols]
                acc = term if acc is None else acc + term
            return acc
        xb = cbuf[:, cols]
        wrapped = []
        for k in range(hist):
            r = SUBLANES - hist + k
            full = jnp.concatenate([hist_s[k, :, cols], xb[r * rres:(r + 1) * rres]], axis=0)
            wrapped.append(full[SUBLANES - 1:SUBLANES - 1 + rres])
        ext = jnp.concatenate(wrapped + [xb], axis=0)
        acc = None
        for j in range(CONV_WIDTH):
            term = ext[j * rres:j * rres + tls] * cw_ref[j:j + 1, cols]
            acc = term if acc is None else acc + term
        return acc

    def phase_b():
        for blk in range(conv_w // LANES):
            cols = slice(blk * LANES, (blk + 1) * LANES)
            seg, hb = divmod(blk, DN_HEADS)
            dst = slice(hb * LANES, (hb + 1) * LANES)
            for s in range(ns):
                acc = conv_block(s, cols)
                yv = acc * _sigmoid(acc)
                rs = slice(s * tls, (s + 1) * tls)
                if seg == 2:
                    v_s[rs, dst] = yv
                else:
                    inv = lax.rsqrt(jnp.sum(yv * yv, axis=-1, keepdims=True) + EPS)
                    if seg == 0:
                        qn_s[rs, dst] = yv * (inv * (DN_DIM ** -0.5))
                    else:
                        kn_s[rs, dst] = yv * inv
            if blk % 2 == 1:
                yield

    ga, gb = phase_a(), phase_b()
    for _ in range(1 + qk_w // MXU_DIM):
        next(ga)
    _run_interleaved(gb, ga)

    if rm:
        for k in range(hist):
            r = SUBLANES - hist + k
            last = cbuf[(r + 1) * rres - SUBLANES:(r + 1) * rres, :]
            hist_s[k] = last
            cout_ref[0, k:k + 1, :] = last[SUBLANES - 1:SUBLANES]
    else:
        cout_ref[...] = cbuf[:, CONV_PAD + tls - hist:CONV_PAD + tls, :]

    hg = MXU_DIM // chunk
    n_groups = DN_HEADS // hg
    n_levels = int(round(math.log2(chunk)))
    ri = lax.broadcasted_iota(jnp.int32, (MXU_DIM, MXU_DIM), 0)
    ci = lax.broadcasted_iota(jnp.int32, (MXU_DIM, MXU_DIM), 1)

    def time_of(pos):
        p = pos % chunk
        return SUBLANES * (p % SUBLANES) + p // SUBLANES if rm else p

    tri_mask = (ri // chunk == ci // chunk) & (time_of(ri) >= time_of(ci))
    diag = ri == ci
    eye = jnp.where(diag, 1.0, 0.0).astype(F32)
    lane = lax.broadcasted_iota(jnp.int32, (MXU_DIM, LANES), 1)
    ones_rows = jnp.ones((2 * SUBLANES, LANES), BF16)
    n_chunks = tls // chunk

    def hcols(hd):
        return slice(hd * LANES, (hd + 1) * LANES)

    def crows(ref, s, c, cols):
        if rm:
            return jnp.concatenate(
                [ref[r * rres + SUBLANES * c:r * rres + SUBLANES * (c + 1), cols] for r in range(SUBLANES)], axis=0)
        f0 = s * tls + c * chunk
        return ref[f0:f0 + chunk, cols]

    def crow_last(ref, s, c, cols):
        r = (SUBLANES - 1) * rres + SUBLANES * c + SUBLANES - 1 if rm else s * tls + (c + 1) * chunk - 1
        return ref[r:r + 1, cols]

    def phase_c(groups):
        st = []
        for (s, c, grp) in groups:
            f0 = s * tls + c * chunk
            rsl = slice(f0, f0 + chunk)
            heads = list(range(grp * hg, (grp + 1) * hg))
            k_h, q_h, rhs_rows, gc_cols, beta_cols = [], [], [], [], []
            for hd in heads:
                beta_h = crows(bt_s, s, c, slice(hd, hd + 1))
                gc_h = crows(gc_s, s, c, slice(hd, hd + 1))
                gl_h = crow_last(gc_s, s, c, slice(hd, hd + 1))
                eg = jnp.exp(gc_h)
                kf = crows(kn_s, s, c, hcols(hd))
                qf = crows(qn_s, s, c, hcols(hd))
                vb = (crows(v_s, s, c, hcols(hd)) * beta_h).astype(BF16)
                kbg = (kf * (beta_h * eg)).astype(BF16)
                qg_s[rsl, hcols(hd)] = (qf * eg).astype(BF16)
                kd_s[rsl, hcols(hd)] = (kf * jnp.exp(gl_h - gc_h)).astype(BF16)
                k_h.append(kf.astype(BF16))
                q_h.append(qf.astype(BF16))
                rhs_rows.append(jnp.concatenate([vb, kbg], axis=1))
                gc_cols.append(gc_h)
                beta_cols.append(beta_h)
            gc_c = jnp.concatenate(gc_cols, axis=0)
            hi, mid, lo = (p.astype(F32) for p in _split3(gc_c))
            rmat = jnp.where(lane == 0, hi, jnp.where(lane == 1, mid, jnp.where(lane == 2, lo, 0.0)))
            gc_row = _dot_nt(ones_rows, rmat.astype(BF16))[0:1]
            k_st = jnp.concatenate(k_h, axis=0)
            q_st = jnp.concatenate(q_h, axis=0)
            st.append(dict(
                rsl=rsl, heads=heads, gi=(s * n_chunks + c) * n_groups + grp,
                rhs=jnp.concatenate(rhs_rows, axis=0),
                beta=jnp.concatenate(beta_cols, axis=0),
                gd=gc_c - gc_row,
                kq=_dot_nt(jnp.concatenate([k_st, q_st], axis=0), k_st)))
        yield
        for d in st:
            decay = jnp.exp(jnp.where(tri_mask, d.pop('gd'), -jnp.inf))
            kq = d.pop('kq')
            m = jnp.where(diag, 0.0, kq[:MXU_DIM] * decay * d.pop('beta'))
            intra_s[d['gi']] = (kq[MXU_DIM:] * decay).astype(BF16)
            mb = m.astype(BF16)
            d['p'] = eye - m
            d['q'] = _dot(mb, mb).astype(BF16)
        yield
        for lvl in range(1, n_levels):
            for d in st:
                qb = d['q']
                if lvl < n_levels - 1:
                    pq = _dot(jnp.concatenate([d['p'].astype(BF16), qb], axis=0), qb)
                    d['p'] = d['p'] + pq[:MXU_DIM]
                    d['q'] = pq[MXU_DIM:].astype(BF16)
                else:
                    d['p'] = d['p'] + _dot(d['p'].astype(BF16), qb)
            yield
        for d in st:
            uw = _dot(d['p'].astype(BF16), d['rhs'])
            for i, hd in enumerate(d['heads']):
                hs = slice(i * chunk, (i + 1) * chunk)
                u_s[d['rsl'], hcols(hd)] = uw[hs, :LANES]
                w_s[d['rsl'], hcols(hd)] = uw[hs, LANES:].astype(BF16)
        yield

    def phase_d(s, c):
        f0 = s * tls + c * chunk
        rsl = slice(f0, f0 + chunk)
        both = []
        for p in range(n_pairs):
            pc = slice(p * MXU_DIM, (p + 1) * MXU_DIM)
            lhs = jnp.concatenate([w_s[rsl, pc], qg_s[rsl, pc]], axis=0)
            both.append(_dot(lhs, sb_s[s, p]))
        yield
        vnb, outer = [], []
        for p in range(n_pairs):
            pc = slice(p * MXU_DIM, (p + 1) * MXU_DIM)
            v_new = (u_s[rsl, pc] - both[p][:chunk]).astype(BF16)
            for i in range(2):
                hd = 2 * p + i
                vh = v_new[:, i * LANES:(i + 1) * LANES]
                vnb.append(vh)
                outer.append(_dot_tn(kd_s[rsl, hcols(hd)], vh))
        for grp in range(n_groups):
            heads = list(range(grp * hg, (grp + 1) * hg))
            v_st = jnp.concatenate([vnb[hd] for hd in heads], axis=0)
            qs_st = jnp.concatenate(
                [both[hd // 2][chunk:, (hd % 2) * LANES:(hd % 2 + 1) * LANES] for hd in heads], axis=0)
            o_st = qs_st + _dot(intra_s[(s * n_chunks + c) * n_groups + grp], v_st)
            for i, hd in enumerate(heads):
                o_s[rsl, hcols(hd)] = o_st[i * chunk:(i + 1) * chunk]
        yield
        for hd in range(DN_HEADS):
            gl = crow_last(gc_s, s, c, slice(hd, hd + 1))
            s_new = s_ref[s, hd] * jnp.exp(gl) + outer[hd]
            s_ref[s, hd] = s_new
            d0 = (hd % 2) * DN_DIM
            sb_s[s, hd // 2, d0:d0 + DN_DIM, d0:d0 + DN_DIM] = s_new.astype(BF16)
        yield

    def phase_e(r0, r1):
        rsl = slice(r0, r1)
        for hd in range(DN_HEADS):
            o = o_s[rsl, hcols(hd)]
            if rm:
                zz = jnp.concatenate([crows(z_s, 0, c, hcols(hd)) for c in range(r0 // chunk, r1 // chunk)], axis=0)
            else:
                zz = z_s[rsl, hcols(hd)]
            on_s[rsl, hcols(hd)] = (o * _rms_scale(o) * ng_ref[...] * (zz * _sigmoid(zz))).astype(BF16)
            yield
        y1 = _dot(on_s[rsl, :], wout_ref[...])
        if rm:
            for i, c in enumerate(range(r0 // chunk, r1 // chunk)):
                for r in range(SUBLANES):
                    p0 = i * chunk + r * SUBLANES
                    for j in range(n_slabs):
                        ys_s[pl.ds(j * tls + c * chunk + r, SUBLANES, stride=SUBLANES), :] = (
                            y1[p0:p0 + SUBLANES, j * LANES:(j + 1) * LANES])
            y1 = jnp.concatenate([ys_s[j * tls + r0:j * tls + r1, :] for j in range(n_slabs)], axis=1)
            x2 = x_ref[0, r0:r1, :] + y1
            y_ref[0, r0:r1, :] = x2 * _rms_scale(x2) * fg_ref[...]
        else:
            x2 = x_ref[r0 // tls:r1 // tls].reshape(r1 - r0, D_MODEL) + y1
            yv = x2 * _rms_scale(x2) * fg_ref[...]
            y_ref[r0 // tls:r1 // tls] = yv.reshape((r1 - r0) // tls, tls, D_MODEL)
        yield

    units = [(s, c) for s in range(ns) for c in range(n_chunks)]
    upb = max(1, 4 // n_groups)
    batches = [units[i:i + upb] for i in range(0, len(units), upb)]
    brows = upb * chunk

    def c_of(batch):
        return phase_c([(s, c, grp) for (s, c) in batch for grp in range(n_groups)])

    def d_of(batch):
        if use_state:
            return [phase_d(s, c) for (s, c) in batch]
        return [_chain([phase_d(s, c) for (s, c) in batch])]

    _run_interleaved(c_of(batches[0]), ga)
    for i in range(1, len(batches)):
        extra = [phase_e((i - 2) * brows, (i - 1) * brows)] if i >= 2 else []
        _run_interleaved(c_of(batches[i]), *d_of(batches[i - 1]), *extra)
    last = len(batches) - 1
    extra = [phase_e((last - 1) * brows, last * brows)] if last >= 1 else []
    _run_interleaved(*d_of(batches[last]), *extra)
    _run_interleaved(phase_e(last * brows, (last + 1) * brows))


def _block_tri(rows, chunk, residue_major):
    q = np.arange(rows)
    tm = SUBLANES * (q % (rows // SUBLANES)) + q // (rows // SUBLANES) if residue_major else q
    m = (tm[:, None] // chunk == tm[None, :] // chunk) & (tm[:, None] >= tm[None, :])
    return jnp.asarray(m.astype(np.float32), BF16)


def _dn_layer(x, g, w1, w2, conv_w, alog, dtb, ng, w_out, fg, *, ns, tls, chunk, state=None):
    nb, seq, _ = x.shape
    use_state = state is not None
    grid = (nb // ns, seq // tls)
    rows = ns * tls
    hist = CONV_WIDTH - 1
    conv_dim = 3 * DN_HEADS * DN_DIM
    n_stacks = rows * DN_HEADS // MXU_DIM
    const2 = lambda b, t: (0, 0)
    tri = _block_tri(rows, chunk, not use_state)
    args = [x, g, w1, w2, conv_w, alog, dtb, ng, w_out, fg, tri]
    x_spec = pl.BlockSpec((ns, tls, D_MODEL), lambda b, t: (b, t, 0))
    in_specs = [x_spec]
    in_specs += [pl.BlockSpec(a.shape, const2, pipeline_mode=pl.Buffered(1)) for a in args[1:]]
    if use_state:
        in_specs += [
            pl.BlockSpec((None, ns, hist, conv_dim), lambda b, t: (0, b, 0, 0)),
            pl.BlockSpec((None, ns, DN_HEADS, DN_DIM, DN_DIM), lambda b, t: (0, b, 0, 0, 0)),
        ]
        args += list(state)
    out_shape = (
        jax.ShapeDtypeStruct((nb, seq, D_MODEL), F32),
        jax.ShapeDtypeStruct((nb, hist, conv_dim), F32),
        jax.ShapeDtypeStruct((nb, DN_HEADS, DN_DIM, DN_DIM), F32),
    )
    out_specs = (
        x_spec,
        pl.BlockSpec((ns, hist, conv_dim), lambda b, t: (b, 0, 0)),
        pl.BlockSpec((ns, DN_HEADS, DN_DIM, DN_DIM), lambda b, t: (b, 0, 0, 0)),
    )
    if use_state:
        cbuf = pltpu.VMEM((ns, CONV_PAD + tls, conv_dim), F32)
    else:
        cbuf = pltpu.VMEM((tls, conv_dim), F32)
    scratch = [
        cbuf,
        pltpu.VMEM((rows, D_MODEL), F32),
        pltpu.VMEM((rows, D_MODEL), F32),
        pltpu.VMEM((rows, D_MODEL), F32),
        pltpu.VMEM((rows, D_MODEL), F32),
        pltpu.VMEM((rows, LANES), F32),
        pltpu.VMEM((rows, LANES), F32),
        pltpu.VMEM((rows, D_MODEL), F32),
        pltpu.VMEM((rows, D_MODEL), BF16),
        pltpu.VMEM((rows, D_MODEL), BF16),
        pltpu.VMEM((rows, D_MODEL), BF16),
        pltpu.VMEM((rows, D_MODEL), F32),
        pltpu.VMEM((rows, D_MODEL), BF16),
        pltpu.VMEM((n_stacks, MXU_DIM, MXU_DIM), BF16),
        pltpu.VMEM((ns, DN_HEADS // 2, MXU_DIM, MXU_DIM), BF16),
    ]
    if not use_state:
        scratch.append(pltpu.VMEM((hist, SUBLANES, conv_dim), F32))
        scratch.append(pltpu.VMEM((D_MODEL // LANES * tls, LANES), F32))
        scratch.append(pltpu.VMEM((D_MODEL // LANES * tls, LANES), F32))
    return pl.pallas_call(
        functools.partial(_dn_kernel, ns=ns, tls=tls, chunk=chunk, use_state=use_state),
        grid=grid, in_specs=in_specs, out_specs=out_specs, out_shape=out_shape,
        scratch_shapes=scratch,
        compiler_params=pltpu.CompilerParams(
            dimension_semantics=("arbitrary", "arbitrary"),
            vmem_limit_bytes=VMEM_LIMIT_BYTES),
        name="dn_state" if use_state else "dn_prompt",
    )(*args)


def _pad_lanes(v):
    return jnp.pad(v.astype(F32)[None, :], ((0, 0), (0, LANES - v.shape[0])))


def kernel(x_prompt, x_sample, cache_k, cache_v, state_conv, state_ssm, norm_g, final_norm_g,
           attn_w_in, attn_sinks, attn_w_out, dn_w_in, dn_conv_w, dn_a_log, dn_dt_bias,
           dn_norm_g, dn_w_out):
    attn_width = N_HEADS * HEAD_DIM
    kv_width = N_KV_HEADS * HEAD_DIM
    col_scale = jnp.where(jnp.arange(attn_w_in.shape[-1]) < attn_width, HEAD_DIM ** -0.5 * LOG2E, 1.0).astype(F32)
    a_win = (attn_w_in[0] * col_scale).astype(BF16)
    a_wout = attn_w_out[0].astype(BF16)
    g0 = norm_g[0].reshape(1, D_MODEL)
    sinks = attn_sinks[0]

    wide = 3 * DN_HEADS * DN_DIM + DN_HEADS * DN_DIM
    dw = dn_w_in[0]
    d_w1 = dw.astype(BF16)
    lane_pad = ((0, 0), (0, LANES - DN_HEADS))
    d_w2 = jnp.concatenate([jnp.pad(dw[:, wide:wide + DN_HEADS], lane_pad),
                            jnp.pad(dw[:, wide + DN_HEADS:], lane_pad)], axis=1).astype(BF16)
    d_wout = dn_w_out[0].astype(BF16)
    g1 = norm_g[1].reshape(1, D_MODEL)
    fg = final_norm_g.reshape(1, D_MODEL)
    alog = _pad_lanes(dn_a_log[0])
    dtb = _pad_lanes(dn_dt_bias[0])
    ng = dn_norm_g[0].reshape(1, DN_DIM)
    dn_args = (g1, d_w1, d_w2, dn_conv_w[0], alog, dtb, ng, d_wout, fg)

    x1p, kp, vp = _attn_layer(x_prompt, g0, a_win, a_wout, sinks, ns=1, tls=ATTN_TILE_ROWS, chunk=CHUNK)
    yp, cp, sp = _dn_layer(x1p, *dn_args, ns=1, tls=DN_TILE_ROWS, chunk=CHUNK)

    n_run, dec_seq = x_sample.shape[:2]
    ck = cache_k[0].reshape(n_run, WINDOW, kv_width)
    cv = cache_v[0].reshape(n_run, WINDOW, kv_width)
    x1s, ks, vs = _attn_layer(x_sample, g0, a_win, a_wout, sinks, ns=RUN_SEQS_PER_STEP, tls=dec_seq,
                              chunk=min(CHUNK, dec_seq), cache=(ck, cv))
    ys, cs, ss = _dn_layer(x1s, *dn_args, ns=RUN_SEQS_PER_STEP, tls=dec_seq, chunk=min(CHUNK, dec_seq),
                           state=(state_conv, state_ssm))

    def kv_out(t):
        return t.reshape(1, t.shape[0], WINDOW, N_KV_HEADS, HEAD_DIM)

    return (yp, ys, kv_out(kp), kv_out(vp), cp[None], sp[None],
            kv_out(ks), kv_out(vs), cs[None], ss[None])
```

```python
import functools
import math

import numpy as np
import jax
import jax.numpy as jnp
from jax import lax
from jax.experimental import pallas as pl
from jax.experimental.pallas import tpu as pltpu

F32 = jnp.float32
BF16 = jnp.bfloat16

D_MODEL = 1024
EPS = 1e-6
CHUNK = 64
WINDOW = 128
N_HEADS = 16
N_KV_HEADS = 4
HEAD_DIM = 64
DN_HEADS = 8
DN_DIM = 128
CONV_WIDTH = 4
CONV_PAD = 8
SUBLANES = 8
LANES = 128
MXU_DIM = 256
VMEM_LIMIT_BYTES = 56 * 1024 * 1024
ATTN_TILE_ROWS = 1024
DN_TILE_ROWS = 512
RUN_SEQS_PER_STEP = 8
CONV_SEG_ORDER = (2, 1, 0)
ATTN_LOCKSTEP = 4
LOG2E = 1.4426950408889634

_NT = (((1,), (1,)), ((), ()))
_TN = (((0,), (0,)), ((), ()))


def _dot(a, b):
    return jnp.dot(a, b, preferred_element_type=F32)


def _dot_nt(a, b):
    return lax.dot_general(a, b, _NT, preferred_element_type=F32)


def _dot_tn(a, b):
    return lax.dot_general(a, b, _TN, preferred_element_type=F32)


def _sigmoid(x):
    return 1.0 / (1.0 + jnp.exp2(x * -LOG2E))


def _rms_scale(x):
    return lax.rsqrt(jnp.mean(x * x, axis=-1, keepdims=True) + EPS)


def _run_interleaved(*gens):
    live = list(gens)
    while live:
        for g in list(live):
            try:
                next(g)
            except StopIteration:
                live.remove(g)


def _chain(gens):
    for g in gens:
        yield from g


def _attn_kernel(*refs, ns, tls, chunk, use_cache):
    if use_cache:
        (x_ref, g_ref, win_ref, wout_ref, bias_ref, ck_ref, cv_ref,
         x1_ref, kout_ref, vout_ref,
         q_s, gate_s, klo_s, khi_s, v_s, og_s) = refs
    else:
        (x_ref, g_ref, win_ref, wout_ref, bias_ref,
         x1_ref, kout_ref, vout_ref,
         q_s, gate_s, klo_s, khi_s, v_s, og_s) = refs
    t = pl.program_id(1)
    rows = ns * tls
    keys = WINDOW + chunk
    kvn = N_KV_HEADS * HEAD_DIM
    kvw = 2 * kvn
    qw = N_HEADS * HEAD_DIM
    gw = qw // N_KV_HEADS

    half_lo = lax.broadcasted_iota(jnp.int32, (1, LANES), 1) < HEAD_DIM

    x = x_ref[...].reshape(rows, D_MODEL)
    h = (x * _rms_scale(x) * g_ref[...]).astype(BF16)
    q_s[...] = _dot(h, win_ref[:, 0:qw]).astype(BF16)
    kd = _dot(h, win_ref[:, qw:qw + kvn])
    vd = _dot(h, win_ref[:, qw + kvn:qw + 2 * kvn])

    def gate_proj():
        c0 = qw + 2 * kvn
        for s0 in range(0, qw, MXU_DIM):
            gate_s[:, s0:s0 + MXU_DIM] = _dot(h, win_ref[:, c0 + s0:c0 + s0 + MXU_DIM])
            yield

    def spread(k2, v2, nrows, row0):
        for b in range(N_KV_HEADS // 2):
            kb = k2[:, b * LANES:(b + 1) * LANES]
            vb = v2[:, b * LANES:(b + 1) * LANES]
            kr = pltpu.roll(kb, HEAD_DIM, 1)
            vr = pltpu.roll(vb, HEAD_DIM, 1)
            forms = (
                (klo_s, jnp.where(half_lo, kb, 0.0), jnp.where(half_lo, kr, 0.0)),
                (khi_s, jnp.where(half_lo, 0.0, kr), jnp.where(half_lo, 0.0, kb)),
                (v_s, jnp.where(half_lo, vb, vr), jnp.where(half_lo, vr, vb)),
            )
            for ref, even, odd in forms:
                for j, val in ((2 * b, even), (2 * b + 1, odd)):
                    ref[:, row0:row0 + nrows, j * LANES:(j + 1) * LANES] = (
                        val.astype(BF16).reshape(ns, nrows, LANES))
                yield

    if use_cache:
        _run_interleaved(spread(ck_ref[...].reshape(ns * WINDOW, kvn), cv_ref[...].reshape(ns * WINDOW, kvn),
                                WINDOW, 0))
    else:
        @pl.when(t == 0)
        def _():
            zeros = jnp.zeros((ns, WINDOW, kvw), BF16)
            klo_s[:, 0:WINDOW, :] = zeros
            khi_s[:, 0:WINDOW, :] = zeros
            v_s[:, 0:WINDOW, :] = zeros

    _run_interleaved(spread(kd, vd, tls, WINDOW), gate_proj())

    kd3 = kd.reshape(ns, tls, kvn)
    vd3 = vd.reshape(ns, tls, kvn)
    if use_cache:
        kout_ref[:, 0:WINDOW - tls, :] = ck_ref[:, tls:, :]
        vout_ref[:, 0:WINDOW - tls, :] = cv_ref[:, tls:, :]
        kout_ref[:, WINDOW - tls:, :] = kd3
        vout_ref[:, WINDOW - tls:, :] = vd3
    else:
        @pl.when(t == pl.num_programs(1) - 1)
        def _():
            kout_ref[...] = kd3[:, tls - WINDOW:, :]
            vout_ref[...] = vd3[:, tls - WINDOW:, :]

    pad = MXU_DIM - keys
    col = lax.broadcasted_iota(jnp.int32, (chunk, MXU_DIM), 1)
    pair_lo = lax.broadcasted_iota(jnp.int32, (chunk, LANES), 1) < HEAD_DIM
    ones_blk = jnp.ones((keys, LANES), BF16)
    k_pad = jnp.zeros((pad, LANES), BF16)
    v_pad = jnp.concatenate([jnp.zeros((pad, LANES), BF16), jnp.ones((pad, LANES), BF16)], axis=1)
    n_chunks = tls // chunk

    def attend(s, c):
        r0 = c * chunk
        f0 = s * tls + r0
        masked = (not use_cache) and r0 < WINDOW
        if masked:
            n_inv = jnp.maximum(0, WINDOW - (t * tls + r0))
            valid = col >= n_inv
        scores = []
        for j in range(N_KV_HEADS):
            qj = q_s[f0:f0 + chunk, j * gw:(j + 1) * gw]
            lhs = jnp.concatenate([qj[:, :LANES], qj[:, LANES:]], axis=0)
            klo = jnp.concatenate([klo_s[s, r0:r0 + keys, j * LANES:(j + 1) * LANES], k_pad], axis=0)
            khi = jnp.concatenate([khi_s[s, r0:r0 + keys, j * LANES:(j + 1) * LANES], k_pad], axis=0)
            s_even = _dot_nt(lhs, klo)
            s_odd = _dot_nt(lhs, khi)
            scores.append((s_even[:chunk], s_odd[:chunk], s_even[chunk:], s_odd[chunk:]))
        yield
        ovs = []
        for j in range(N_KV_HEADS):
            ps = []
            for g in range(4):
                sg = scores[j][g] + bias_ref[4 * j + g]
                if masked:
                    sg = jnp.where(valid, sg, -jnp.inf)
                m = jnp.max(sg, axis=-1, keepdims=True)
                ps.append(jnp.exp2(sg - m).astype(BF16))
            p_all = jnp.concatenate(ps, axis=0)
            vj = v_s[s, r0:r0 + keys, j * LANES:(j + 1) * LANES]
            vaug = jnp.concatenate([jnp.concatenate([vj, ones_blk], axis=1), v_pad], axis=0)
            ovs.append(_dot(p_all, vaug))
        yield
        for j in range(N_KV_HEADS):
            outs = []
            for g in range(4):
                blk = ovs[j][g * chunk:(g + 1) * chunk]
                outs.append(blk[:, :LANES] / blk[:, LANES:])
            o01 = jnp.where(pair_lo, outs[0], outs[1])
            o23 = jnp.where(pair_lo, outs[2], outs[3])
            gt = gate_s[f0:f0 + chunk, j * gw:(j + 1) * gw]
            og = jnp.concatenate([o01, o23], axis=1) * (gt * _sigmoid(gt))
            og_s[f0:f0 + chunk, j * gw:(j + 1) * gw] = og.astype(BF16)
        yield

    units = [(s, c) for s in range(ns) for c in range(n_chunks)]
    for i in range(0, len(units), ATTN_LOCKSTEP):
        _run_interleaved(*[attend(s, c) for s, c in units[i:i + ATTN_LOCKSTEP]])

    if not use_cache:
        klo_s[:, 0:WINDOW, :] = klo_s[:, tls:tls + WINDOW, :]
        khi_s[:, 0:WINDOW, :] = khi_s[:, tls:tls + WINDOW, :]
        v_s[:, 0:WINDOW, :] = v_s[:, tls:tls + WINDOW, :]

    y = _dot(og_s[...], wout_ref[...])
    x1_ref[...] = (x + y).reshape(ns, tls, D_MODEL)


def _attn_bias(chunk, sinks):
    keys = WINDOW + chunk
    slopes = (2.0 ** (-8.0 * np.arange(1, N_HEADS + 1, dtype=np.float32) / N_HEADS)).astype(np.float32)
    i = np.arange(chunk, dtype=np.float32)[:, None]
    j = np.arange(keys, dtype=np.float32)[None, :]
    dist = np.abs(i + WINDOW - j).astype(np.float32)
    alibi = jnp.asarray(-(slopes[:, None, None] * dist[None]), F32) * LOG2E
    sink_col = jnp.broadcast_to((sinks.astype(F32) * LOG2E)[:, None, None], (N_HEADS, chunk, 1))
    rest = jnp.full((N_HEADS, chunk, MXU_DIM - keys - 1), -jnp.inf, F32)
    return jnp.concatenate([alibi, sink_col, rest], axis=2)


def _attn_layer(x, g, w_in_prep, w_out, sinks, *, ns, tls, chunk, cache=None):
    nb, seq, _ = x.shape
    use_cache = cache is not None
    grid = (nb // ns, seq // tls)
    kvn = N_KV_HEADS * HEAD_DIM
    kvw = 2 * kvn
    rows = ns * tls
    const2 = lambda b, t: (0, 0)
    in_specs = [
        pl.BlockSpec((ns, tls, D_MODEL), lambda b, t: (b, t, 0)),
        pl.BlockSpec((1, D_MODEL), const2),
        pl.BlockSpec(w_in_prep.shape, const2, pipeline_mode=pl.Buffered(1)),
        pl.BlockSpec(w_out.shape, const2, pipeline_mode=pl.Buffered(1)),
        pl.BlockSpec((N_HEADS, chunk, MXU_DIM), lambda b, t: (0, 0, 0), pipeline_mode=pl.Buffered(1)),
    ]
    args = [x, g, w_in_prep, w_out, _attn_bias(chunk, sinks)]
    if use_cache:
        in_specs += [pl.BlockSpec((ns, WINDOW, kvn), lambda b, t: (b, 0, 0))] * 2
        args += list(cache)
    out_shape = (
        jax.ShapeDtypeStruct((nb, seq, D_MODEL), F32),
        jax.ShapeDtypeStruct((nb, WINDOW, kvn), F32),
        jax.ShapeDtypeStruct((nb, WINDOW, kvn), F32),
    )
    out_specs = (
        pl.BlockSpec((ns, tls, D_MODEL), lambda b, t: (b, t, 0)),
        pl.BlockSpec((ns, WINDOW, kvn), lambda b, t: (b, 0, 0)),
        pl.BlockSpec((ns, WINDOW, kvn), lambda b, t: (b, 0, 0)),
    )
    scratch = [
        pltpu.VMEM((rows, D_MODEL), BF16),
        pltpu.VMEM((rows, D_MODEL), F32),
        pltpu.VMEM((ns, WINDOW + tls, kvw), BF16),
        pltpu.VMEM((ns, WINDOW + tls, kvw), BF16),
        pltpu.VMEM((ns, WINDOW + tls, kvw), BF16),
        pltpu.VMEM((rows, D_MODEL), BF16),
    ]
    return pl.pallas_call(
        functools.partial(_attn_kernel, ns=ns, tls=tls, chunk=chunk, use_cache=use_cache),
        grid=grid, in_specs=in_specs, out_specs=out_specs, out_shape=out_shape,
        scratch_shapes=scratch,
        compiler_params=pltpu.CompilerParams(
            dimension_semantics=("arbitrary", "arbitrary"),
            vmem_limit_bytes=VMEM_LIMIT_BYTES),
        name="attn_cache" if use_cache else "attn_prompt",
    )(*args)


def _split3(x):
    hi = x.astype(BF16)
    r1 = x - hi.astype(F32)
    mid = r1.astype(BF16)
    lo = (r1 - mid.astype(F32)).astype(BF16)
    return hi, mid, lo


def _dn_kernel(*refs, ns, tls, chunk, use_state):
    rm = not use_state
    if use_state:
        (x_ref, g_ref, w1_ref, w2_ref, cw_ref, alog_ref, dtb_ref, ng_ref, wout_ref, fg_ref,
         tri_ref, cst_ref, sst_ref,
         y_ref, cout_ref, s_ref,
         cbuf, qn_s, kn_s, v_s, z_s, bt_s, gc_s, o_s, on_s,
         qg_s, kd_s, u_s, w_s, intra_s, sb_s) = refs
    else:
        (x_ref, g_ref, w1_ref, w2_ref, cw_ref, alog_ref, dtb_ref, ng_ref, wout_ref, fg_ref,
         tri_ref,
         y_ref, cout_ref, s_ref,
         cbuf, qn_s, kn_s, v_s, z_s, bt_s, gc_s, o_s, on_s,
         qg_s, kd_s, u_s, w_s, intra_s, sb_s, hist_s, xs_s, ys_s) = refs
        assert ns == 1 and chunk == SUBLANES * SUBLANES
    t = pl.program_id(1)
    rows = ns * tls
    hist = CONV_WIDTH - 1
    qk_w = DN_HEADS * DN_DIM
    conv_w = 3 * qk_w
    n_pairs = DN_HEADS // 2
    rres = tls // SUBLANES

    if use_state:
        s_ref[...] = sst_ref[...]
        cbuf[:, CONV_PAD - hist:CONV_PAD, :] = cst_ref[...]
        sb_s[...] = jnp.zeros(sb_s.shape, BF16)
        for s in range(ns):
            for hd in range(DN_HEADS):
                d0 = (hd % 2) * DN_DIM
                sb_s[s, hd // 2, d0:d0 + DN_DIM, d0:d0 + DN_DIM] = sst_ref[s, hd].astype(BF16)
    else:
        @pl.when(t == 0)
        def _():
            s_ref[...] = jnp.zeros(s_ref.shape, F32)
            sb_s[...] = jnp.zeros(sb_s.shape, BF16)
            hist_s[...] = jnp.zeros(hist_s.shape, F32)

    n_slabs = D_MODEL // LANES
    if rm:
        for j in range(n_slabs):
            xs_s[j * tls:(j + 1) * tls, :] = x_ref[0, :, j * LANES:(j + 1) * LANES]
        x = jnp.concatenate(
            [jnp.concatenate([xs_s[pl.ds(j * tls + r, rres, stride=SUBLANES), :] for j in range(n_slabs)], axis=1)
             for r in range(SUBLANES)], axis=0)
    else:
        x = x_ref[...].reshape(rows, D_MODEL)
    h = (x * _rms_scale(x) * g_ref[...]).astype(BF16)

    def phase_a():
        ba = _dot(h, w2_ref[...])
        bt_s[...] = _sigmoid(ba[:, :LANES])
        a = ba[:, LANES:] + dtb_ref[...]
        softplus = jnp.maximum(a, 0.0) + jnp.log1p(jnp.exp(-jnp.abs(a)))
        g = -jnp.exp(alog_ref[...]) * softplus
        g_hi, g_mid, g_lo = _split3(g)
        gcs = _dot(tri_ref[...], jnp.concatenate([g_hi, g_mid, g_lo], axis=1))
        gc_s[...] = gcs[:, :LANES] + gcs[:, LANES:2 * LANES] + gcs[:, 2 * LANES:]
        yield
        for c0 in [c for seg in CONV_SEG_ORDER for c in range(seg * qk_w, (seg + 1) * qk_w, MXU_DIM)]:
            raw = _dot(h, w1_ref[:, c0:c0 + MXU_DIM])
            if rm:
                cbuf[:, c0:c0 + MXU_DIM] = raw
            else:
                cbuf[:, CONV_PAD:, c0:c0 + MXU_DIM] = raw.reshape(ns, tls, MXU_DIM)
            yield
        for c0 in range(0, qk_w, MXU_DIM):
            z_s[:, c0:c0 + MXU_DIM] = _dot(h, w1_ref[:, conv_w + c0:conv_w + c0 + MXU_DIM])
            yield

    def conv_block(s, cols):
        if not rm:
            acc = None
            for j in range(CONV_WIDTH):
                lo = CONV_PAD - hist + j
                term = cbuf[s, lo:lo + tls, cols] * cw_ref[j:j + 1, cols]
                acc = term if acc is None else acc + term
            return acc
        xb = cbuf[:, cols]
        wrapped = []
        for k in range(hist):
            r = SUBLANES - hist + k
            full = jnp.concatenate([hist_s[k, :, cols], xb[r * rres:(r + 1) * rres]], axis=0)
            wrapped.append(full[SUBLANES - 1:SUBLANES - 1 + rres])
        ext = jnp.concatenate(wrapped + [xb], axis=0)
        acc = None
        for j in range(CONV_WIDTH):
            term = ext[j * rres:j * rres + tls] * cw_ref[j:j + 1, cols]
            acc = term if acc is None else acc + term
        return acc

    def phase_b():
        for blk in [b for seg in CONV_SEG_ORDER for b in range(seg * DN_HEADS, (seg + 1) * DN_HEADS)]:
            cols = slice(blk * LANES, (blk + 1) * LANES)
            seg, hb = divmod(blk, DN_HEADS)
            dst = slice(hb * LANES, (hb + 1) * LANES)
            for s in range(ns):
                acc = conv_block(s, cols)
                yv = acc * _sigmoid(acc)
                rs = slice(s * tls, (s + 1) * tls)
                if seg == 2:
                    v_s[rs, dst] = yv
                else:
                    inv = lax.rsqrt(jnp.sum(yv * yv, axis=-1, keepdims=True) + EPS)
                    if seg == 0:
                        qn_s[rs, dst] = yv * (inv * (DN_DIM ** -0.5))
                    else:
                        kn_s[rs, dst] = yv * inv
            if blk % 2 == 1:
                yield

    ga, gb = phase_a(), phase_b()
    for _ in range(1 + qk_w // MXU_DIM):
        next(ga)
    _run_interleaved(gb, ga)

    if rm:
        for k in range(hist):
            r = SUBLANES - hist + k
            last = cbuf[(r + 1) * rres - SUBLANES:(r + 1) * rres, :]
            hist_s[k] = last
            cout_ref[0, k:k + 1, :] = last[SUBLANES - 1:SUBLANES]
    else:
        cout_ref[...] = cbuf[:, CONV_PAD + tls - hist:CONV_PAD + tls, :]

    hg = MXU_DIM // chunk
    n_groups = DN_HEADS // hg
    n_levels = int(round(math.log2(chunk)))
    ri = lax.broadcasted_iota(jnp.int32, (MXU_DIM, MXU_DIM), 0)
    ci = lax.broadcasted_iota(jnp.int32, (MXU_DIM, MXU_DIM), 1)

    def time_of(pos):
        p = pos % chunk
        return SUBLANES * (p % SUBLANES) + p // SUBLANES if rm else p

    tri_mask = (ri // chunk == ci // chunk) & (time_of(ri) >= time_of(ci))
    diag = ri == ci
    eye = jnp.where(diag, 1.0, 0.0).astype(F32)
    lane = lax.broadcasted_iota(jnp.int32, (MXU_DIM, LANES), 1)
    ones_rows = jnp.ones((2 * SUBLANES, LANES), BF16)
    n_chunks = tls // chunk

    def hcols(hd):
        return slice(hd * LANES, (hd + 1) * LANES)

    def crows(ref, s, c, cols):
        if rm:
            return jnp.concatenate(
                [ref[r * rres + SUBLANES * c:r * rres + SUBLANES * (c + 1), cols] for r in range(SUBLANES)], axis=0)
        f0 = s * tls + c * chunk
        return ref[f0:f0 + chunk, cols]

    def crow_last(ref, s, c, cols):
        r = (SUBLANES - 1) * rres + SUBLANES * c + SUBLANES - 1 if rm else s * tls + (c + 1) * chunk - 1
        return ref[r:r + 1, cols]

    def phase_c(groups):
        st = []
        for (s, c, grp) in groups:
            f0 = s * tls + c * chunk
            rsl = slice(f0, f0 + chunk)
            heads = list(range(grp * hg, (grp + 1) * hg))
            k_h, q_h, rhs_rows, gc_cols, beta_cols = [], [], [], [], []
            for hd in heads:
                beta_h = crows(bt_s, s, c, slice(hd, hd + 1))
                gc_h = crows(gc_s, s, c, slice(hd, hd + 1))
                gl_h = crow_last(gc_s, s, c, slice(hd, hd + 1))
                eg = jnp.exp(gc_h)
                kf = crows(kn_s, s, c, hcols(hd))
                qf = crows(qn_s, s, c, hcols(hd))
                vb = (crows(v_s, s, c, hcols(hd)) * beta_h).astype(BF16)
                kbg = (kf * (beta_h * eg)).astype(BF16)
                qg_s[rsl, hcols(hd)] = (qf * eg).astype(BF16)
                kd_s[rsl, hcols(hd)] = (kf * jnp.exp(gl_h - gc_h)).astype(BF16)
                k_h.append(kf.astype(BF16))
                q_h.append(qf.astype(BF16))
                rhs_rows.append(jnp.concatenate([vb, kbg], axis=1))
                gc_cols.append(gc_h)
                beta_cols.append(beta_h)
            gc_c = jnp.concatenate(gc_cols, axis=0)
            hi, mid, lo = (p.astype(F32) for p in _split3(gc_c))
            rmat = jnp.where(lane == 0, hi, jnp.where(lane == 1, mid, jnp.where(lane == 2, lo, 0.0)))
            gc_row = _dot_nt(ones_rows, rmat.astype(BF16))[0:1]
            k_st = jnp.concatenate(k_h, axis=0)
            q_st = jnp.concatenate(q_h, axis=0)
            st.append(dict(
                rsl=rsl, heads=heads, gi=(s * n_chunks + c) * n_groups + grp,
                rhs=jnp.concatenate(rhs_rows, axis=0),
                beta=jnp.concatenate(beta_cols, axis=0),
                gd=gc_c - gc_row,
                kq=_dot_nt(jnp.concatenate([k_st, q_st], axis=0), k_st)))
        yield
        for d in st:
            decay = jnp.exp(jnp.where(tri_mask, d.pop('gd'), -jnp.inf))
            kq = d.pop('kq')
            m = jnp.where(diag, 0.0, kq[:MXU_DIM] * decay * d.pop('beta'))
            intra_s[d['gi']] = (kq[MXU_DIM:] * decay).astype(BF16)
            mb = m.astype(BF16)
            d['p'] = eye - m
            d['q'] = _dot(mb, mb).astype(BF16)
        yield
        for lvl in range(1, n_levels):
            for d in st:
                qb = d['q']
                if lvl < n_levels - 1:
                    pq = _dot(jnp.concatenate([d['p'].astype(BF16), qb], axis=0), qb)
                    d['p'] = d['p'] + pq[:MXU_DIM]
                    d['q'] = pq[MXU_DIM:].astype(BF16)
                else:
                    d['p'] = d['p'] + _dot(d['p'].astype(BF16), qb)
            yield
        for d in st:
            uw = _dot(d['p'].astype(BF16), d['rhs'])
            for i, hd in enumerate(d['heads']):
                hs = slice(i * chunk, (i + 1) * chunk)
                u_s[d['rsl'], hcols(hd)] = uw[hs, :LANES]
                w_s[d['rsl'], hcols(hd)] = uw[hs, LANES:].astype(BF16)
        yield

    def phase_d(s, c):
        f0 = s * tls + c * chunk
        rsl = slice(f0, f0 + chunk)
        both = []
        for p in range(n_pairs):
            pc = slice(p * MXU_DIM, (p + 1) * MXU_DIM)
            lhs = jnp.concatenate([w_s[rsl, pc], qg_s[rsl, pc]], axis=0)
            both.append(_dot(lhs, sb_s[s, p]))
        yield
        vnb, outer = [], []
        for p in range(n_pairs):
            pc = slice(p * MXU_DIM, (p + 1) * MXU_DIM)
            v_new = (u_s[rsl, pc] - both[p][:chunk]).astype(BF16)
            for i in range(2):
                hd = 2 * p + i
                vh = v_new[:, i * LANES:(i + 1) * LANES]
                vnb.append(vh)
                outer.append(_dot_tn(kd_s[rsl, hcols(hd)], vh))
        for grp in range(n_groups):
            heads = list(range(grp * hg, (grp + 1) * hg))
            v_st = jnp.concatenate([vnb[hd] for hd in heads], axis=0)
            qs_st = jnp.concatenate(
                [both[hd // 2][chunk:, (hd % 2) * LANES:(hd % 2 + 1) * LANES] for hd in heads], axis=0)
            o_st = qs_st + _dot(intra_s[(s * n_chunks + c) * n_groups + grp], v_st)
            for i, hd in enumerate(heads):
                o_s[rsl, hcols(hd)] = o_st[i * chunk:(i + 1) * chunk]
        yield
        for hd in range(DN_HEADS):
            gl = crow_last(gc_s, s, c, slice(hd, hd + 1))
            s_new = s_ref[s, hd] * jnp.exp(gl) + outer[hd]
            s_ref[s, hd] = s_new
            d0 = (hd % 2) * DN_DIM
            sb_s[s, hd // 2, d0:d0 + DN_DIM, d0:d0 + DN_DIM] = s_new.astype(BF16)
        yield

    def phase_e(r0, r1):
        rsl = slice(r0, r1)
        for hd in range(DN_HEADS):
            o = o_s[rsl, hcols(hd)]
            if rm:
                zz = jnp.concatenate([crows(z_s, 0, c, hcols(hd)) for c in range(r0 // chunk, r1 // chunk)], axis=0)
            else:
                zz = z_s[rsl, hcols(hd)]
            on_s[rsl, hcols(hd)] = (o * _rms_scale(o) * ng_ref[...] * (zz * _sigmoid(zz))).astype(BF16)
            yield
        y1 = _dot(on_s[rsl, :], wout_ref[...])
        if rm:
            for i, c in enumerate(range(r0 // chunk, r1 // chunk)):
                for r in range(SUBLANES):
                    p0 = i * chunk + r * SUBLANES
                    for j in range(n_slabs):
                        ys_s[pl.ds(j * tls + c * chunk + r, SUBLANES, stride=SUBLANES), :] = (
                            y1[p0:p0 + SUBLANES, j * LANES:(j + 1) * LANES])
            y1 = jnp.concatenate([ys_s[j * tls + r0:j * tls + r1, :] for j in range(n_slabs)], axis=1)
            x2 = x_ref[0, r0:r1, :] + y1
            y_ref[0, r0:r1, :] = x2 * _rms_scale(x2) * fg_ref[...]
        else:
            x2 = x_ref[r0 // tls:r1 // tls].reshape(r1 - r0, D_MODEL) + y1
            yv = x2 * _rms_scale(x2) * fg_ref[...]
            y_ref[r0 // tls:r1 // tls] = yv.reshape((r1 - r0) // tls, tls, D_MODEL)
        yield

    units = [(s, c) for s in range(ns) for c in range(n_chunks)]
    upb = max(1, 4 // n_groups)
    batches = [units[i:i + upb] for i in range(0, len(units), upb)]
    brows = upb * chunk

    def c_of(batch):
        return phase_c([(s, c, grp) for (s, c) in batch for grp in range(n_groups)])

    def d_of(batch):
        if use_state:
            return [phase_d(s, c) for (s, c) in batch]
        return [_chain([phase_d(s, c) for (s, c) in batch])]

    _run_interleaved(c_of(batches[0]), ga)
    for i in range(1, len(batches)):
        extra = [phase_e((i - 2) * brows, (i - 1) * brows)] if i >= 2 else []
        _run_interleaved(c_of(batches[i]), *d_of(batches[i - 1]), *extra)
    last = len(batches) - 1
    extra = [phase_e((last - 1) * brows, last * brows)] if last >= 1 else []
    _run_interleaved(*d_of(batches[last]), *extra)
    _run_interleaved(phase_e(last * brows, (last + 1) * brows))


def _block_tri(rows, chunk, residue_major):
    q = np.arange(rows)
    tm = SUBLANES * (q % (rows // SUBLANES)) + q // (rows // SUBLANES) if residue_major else q
    m = (tm[:, None] // chunk == tm[None, :] // chunk) & (tm[:, None] >= tm[None, :])
    return jnp.asarray(m.astype(np.float32), BF16)


def _dn_layer(x, g, w1, w2, conv_w, alog, dtb, ng, w_out, fg, *, ns, tls, chunk, state=None):
    nb, seq, _ = x.shape
    use_state = state is not None
    grid = (nb // ns, seq // tls)
    rows = ns * tls
    hist = CONV_WIDTH - 1
    conv_dim = 3 * DN_HEADS * DN_DIM
    n_stacks = rows * DN_HEADS // MXU_DIM
    const2 = lambda b, t: (0, 0)
    tri = _block_tri(rows, chunk, not use_state)
    args = [x, g, w1, w2, conv_w, alog, dtb, ng, w_out, fg, tri]
    x_spec = pl.BlockSpec((ns, tls, D_MODEL), lambda b, t: (b, t, 0))
    in_specs = [x_spec]
    in_specs += [pl.BlockSpec(a.shape, const2, pipeline_mode=pl.Buffered(1)) for a in args[1:]]
    if use_state:
        in_specs += [
            pl.BlockSpec((None, ns, hist, conv_dim), lambda b, t: (0, b, 0, 0)),
            pl.BlockSpec((None, ns, DN_HEADS, DN_DIM, DN_DIM), lambda b, t: (0, b, 0, 0, 0)),
        ]
        args += list(state)
    out_shape = (
        jax.ShapeDtypeStruct((nb, seq, D_MODEL), F32),
        jax.ShapeDtypeStruct((nb, hist, conv_dim), F32),
        jax.ShapeDtypeStruct((nb, DN_HEADS, DN_DIM, DN_DIM), F32),
    )
    out_specs = (
        x_spec,
        pl.BlockSpec((ns, hist, conv_dim), lambda b, t: (b, 0, 0)),
        pl.BlockSpec((ns, DN_HEADS, DN_DIM, DN_DIM), lambda b, t: (b, 0, 0, 0)),
    )
    if use_state:
        cbuf = pltpu.VMEM((ns, CONV_PAD + tls, conv_dim), F32)
    else:
        cbuf = pltpu.VMEM((tls, conv_dim), F32)
    scratch = [
        cbuf,
        pltpu.VMEM((rows, D_MODEL), F32),
        pltpu.VMEM((rows, D_MODEL), F32),
        pltpu.VMEM((rows, D_MODEL), F32),
        pltpu.VMEM((rows, D_MODEL), F32),
        pltpu.VMEM((rows, LANES), F32),
        pltpu.VMEM((rows, LANES), F32),
        pltpu.VMEM((rows, D_MODEL), F32),
        pltpu.VMEM((rows, D_MODEL), BF16),
        pltpu.VMEM((rows, D_MODEL), BF16),
        pltpu.VMEM((rows, D_MODEL), BF16),
        pltpu.VMEM((rows, D_MODEL), F32),
        pltpu.VMEM((rows, D_MODEL), BF16),
        pltpu.VMEM((n_stacks, MXU_DIM, MXU_DIM), BF16),
        pltpu.VMEM((ns, DN_HEADS // 2, MXU_DIM, MXU_DIM), BF16),
    ]
    if not use_state:
        scratch.append(pltpu.VMEM((hist, SUBLANES, conv_dim), F32))
        scratch.append(pltpu.VMEM((D_MODEL // LANES * tls, LANES), F32))
        scratch.append(pltpu.VMEM((D_MODEL // LANES * tls, LANES), F32))
    return pl.pallas_call(
        functools.partial(_dn_kernel, ns=ns, tls=tls, chunk=chunk, use_state=use_state),
        grid=grid, in_specs=in_specs, out_specs=out_specs, out_shape=out_shape,
        scratch_shapes=scratch,
        compiler_params=pltpu.CompilerParams(
            dimension_semantics=("arbitrary", "arbitrary"),
            vmem_limit_bytes=VMEM_LIMIT_BYTES),
        name="dn_state" if use_state else "dn_prompt",
    )(*args)


def _pad_lanes(v):
    return jnp.pad(v.astype(F32)[None, :], ((0, 0), (0, LANES - v.shape[0])))


def kernel(x_prompt, x_sample, cache_k, cache_v, state_conv, state_ssm, norm_g, final_norm_g,
           attn_w_in, attn_sinks, attn_w_out, dn_w_in, dn_conv_w, dn_a_log, dn_dt_bias,
           dn_norm_g, dn_w_out):
    attn_width = N_HEADS * HEAD_DIM
    kv_width = N_KV_HEADS * HEAD_DIM
    col_scale = jnp.where(jnp.arange(attn_w_in.shape[-1]) < attn_width, HEAD_DIM ** -0.5 * LOG2E, 1.0).astype(F32)
    a_win = (attn_w_in[0] * col_scale).astype(BF16)
    a_wout = attn_w_out[0].astype(BF16)
    g0 = norm_g[0].reshape(1, D_MODEL)
    sinks = attn_sinks[0]

    wide = 3 * DN_HEADS * DN_DIM + DN_HEADS * DN_DIM
    dw = dn_w_in[0]
    d_w1 = dw.astype(BF16)
    lane_pad = ((0, 0), (0, LANES - DN_HEADS))
    d_w2 = jnp.concatenate([jnp.pad(dw[:, wide:wide + DN_HEADS], lane_pad),
                            jnp.pad(dw[:, wide + DN_HEADS:], lane_pad)], axis=1).astype(BF16)
    d_wout = dn_w_out[0].astype(BF16)
    g1 = norm_g[1].reshape(1, D_MODEL)
    fg = final_norm_g.reshape(1, D_MODEL)
    alog = _pad_lanes(dn_a_log[0])
    dtb = _pad_lanes(dn_dt_bias[0])
    ng = dn_norm_g[0].reshape(1, DN_DIM)
    dn_args = (g1, d_w1, d_w2, dn_conv_w[0], alog, dtb, ng, d_wout, fg)

    x1p, kp, vp = _attn_layer(x_prompt, g0, a_win, a_wout, sinks, ns=1, tls=ATTN_TILE_ROWS, chunk=CHUNK)
    yp, cp, sp = _dn_layer(x1p, *dn_args, ns=1, tls=DN_TILE_ROWS, chunk=CHUNK)

    n_run, dec_seq = x_sample.shape[:2]
    ck = cache_k[0].reshape(n_run, WINDOW, kv_width)
    cv = cache_v[0].reshape(n_run, WINDOW, kv_width)
    x1s, ks, vs = _attn_layer(x_sample, g0, a_win, a_wout, sinks, ns=RUN_SEQS_PER_STEP, tls=dec_seq,
                              chunk=min(CHUNK, dec_seq), cache=(ck, cv))
    ys, cs, ss = _dn_layer(x1s, *dn_args, ns=RUN_SEQS_PER_STEP, tls=dec_seq, chunk=min(CHUNK, dec_seq),
                           state=(state_conv, state_ssm))

    def kv_out(t):
        return t.reshape(1, t.shape[0], WINDOW, N_KV_HEADS, HEAD_DIM)

    return (yp, ys, kv_out(kp), kv_out(vp), cp[None], sp[None],
            kv_out(ks), kv_out(vs), cs[None], ss[None])
```

```python
import functools
import math

import numpy as np
import jax
import jax.numpy as jnp
from jax import lax
from jax.experimental import pallas as pl
from jax.experimental.pallas import tpu as pltpu

F32 = jnp.float32
BF16 = jnp.bfloat16

D_MODEL = 1024
EPS = 1e-6
CHUNK = 64
WINDOW = 128
N_HEADS = 16
N_KV_HEADS = 4
HEAD_DIM = 64
DN_HEADS = 8
DN_DIM = 128
CONV_WIDTH = 4
CONV_PAD = 8
SUBLANES = 8
LANES = 128
MXU_DIM = 256
VMEM_LIMIT_BYTES = 56 * 1024 * 1024
ATTN_TILE_ROWS = 1024
DN_TILE_ROWS = 512
RUN_SEQS_PER_STEP = 8
ATTN_LOCKSTEP = 4
LOG2E = 1.4426950408889634

_NT = (((1,), (1,)), ((), ()))
_TN = (((0,), (0,)), ((), ()))


def _dot(a, b):
    return jnp.dot(a, b, preferred_element_type=F32)


def _dot_nt(a, b):
    return lax.dot_general(a, b, _NT, preferred_element_type=F32)


def _dot_tn(a, b):
    return lax.dot_general(a, b, _TN, preferred_element_type=F32)


def _sigmoid(x):
    return 1.0 / (1.0 + jnp.exp2(x * -LOG2E))


def _rms_scale(x):
    return lax.rsqrt(jnp.mean(x * x, axis=-1, keepdims=True) + EPS)


def _run_interleaved(*gens):
    live = list(gens)
    while live:
        for g in list(live):
            try:
                next(g)
            except StopIteration:
                live.remove(g)


def _chain(gens):
    for g in gens:
        yield from g


def _attn_kernel(*refs, ns, tls, chunk, use_cache):
    if use_cache:
        (x_ref, g_ref, win_ref, wout_ref, bias_ref, ck_ref, cv_ref,
         x1_ref, kout_ref, vout_ref,
         q_s, gate_s, klo_s, khi_s, v_s, og_s) = refs
    else:
        (x_ref, g_ref, win_ref, wout_ref, bias_ref,
         x1_ref, kout_ref, vout_ref,
         q_s, gate_s, klo_s, khi_s, v_s, og_s) = refs
    t = pl.program_id(1)
    rows = ns * tls
    keys = WINDOW + chunk
    kvn = N_KV_HEADS * HEAD_DIM
    kvw = 2 * kvn
    qw = N_HEADS * HEAD_DIM
    gw = qw // N_KV_HEADS

    half_lo = lax.broadcasted_iota(jnp.int32, (1, LANES), 1) < HEAD_DIM

    x = x_ref[...].reshape(rows, D_MODEL)
    h = (x * _rms_scale(x) * g_ref[...]).astype(BF16)
    q_s[...] = _dot(h, win_ref[:, 0:qw]).astype(BF16)
    kd = _dot(h, win_ref[:, qw:qw + kvn])
    vd = _dot(h, win_ref[:, qw + kvn:qw + 2 * kvn])

    def gate_proj():
        c0 = qw + 2 * kvn
        for s0 in range(0, qw, MXU_DIM):
            gate_s[:, s0:s0 + MXU_DIM] = _dot(h, win_ref[:, c0 + s0:c0 + s0 + MXU_DIM])
            yield

    def spread(k2, v2, nrows, row0):
        for b in range(N_KV_HEADS // 2):
            kb = k2[:, b * LANES:(b + 1) * LANES]
            vb = v2[:, b * LANES:(b + 1) * LANES]
            kr = pltpu.roll(kb, HEAD_DIM, 1)
            vr = pltpu.roll(vb, HEAD_DIM, 1)
            forms = (
                (klo_s, jnp.where(half_lo, kb, 0.0), jnp.where(half_lo, kr, 0.0)),
                (khi_s, jnp.where(half_lo, 0.0, kr), jnp.where(half_lo, 0.0, kb)),
                (v_s, jnp.where(half_lo, vb, vr), jnp.where(half_lo, vr, vb)),
            )
            for ref, even, odd in forms:
                for j, val in ((2 * b, even), (2 * b + 1, odd)):
                    ref[:, row0:row0 + nrows, j * LANES:(j + 1) * LANES] = (
                        val.astype(BF16).reshape(ns, nrows, LANES))
                yield

    if use_cache:
        _run_interleaved(spread(ck_ref[...].reshape(ns * WINDOW, kvn), cv_ref[...].reshape(ns * WINDOW, kvn),
                                WINDOW, 0))
    else:
        @pl.when(t == 0)
        def _():
            zeros = jnp.zeros((ns, WINDOW, kvw), BF16)
            klo_s[:, 0:WINDOW, :] = zeros
            khi_s[:, 0:WINDOW, :] = zeros
            v_s[:, 0:WINDOW, :] = zeros

    _run_interleaved(spread(kd, vd, tls, WINDOW), gate_proj())

    kd3 = kd.reshape(ns, tls, kvn)
    vd3 = vd.reshape(ns, tls, kvn)
    if use_cache:
        kout_ref[:, 0:WINDOW - tls, :] = ck_ref[:, tls:, :]
        vout_ref[:, 0:WINDOW - tls, :] = cv_ref[:, tls:, :]
        kout_ref[:, WINDOW - tls:, :] = kd3
        vout_ref[:, WINDOW - tls:, :] = vd3
    else:
        @pl.when(t == pl.num_programs(1) - 1)
        def _():
            kout_ref[...] = kd3[:, tls - WINDOW:, :]
            vout_ref[...] = vd3[:, tls - WINDOW:, :]

    pad = MXU_DIM - keys
    col = lax.broadcasted_iota(jnp.int32, (chunk, MXU_DIM), 1)
    pair_lo = lax.broadcasted_iota(jnp.int32, (chunk, LANES), 1) < HEAD_DIM
    ones_blk = jnp.ones((keys, LANES), BF16)
    k_pad = jnp.zeros((pad, LANES), BF16)
    v_pad = jnp.concatenate([jnp.zeros((pad, LANES), BF16), jnp.ones((pad, LANES), BF16)], axis=1)
    n_chunks = tls // chunk

    def attend(s, c):
        r0 = c * chunk
        f0 = s * tls + r0
        masked = (not use_cache) and r0 < WINDOW
        if masked:
            n_inv = jnp.maximum(0, WINDOW - (t * tls + r0))
            valid = col >= n_inv
        scores = []
        for j in range(N_KV_HEADS):
            qj = q_s[f0:f0 + chunk, j * gw:(j + 1) * gw]
            lhs = jnp.concatenate([qj[:, :LANES], qj[:, LANES:]], axis=0)
            klo = jnp.concatenate([klo_s[s, r0:r0 + keys, j * LANES:(j + 1) * LANES], k_pad], axis=0)
            khi = jnp.concatenate([khi_s[s, r0:r0 + keys, j * LANES:(j + 1) * LANES], k_pad], axis=0)
            s_even = _dot_nt(lhs, klo)
            s_odd = _dot_nt(lhs, khi)
            scores.append((s_even[:chunk], s_odd[:chunk], s_even[chunk:], s_odd[chunk:]))
        yield
        ovs = []
        for j in range(N_KV_HEADS):
            ps = []
            for g in range(4):
                sg = scores[j][g] + bias_ref[4 * j + g]
                if masked:
                    sg = jnp.where(valid, sg, -jnp.inf)
                m = jnp.max(sg, axis=-1, keepdims=True)
                ps.append(jnp.exp2(sg - m).astype(BF16))
            p_all = jnp.concatenate(ps, axis=0)
            vj = v_s[s, r0:r0 + keys, j * LANES:(j + 1) * LANES]
            vaug = jnp.concatenate([jnp.concatenate([vj, ones_blk], axis=1), v_pad], axis=0)
            ovs.append(_dot(p_all, vaug))
        yield
        for j in range(N_KV_HEADS):
            outs = []
            for g in range(4):
                blk = ovs[j][g * chunk:(g + 1) * chunk]
                outs.append(blk[:, :LANES] / blk[:, LANES:])
            o01 = jnp.where(pair_lo, outs[0], outs[1])
            o23 = jnp.where(pair_lo, outs[2], outs[3])
            gt = gate_s[f0:f0 + chunk, j * gw:(j + 1) * gw]
            og = jnp.concatenate([o01, o23], axis=1) * (gt * _sigmoid(gt))
            og_s[f0:f0 + chunk, j * gw:(j + 1) * gw] = og.astype(BF16)
        yield

    units = [(s, c) for s in range(ns) for c in range(n_chunks)]
    for i in range(0, len(units), ATTN_LOCKSTEP):
        _run_interleaved(*[attend(s, c) for s, c in units[i:i + ATTN_LOCKSTEP]])

    if not use_cache:
        klo_s[:, 0:WINDOW, :] = klo_s[:, tls:tls + WINDOW, :]
        khi_s[:, 0:WINDOW, :] = khi_s[:, tls:tls + WINDOW, :]
        v_s[:, 0:WINDOW, :] = v_s[:, tls:tls + WINDOW, :]

    y = _dot(og_s[...], wout_ref[...])
    x1_ref[...] = (x + y).reshape(ns, tls, D_MODEL)


def _attn_bias(chunk, sinks):
    keys = WINDOW + chunk
    slopes = (2.0 ** (-8.0 * np.arange(1, N_HEADS + 1, dtype=np.float32) / N_HEADS)).astype(np.float32)
    i = np.arange(chunk, dtype=np.float32)[:, None]
    j = np.arange(keys, dtype=np.float32)[None, :]
    dist = np.abs(i + WINDOW - j).astype(np.float32)
    alibi = jnp.asarray(-(slopes[:, None, None] * dist[None]), F32) * LOG2E
    sink_col = jnp.broadcast_to((sinks.astype(F32) * LOG2E)[:, None, None], (N_HEADS, chunk, 1))
    rest = jnp.full((N_HEADS, chunk, MXU_DIM - keys - 1), -jnp.inf, F32)
    return jnp.concatenate([alibi, sink_col, rest], axis=2)


def _attn_layer(x, g, w_in_prep, w_out, sinks, *, ns, tls, chunk, cache=None):
    nb, seq, _ = x.shape
    use_cache = cache is not None
    grid = (nb // ns, seq // tls)
    kvn = N_KV_HEADS * HEAD_DIM
    kvw = 2 * kvn
    rows = ns * tls
    const2 = lambda b, t: (0, 0)
    in_specs = [
        pl.BlockSpec((ns, tls, D_MODEL), lambda b, t: (b, t, 0)),
        pl.BlockSpec((1, D_MODEL), const2),
        pl.BlockSpec(w_in_prep.shape, const2, pipeline_mode=pl.Buffered(1)),
        pl.BlockSpec(w_out.shape, const2, pipeline_mode=pl.Buffered(1)),
        pl.BlockSpec((N_HEADS, chunk, MXU_DIM), lambda b, t: (0, 0, 0), pipeline_mode=pl.Buffered(1)),
    ]
    args = [x, g, w_in_prep, w_out, _attn_bias(chunk, sinks)]
    if use_cache:
        in_specs += [pl.BlockSpec((ns, WINDOW, kvn), lambda b, t: (b, 0, 0))] * 2
        args += list(cache)
    out_shape = (
        jax.ShapeDtypeStruct((nb, seq, D_MODEL), F32),
        jax.ShapeDtypeStruct((nb, WINDOW, kvn), F32),
        jax.ShapeDtypeStruct((nb, WINDOW, kvn), F32),
    )
    out_specs = (
        pl.BlockSpec((ns, tls, D_MODEL), lambda b, t: (b, t, 0)),
        pl.BlockSpec((ns, WINDOW, kvn), lambda b, t: (b, 0, 0)),
        pl.BlockSpec((ns, WINDOW, kvn), lambda b, t: (b, 0, 0)),
    )
    scratch = [
        pltpu.VMEM((rows, D_MODEL), BF16),
        pltpu.VMEM((rows, D_MODEL), F32),
        pltpu.VMEM((ns, WINDOW + tls, kvw), BF16),
        pltpu.VMEM((ns, WINDOW + tls, kvw), BF16),
        pltpu.VMEM((ns, WINDOW + tls, kvw), BF16),
        pltpu.VMEM((rows, D_MODEL), BF16),
    ]
    return pl.pallas_call(
        functools.partial(_attn_kernel, ns=ns, tls=tls, chunk=chunk, use_cache=use_cache),
        grid=grid, in_specs=in_specs, out_specs=out_specs, out_shape=out_shape,
        scratch_shapes=scratch,
        compiler_params=pltpu.CompilerParams(
            dimension_semantics=("arbitrary", "arbitrary"),
            vmem_limit_bytes=VMEM_LIMIT_BYTES),
        name="attn_cache" if use_cache else "attn_prompt",
    )(*args)


def _split3(x):
    hi = x.astype(BF16)
    r1 = x - hi.astype(F32)
    mid = r1.astype(BF16)
    lo = (r1 - mid.astype(F32)).astype(BF16)
    return hi, mid, lo


def _dn_kernel(*refs, ns, tls, chunk, use_state):
    rm = not use_state
    if use_state:
        (x_ref, g_ref, w1_ref, w2_ref, cw_ref, alog_ref, dtb_ref, ng_ref, wout_ref, fg_ref,
         tri_ref, cst_ref, sst_ref,
         y_ref, cout_ref, s_ref,
         cbuf, qn_s, kn_s, v_s, z_s, bt_s, gc_s, o_s, on_s,
         qg_s, kd_s, u_s, w_s, intra_s, sb_s) = refs
    else:
        (x_ref, g_ref, w1_ref, w2_ref, cw_ref, alog_ref, dtb_ref, ng_ref, wout_ref, fg_ref,
         tri_ref,
         y_ref, cout_ref, s_ref,
         cbuf, qn_s, kn_s, v_s, z_s, bt_s, gc_s, o_s, on_s,
         qg_s, kd_s, u_s, w_s, intra_s, sb_s, hist_s, xs_s, ys_s) = refs
        assert ns == 1 and chunk == SUBLANES * SUBLANES
    t = pl.program_id(1)
    rows = ns * tls
    hist = CONV_WIDTH - 1
    qk_w = DN_HEADS * DN_DIM
    conv_w = 3 * qk_w
    n_pairs = DN_HEADS // 2
    rres = tls // SUBLANES

    if use_state:
        s_ref[...] = sst_ref[...]
        cbuf[:, CONV_PAD - hist:CONV_PAD, :] = cst_ref[...]
        sb_s[...] = jnp.zeros(sb_s.shape, BF16)
        for s in range(ns):
            for hd in range(DN_HEADS):
                d0 = (hd % 2) * DN_DIM
                sb_s[s, hd // 2, d0:d0 + DN_DIM, d0:d0 + DN_DIM] = sst_ref[s, hd].astype(BF16)
    else:
        @pl.when(t == 0)
        def _():
            s_ref[...] = jnp.zeros(s_ref.shape, F32)
            sb_s[...] = jnp.zeros(sb_s.shape, BF16)
            hist_s[...] = jnp.zeros(hist_s.shape, F32)

    n_slabs = D_MODEL // LANES
    if rm:
        for j in range(n_slabs):
            xs_s[j * tls:(j + 1) * tls, :] = x_ref[0, :, j * LANES:(j + 1) * LANES]
        x = jnp.concatenate(
            [jnp.concatenate([xs_s[pl.ds(j * tls + r, rres, stride=SUBLANES), :] for j in range(n_slabs)], axis=1)
             for r in range(SUBLANES)], axis=0)
    else:
        x = x_ref[...].reshape(rows, D_MODEL)
    h = (x * _rms_scale(x) * g_ref[...]).astype(BF16)

    def phase_a():
        ba = _dot(h, w2_ref[...])
        bt_s[...] = _sigmoid(ba[:, :LANES])
        a = ba[:, LANES:] + dtb_ref[...]
        softplus = jnp.maximum(a, 0.0) + jnp.log1p(jnp.exp(-jnp.abs(a)))
        g = -jnp.exp(alog_ref[...]) * softplus
        g_hi, g_mid, g_lo = _split3(g)
        gcs = _dot(tri_ref[...], jnp.concatenate([g_hi, g_mid, g_lo], axis=1))
        gc_s[...] = gcs[:, :LANES] + gcs[:, LANES:2 * LANES] + gcs[:, 2 * LANES:]
        yield
        for c0 in range(0, conv_w, MXU_DIM):
            raw = _dot(h, w1_ref[:, c0:c0 + MXU_DIM])
            if rm:
                cbuf[:, c0:c0 + MXU_DIM] = raw
            else:
                cbuf[:, CONV_PAD:, c0:c0 + MXU_DIM] = raw.reshape(ns, tls, MXU_DIM)
            yield
        for c0 in range(0, qk_w, MXU_DIM):
            z_s[:, c0:c0 + MXU_DIM] = _dot(h, w1_ref[:, conv_w + c0:conv_w + c0 + MXU_DIM])
            yield

    def conv_block(s, cols):
        if not rm:
            acc = None
            for j in range(CONV_WIDTH):
                lo = CONV_PAD - hist + j
                term = cbuf[s, lo:lo + tls, cols] * cw_ref[j:j + 1, cols]
                acc = term if acc is None else acc + term
            return acc
        xb = cbuf[:, cols]
        wrapped = []
        for k in range(hist):
            r = SUBLANES - hist + k
            full = jnp.concatenate([hist_s[k, :, cols], xb[r * rres:(r + 1) * rres]], axis=0)
            wrapped.append(full[SUBLANES - 1:SUBLANES - 1 + rres])
        ext = jnp.concatenate(wrapped + [xb], axis=0)
        acc = None
        for j in range(CONV_WIDTH):
            term = ext[j * rres:j * rres + tls] * cw_ref[j:j + 1, cols]
            acc = term if acc is None else acc + term
        return acc

    def phase_b():
        for blk in range(conv_w // LANES):
            cols = slice(blk * LANES, (blk + 1) * LANES)
            seg, hb = divmod(blk, DN_HEADS)
            dst = slice(hb * LANES, (hb + 1) * LANES)
            for s in range(ns):
                acc = conv_block(s, cols)
                yv = acc * _sigmoid(acc)
                rs = slice(s * tls, (s + 1) * tls)
                if seg == 2:
                    v_s[rs, dst] = yv
                else:
                    inv = lax.rsqrt(jnp.sum(yv * yv, axis=-1, keepdims=True) + EPS)
                    if seg == 0:
                        qn_s[rs, dst] = yv * (inv * (DN_DIM ** -0.5))
                    else:
                        kn_s[rs, dst] = yv * inv
            if blk % 2 == 1:
                yield

    ga, gb = phase_a(), phase_b()
    for _ in range(1 + qk_w // MXU_DIM):
        next(ga)
    for _ in range(2 * DN_HEADS * LANES // (2 * LANES)):
        next(gb)
        next(ga)

    def save_conv_tail():
        if rm:
            for k in range(hist):
                r = SUBLANES - hist + k
                last = cbuf[(r + 1) * rres - SUBLANES:(r + 1) * rres, :]
                hist_s[k] = last
                cout_ref[0, k:k + 1, :] = last[SUBLANES - 1:SUBLANES]
        else:
            cout_ref[...] = cbuf[:, CONV_PAD + tls - hist:CONV_PAD + tls, :]

    hg = MXU_DIM // chunk
    n_groups = DN_HEADS // hg
    n_levels = int(round(math.log2(chunk)))
    ri = lax.broadcasted_iota(jnp.int32, (MXU_DIM, MXU_DIM), 0)
    ci = lax.broadcasted_iota(jnp.int32, (MXU_DIM, MXU_DIM), 1)

    def time_of(pos):
        p = pos % chunk
        return SUBLANES * (p % SUBLANES) + p // SUBLANES if rm else p

    tri_mask = (ri // chunk == ci // chunk) & (time_of(ri) >= time_of(ci))
    diag = ri == ci
    eye = jnp.where(diag, 1.0, 0.0).astype(F32)
    lane = lax.broadcasted_iota(jnp.int32, (MXU_DIM, LANES), 1)
    ones_rows = jnp.ones((2 * SUBLANES, LANES), BF16)
    n_chunks = tls // chunk

    def hcols(hd):
        return slice(hd * LANES, (hd + 1) * LANES)

    def crows(ref, s, c, cols):
        if rm:
            return jnp.concatenate(
                [ref[r * rres + SUBLANES * c:r * rres + SUBLANES * (c + 1), cols] for r in range(SUBLANES)], axis=0)
        f0 = s * tls + c * chunk
        return ref[f0:f0 + chunk, cols]

    def crow_last(ref, s, c, cols):
        r = (SUBLANES - 1) * rres + SUBLANES * c + SUBLANES - 1 if rm else s * tls + (c + 1) * chunk - 1
        return ref[r:r + 1, cols]

    def phase_c(groups):
        st = []
        for (s, c, grp) in groups:
            f0 = s * tls + c * chunk
            rsl = slice(f0, f0 + chunk)
            heads = list(range(grp * hg, (grp + 1) * hg))
            k_h, q_h, gc_cols, beta_cols = [], [], [], []
            for hd in heads:
                beta_h = crows(bt_s, s, c, slice(hd, hd + 1))
                gc_h = crows(gc_s, s, c, slice(hd, hd + 1))
                gl_h = crow_last(gc_s, s, c, slice(hd, hd + 1))
                kf = crows(kn_s, s, c, hcols(hd))
                qf = crows(qn_s, s, c, hcols(hd))
                qg_s[rsl, hcols(hd)] = (qf * jnp.exp(gc_h)).astype(BF16)
                kd_s[rsl, hcols(hd)] = (kf * jnp.exp(gl_h - gc_h)).astype(BF16)
                k_h.append(kf.astype(BF16))
                q_h.append(qf.astype(BF16))
                gc_cols.append(gc_h)
                beta_cols.append(beta_h)
            gc_c = jnp.concatenate(gc_cols, axis=0)
            hi, mid, lo = (p.astype(F32) for p in _split3(gc_c))
            rmat = jnp.where(lane == 0, hi, jnp.where(lane == 1, mid, jnp.where(lane == 2, lo, 0.0)))
            gc_row = _dot_nt(ones_rows, rmat.astype(BF16))[0:1]
            k_st = jnp.concatenate(k_h, axis=0)
            q_st = jnp.concatenate(q_h, axis=0)
            st.append(dict(
                rsl=rsl, heads=heads, sc=(s, c), gi=(s * n_chunks + c) * n_groups + grp,
                beta=jnp.concatenate(beta_cols, axis=0),
                gd=gc_c - gc_row,
                kq=_dot_nt(jnp.concatenate([k_st, q_st], axis=0), k_st)))
        yield
        for d in st:
            decay = jnp.exp(jnp.where(tri_mask, d.pop('gd'), -jnp.inf))
            kq = d.pop('kq')
            m = jnp.where(diag, 0.0, kq[:MXU_DIM] * decay * d.pop('beta'))
            intra_s[d['gi']] = (kq[MXU_DIM:] * decay).astype(BF16)
            mb = m.astype(BF16)
            d['p'] = eye - m
            d['q'] = _dot(mb, mb).astype(BF16)
        yield
        for lvl in range(1, n_levels):
            for d in st:
                qb = d['q']
                if lvl < n_levels - 1:
                    pq = _dot(jnp.concatenate([d['p'].astype(BF16), qb], axis=0), qb)
                    d['p'] = d['p'] + pq[:MXU_DIM]
                    d['q'] = pq[MXU_DIM:].astype(BF16)
                else:
                    d['p'] = d['p'] + _dot(d['p'].astype(BF16), qb)
            yield
        for d in st:
            rhs_rows = []
            for hd in d['heads']:
                beta_h = crows(bt_s, *d['sc'], slice(hd, hd + 1))
                gc_h = crows(gc_s, *d['sc'], slice(hd, hd + 1))
                vb = (crows(v_s, *d['sc'], hcols(hd)) * beta_h).astype(BF16)
                kbg = (crows(kn_s, *d['sc'], hcols(hd)) * (beta_h * jnp.exp(gc_h))).astype(BF16)
                rhs_rows.append(jnp.concatenate([vb, kbg], axis=1))
            uw = _dot(d['p'].astype(BF16), jnp.concatenate(rhs_rows, axis=0))
            for i, hd in enumerate(d['heads']):
                hs = slice(i * chunk, (i + 1) * chunk)
                u_s[d['rsl'], hcols(hd)] = uw[hs, :LANES]
                w_s[d['rsl'], hcols(hd)] = uw[hs, LANES:].astype(BF16)
        yield

    def phase_d(s, c):
        f0 = s * tls + c * chunk
        rsl = slice(f0, f0 + chunk)
        both = []
        for p in range(n_pairs):
            pc = slice(p * MXU_DIM, (p + 1) * MXU_DIM)
            lhs = jnp.concatenate([w_s[rsl, pc], qg_s[rsl, pc]], axis=0)
            both.append(_dot(lhs, sb_s[s, p]))
        yield
        vnb, outer = [], []
        for p in range(n_pairs):
            pc = slice(p * MXU_DIM, (p + 1) * MXU_DIM)
            v_new = (u_s[rsl, pc] - both[p][:chunk]).astype(BF16)
            for i in range(2):
                hd = 2 * p + i
                vh = v_new[:, i * LANES:(i + 1) * LANES]
                vnb.append(vh)
                outer.append(_dot_tn(kd_s[rsl, hcols(hd)], vh))
        for grp in range(n_groups):
            heads = list(range(grp * hg, (grp + 1) * hg))
            v_st = jnp.concatenate([vnb[hd] for hd in heads], axis=0)
            qs_st = jnp.concatenate(
                [both[hd // 2][chunk:, (hd % 2) * LANES:(hd % 2 + 1) * LANES] for hd in heads], axis=0)
            o_st = qs_st + _dot(intra_s[(s * n_chunks + c) * n_groups + grp], v_st)
            for i, hd in enumerate(heads):
                o_s[rsl, hcols(hd)] = o_st[i * chunk:(i + 1) * chunk]
        yield
        for hd in range(DN_HEADS):
            gl = crow_last(gc_s, s, c, slice(hd, hd + 1))
            s_new = s_ref[s, hd] * jnp.exp(gl) + outer[hd]
            s_ref[s, hd] = s_new
            d0 = (hd % 2) * DN_DIM
            sb_s[s, hd // 2, d0:d0 + DN_DIM, d0:d0 + DN_DIM] = s_new.astype(BF16)
        yield

    def phase_e(r0, r1):
        rsl = slice(r0, r1)
        for hd in range(DN_HEADS):
            o = o_s[rsl, hcols(hd)]
            if rm:
                zz = jnp.concatenate([crows(z_s, 0, c, hcols(hd)) for c in range(r0 // chunk, r1 // chunk)], axis=0)
            else:
                zz = z_s[rsl, hcols(hd)]
            on_s[rsl, hcols(hd)] = (o * _rms_scale(o) * ng_ref[...] * (zz * _sigmoid(zz))).astype(BF16)
            yield
        y1 = _dot(on_s[rsl, :], wout_ref[...])
        if rm:
            for i, c in enumerate(range(r0 // chunk, r1 // chunk)):
                for r in range(SUBLANES):
                    p0 = i * chunk + r * SUBLANES
                    for j in range(n_slabs):
                        ys_s[pl.ds(j * tls + c * chunk + r, SUBLANES, stride=SUBLANES), :] = (
                            y1[p0:p0 + SUBLANES, j * LANES:(j + 1) * LANES])
            y1 = jnp.concatenate([ys_s[j * tls + r0:j * tls + r1, :] for j in range(n_slabs)], axis=1)
            x2 = x_ref[0, r0:r1, :] + y1
            y_ref[0, r0:r1, :] = x2 * _rms_scale(x2) * fg_ref[...]
        else:
            x2 = x_ref[r0 // tls:r1 // tls].reshape(r1 - r0, D_MODEL) + y1
            yv = x2 * _rms_scale(x2) * fg_ref[...]
            y_ref[r0 // tls:r1 // tls] = yv.reshape((r1 - r0) // tls, tls, D_MODEL)
        yield

    units = [(s, c) for s in range(ns) for c in range(n_chunks)]
    upb = max(1, 4 // n_groups)
    batches = [units[i:i + upb] for i in range(0, len(units), upb)]
    brows = upb * chunk

    def c_of(batch):
        return phase_c([(s, c, grp) for (s, c) in batch for grp in range(n_groups)])

    def d_of(batch):
        if use_state:
            return [phase_d(s, c) for (s, c) in batch]
        return [_chain([phase_d(s, c) for (s, c) in batch])]

    _run_interleaved(c_of(batches[0]), gb, ga)
    save_conv_tail()
    for i in range(1, len(batches)):
        extra = [phase_e((i - 2) * brows, (i - 1) * brows)] if i >= 2 else []
        _run_interleaved(c_of(batches[i]), *d_of(batches[i - 1]), *extra)
    last = len(batches) - 1
    extra = [phase_e((last - 1) * brows, last * brows)] if last >= 1 else []
    _run_interleaved(*d_of(batches[last]), *extra)
    _run_interleaved(phase_e(last * brows, (last + 1) * brows))


def _block_tri(rows, chunk, residue_major):
    q = np.arange(rows)
    tm = SUBLANES * (q % (rows // SUBLANES)) + q // (rows // SUBLANES) if residue_major else q
    m = (tm[:, None] // chunk == tm[None, :] // chunk) & (tm[:, None] >= tm[None, :])
    return jnp.asarray(m.astype(np.float32), BF16)


def _dn_layer(x, g, w1, w2, conv_w, alog, dtb, ng, w_out, fg, *, ns, tls, chunk, state=None):
    nb, seq, _ = x.shape
    use_state = state is not None
    grid = (nb // ns, seq // tls)
    rows = ns * tls
    hist = CONV_WIDTH - 1
    conv_dim = 3 * DN_HEADS * DN_DIM
    n_stacks = rows * DN_HEADS // MXU_DIM
    const2 = lambda b, t: (0, 0)
    tri = _block_tri(rows, chunk, not use_state)
    args = [x, g, w1, w2, conv_w, alog, dtb, ng, w_out, fg, tri]
    x_spec = pl.BlockSpec((ns, tls, D_MODEL), lambda b, t: (b, t, 0))
    in_specs = [x_spec]
    in_specs += [pl.BlockSpec(a.shape, const2, pipeline_mode=pl.Buffered(1)) for a in args[1:]]
    if use_state:
        in_specs += [
            pl.BlockSpec((None, ns, hist, conv_dim), lambda b, t: (0, b, 0, 0)),
            pl.BlockSpec((None, ns, DN_HEADS, DN_DIM, DN_DIM), lambda b, t: (0, b, 0, 0, 0)),
        ]
        args += list(state)
    out_shape = (
        jax.ShapeDtypeStruct((nb, seq, D_MODEL), F32),
        jax.ShapeDtypeStruct((nb, hist, conv_dim), F32),
        jax.ShapeDtypeStruct((nb, DN_HEADS, DN_DIM, DN_DIM), F32),
    )
    out_specs = (
        x_spec,
        pl.BlockSpec((ns, hist, conv_dim), lambda b, t: (b, 0, 0)),
        pl.BlockSpec((ns, DN_HEADS, DN_DIM, DN_DIM), lambda b, t: (b, 0, 0, 0)),
    )
    if use_state:
        cbuf = pltpu.VMEM((ns, CONV_PAD + tls, conv_dim), F32)
    else:
        cbuf = pltpu.VMEM((tls, conv_dim), F32)
    scratch = [
        cbuf,
        pltpu.VMEM((rows, D_MODEL), F32),
        pltpu.VMEM((rows, D_MODEL), F32),
        pltpu.VMEM((rows, D_MODEL), F32),
        pltpu.VMEM((rows, D_MODEL), F32),
        pltpu.VMEM((rows, LANES), F32),
        pltpu.VMEM((rows, LANES), F32),
        pltpu.VMEM((rows, D_MODEL), F32),
        pltpu.VMEM((rows, D_MODEL), BF16),
        pltpu.VMEM((rows, D_MODEL), BF16),
        pltpu.VMEM((rows, D_MODEL), BF16),
        pltpu.VMEM((rows, D_MODEL), F32),
        pltpu.VMEM((rows, D_MODEL), BF16),
        pltpu.VMEM((n_stacks, MXU_DIM, MXU_DIM), BF16),
        pltpu.VMEM((ns, DN_HEADS // 2, MXU_DIM, MXU_DIM), BF16),
    ]
    if not use_state:
        scratch.append(pltpu.VMEM((hist, SUBLANES, conv_dim), F32))
        scratch.append(pltpu.VMEM((D_MODEL // LANES * tls, LANES), F32))
        scratch.append(pltpu.VMEM((D_MODEL // LANES * tls, LANES), F32))
    return pl.pallas_call(
        functools.partial(_dn_kernel, ns=ns, tls=tls, chunk=chunk, use_state=use_state),
        grid=grid, in_specs=in_specs, out_specs=out_specs, out_shape=out_shape,
        scratch_shapes=scratch,
        compiler_params=pltpu.CompilerParams(
            dimension_semantics=("arbitrary", "arbitrary"),
            vmem_limit_bytes=VMEM_LIMIT_BYTES),
        name="dn_state" if use_state else "dn_prompt",
    )(*args)


def _pad_lanes(v):
    return jnp.pad(v.astype(F32)[None, :], ((0, 0), (0, LANES - v.shape[0])))


def kernel(x_prompt, x_sample, cache_k, cache_v, state_conv, state_ssm, norm_g, final_norm_g,
           attn_w_in, attn_sinks, attn_w_out, dn_w_in, dn_conv_w, dn_a_log, dn_dt_bias,
           dn_norm_g, dn_w_out):
    attn_width = N_HEADS * HEAD_DIM
    kv_width = N_KV_HEADS * HEAD_DIM
    col_scale = jnp.where(jnp.arange(attn_w_in.shape[-1]) < attn_width, HEAD_DIM ** -0.5 * LOG2E, 1.0).astype(F32)
    a_win = (attn_w_in[0] * col_scale).astype(BF16)
    a_wout = attn_w_out[0].astype(BF16)
    g0 = norm_g[0].reshape(1, D_MODEL)
    sinks = attn_sinks[0]

    wide = 3 * DN_HEADS * DN_DIM + DN_HEADS * DN_DIM
    dw = dn_w_in[0]
    d_w1 = dw.astype(BF16)
    lane_pad = ((0, 0), (0, LANES - DN_HEADS))
    d_w2 = jnp.concatenate([jnp.pad(dw[:, wide:wide + DN_HEADS], lane_pad),
                            jnp.pad(dw[:, wide + DN_HEADS:], lane_pad)], axis=1).astype(BF16)
    d_wout = dn_w_out[0].astype(BF16)
    g1 = norm_g[1].reshape(1, D_MODEL)
    fg = final_norm_g.reshape(1, D_MODEL)
    alog = _pad_lanes(dn_a_log[0])
    dtb = _pad_lanes(dn_dt_bias[0])
    ng = dn_norm_g[0].reshape(1, DN_DIM)
    dn_args = (g1, d_w1, d_w2, dn_conv_w[0], alog, dtb, ng, d_wout, fg)

    x1p, kp, vp = _attn_layer(x_prompt, g0, a_win, a_wout, sinks, ns=1, tls=ATTN_TILE_ROWS, chunk=CHUNK)
    yp, cp, sp = _dn_layer(x1p, *dn_args, ns=1, tls=DN_TILE_ROWS, chunk=CHUNK)

    n_run, dec_seq = x_sample.shape[:2]
    ck = cache_k[0].reshape(n_run, WINDOW, kv_width)
    cv = cache_v[0].reshape(n_run, WINDOW, kv_width)
    x1s, ks, vs = _attn_layer(x_sample, g0, a_win, a_wout, sinks, ns=RUN_SEQS_PER_STEP, tls=dec_seq,
                              chunk=min(CHUNK, dec_seq), cache=(ck, cv))
    ys, cs, ss = _dn_layer(x1s, *dn_args, ns=RUN_SEQS_PER_STEP, tls=dec_seq, chunk=min(CHUNK, dec_seq),
                           state=(state_conv, state_ssm))

    def kv_out(t):
        return t.reshape(1, t.shape[0], WINDOW, N_KV_HEADS, HEAD_DIM)

    return (yp, ys, kv_out(kp), kv_out(vp), cp[None], sp[None],
            kv_out(ks), kv_out(vs), cs[None], ss[None])
```

```python
import functools
import math

import numpy as np
import jax
import jax.numpy as jnp
from jax import lax
from jax.experimental import pallas as pl
from jax.experimental.pallas import tpu as pltpu

F32 = jnp.float32
BF16 = jnp.bfloat16

D_MODEL = 1024
EPS = 1e-6
CHUNK = 64
WINDOW = 128
N_HEADS = 16
N_KV_HEADS = 4
HEAD_DIM = 64
DN_HEADS = 8
DN_DIM = 128
CONV_WIDTH = 4
CONV_PAD = 8
SUBLANES = 8
LANES = 128
MXU_DIM = 256
VMEM_LIMIT_BYTES = 56 * 1024 * 1024
ATTN_TILE_ROWS = 1024
DN_TILE_ROWS = 512
RUN_SEQS_PER_STEP = 8
ATTN_LOCKSTEP = 4
LOG2E = 1.4426950408889634

_NT = (((1,), (1,)), ((), ()))
_TN = (((0,), (0,)), ((), ()))


def _dot(a, b):
    return jnp.dot(a, b, preferred_element_type=F32)


def _dot_nt(a, b):
    return lax.dot_general(a, b, _NT, preferred_element_type=F32)


def _dot_tn(a, b):
    return lax.dot_general(a, b, _TN, preferred_element_type=F32)


def _sigmoid(x):
    return 1.0 / (1.0 + jnp.exp2(x * -LOG2E))


def _rms_scale(x):
    return lax.rsqrt(jnp.mean(x * x, axis=-1, keepdims=True) + EPS)


def _run_interleaved(*gens):
    live = list(gens)
    while live:
        for g in list(live):
            try:
                next(g)
            except StopIteration:
                live.remove(g)


def _chain(gens):
    for g in gens:
        yield from g


def _attn_kernel(*refs, ns, tls, chunk, use_cache):
    if use_cache:
        (x_ref, g_ref, win_ref, wout_ref, bias_ref, ck_ref, cv_ref,
         x1_ref, kout_ref, vout_ref,
         q_s, gate_s, klo_s, khi_s, v_s, og_s) = refs
    else:
        (x_ref, g_ref, win_ref, wout_ref, bias_ref,
         x1_ref, kout_ref, vout_ref,
         q_s, gate_s, klo_s, khi_s, v_s, og_s) = refs
    t = pl.program_id(1)
    rows = ns * tls
    keys = WINDOW + chunk
    kvn = N_KV_HEADS * HEAD_DIM
    kvw = 2 * kvn
    qw = N_HEADS * HEAD_DIM
    gw = qw // N_KV_HEADS

    half_lo = lax.broadcasted_iota(jnp.int32, (1, LANES), 1) < HEAD_DIM

    x = x_ref[...].reshape(rows, D_MODEL)
    h = (x * _rms_scale(x) * g_ref[...]).astype(BF16)
    q_s[...] = _dot(h, win_ref[:, 0:qw]).astype(BF16)
    kd = _dot(h, win_ref[:, qw:qw + kvn])
    vd = _dot(h, win_ref[:, qw + kvn:qw + 2 * kvn])

    def gate_proj():
        c0 = qw + 2 * kvn
        for s0 in range(0, qw, MXU_DIM):
            gate_s[:, s0:s0 + MXU_DIM] = _dot(h, win_ref[:, c0 + s0:c0 + s0 + MXU_DIM])
            yield

    def spread(k2, v2, nrows, row0):
        for b in range(N_KV_HEADS // 2):
            kb = k2[:, b * LANES:(b + 1) * LANES]
            vb = v2[:, b * LANES:(b + 1) * LANES]
            kr = pltpu.roll(kb, HEAD_DIM, 1)
            vr = pltpu.roll(vb, HEAD_DIM, 1)
            forms = (
                (klo_s, jnp.where(half_lo, kb, 0.0), jnp.where(half_lo, kr, 0.0)),
                (khi_s, jnp.where(half_lo, 0.0, kr), jnp.where(half_lo, 0.0, kb)),
                (v_s, jnp.where(half_lo, vb, vr), jnp.where(half_lo, vr, vb)),
            )
            for ref, even, odd in forms:
                for j, val in ((2 * b, even), (2 * b + 1, odd)):
                    ref[:, row0:row0 + nrows, j * LANES:(j + 1) * LANES] = (
                        val.astype(BF16).reshape(ns, nrows, LANES))
                yield

    if use_cache:
        _run_interleaved(spread(ck_ref[...].reshape(ns * WINDOW, kvn), cv_ref[...].reshape(ns * WINDOW, kvn),
                                WINDOW, 0))
    else:
        @pl.when(t == 0)
        def _():
            zeros = jnp.zeros((ns, WINDOW, kvw), BF16)
            klo_s[:, 0:WINDOW, :] = zeros
            khi_s[:, 0:WINDOW, :] = zeros
            v_s[:, 0:WINDOW, :] = zeros

    _run_interleaved(spread(kd, vd, tls, WINDOW), gate_proj())

    kd3 = kd.reshape(ns, tls, kvn)
    vd3 = vd.reshape(ns, tls, kvn)
    if use_cache:
        kout_ref[:, 0:WINDOW - tls, :] = ck_ref[:, tls:, :]
        vout_ref[:, 0:WINDOW - tls, :] = cv_ref[:, tls:, :]
        kout_ref[:, WINDOW - tls:, :] = kd3
        vout_ref[:, WINDOW - tls:, :] = vd3
    else:
        @pl.when(t == pl.num_programs(1) - 1)
        def _():
            kout_ref[...] = kd3[:, tls - WINDOW:, :]
            vout_ref[...] = vd3[:, tls - WINDOW:, :]

    pad = MXU_DIM - keys
    col = lax.broadcasted_iota(jnp.int32, (chunk, MXU_DIM), 1)
    pair_lo = lax.broadcasted_iota(jnp.int32, (chunk, LANES), 1) < HEAD_DIM
    ones_blk = jnp.ones((keys, LANES), BF16)
    k_pad = jnp.zeros((pad, LANES), BF16)
    v_pad = jnp.concatenate([jnp.zeros((pad, LANES), BF16), jnp.ones((pad, LANES), BF16)], axis=1)
    n_chunks = tls // chunk

    def attend(s, c):
        r0 = c * chunk
        f0 = s * tls + r0
        masked = (not use_cache) and r0 < WINDOW
        if masked:
            n_inv = jnp.maximum(0, WINDOW - (t * tls + r0))
            valid = col >= n_inv
        scores = []
        for j in range(N_KV_HEADS):
            qj = q_s[f0:f0 + chunk, j * gw:(j + 1) * gw]
            lhs = jnp.concatenate([qj[:, :LANES], qj[:, LANES:]], axis=0)
            klo = jnp.concatenate([klo_s[s, r0:r0 + keys, j * LANES:(j + 1) * LANES], k_pad], axis=0)
            khi = jnp.concatenate([khi_s[s, r0:r0 + keys, j * LANES:(j + 1) * LANES], k_pad], axis=0)
            s_even = _dot_nt(lhs, klo)
            s_odd = _dot_nt(lhs, khi)
            scores.append((s_even[:chunk], s_odd[:chunk], s_even[chunk:], s_odd[chunk:]))
        yield
        ovs = []
        for j in range(N_KV_HEADS):
            ps = []
            for g in range(4):
                sg = scores[j][g] + bias_ref[4 * j + g]
                if masked:
                    sg = jnp.where(valid, sg, -jnp.inf)
                m = jnp.max(sg, axis=-1, keepdims=True)
                ps.append(jnp.exp2(sg - m).astype(BF16))
            p_all = jnp.concatenate(ps, axis=0)
            vj = v_s[s, r0:r0 + keys, j * LANES:(j + 1) * LANES]
            vaug = jnp.concatenate([jnp.concatenate([vj, ones_blk], axis=1), v_pad], axis=0)
            ovs.append(_dot(p_all, vaug))
        yield
        for j in range(N_KV_HEADS):
            outs = []
            for g in range(4):
                blk = ovs[j][g * chunk:(g + 1) * chunk]
                outs.append(blk[:, :LANES] / blk[:, LANES:])
            o01 = jnp.where(pair_lo, outs[0], outs[1])
            o23 = jnp.where(pair_lo, outs[2], outs[3])
            gt = gate_s[f0:f0 + chunk, j * gw:(j + 1) * gw]
            og = jnp.concatenate([o01, o23], axis=1) * (gt * _sigmoid(gt))
            og_s[f0:f0 + chunk, j * gw:(j + 1) * gw] = og.astype(BF16)
        yield

    units = [(s, c) for s in range(ns) for c in range(n_chunks)]
    def out_proj(f0, f1):
        y = _dot(og_s[f0:f1, :], wout_ref[...])
        if ns == 1:
            x1_ref[0, f0:f1, :] = x_ref[0, f0:f1, :] + y
        else:
            x1_ref[f0 // tls:f1 // tls] = x_ref[f0 // tls:f1 // tls] + y.reshape((f1 - f0) // tls, tls, D_MODEL)
        yield

    grows = ATTN_LOCKSTEP * chunk
    pending = []
    for i in range(0, len(units), ATTN_LOCKSTEP):
        _run_interleaved(*[attend(s, c) for s, c in units[i:i + ATTN_LOCKSTEP]], *pending)
        pending = [out_proj(i * chunk, min(rows, i * chunk + grows))]

    if not use_cache:
        klo_s[:, 0:WINDOW, :] = klo_s[:, tls:tls + WINDOW, :]
        khi_s[:, 0:WINDOW, :] = khi_s[:, tls:tls + WINDOW, :]
        v_s[:, 0:WINDOW, :] = v_s[:, tls:tls + WINDOW, :]

    _run_interleaved(*pending)


def _attn_bias(chunk, sinks):
    keys = WINDOW + chunk
    slopes = (2.0 ** (-8.0 * np.arange(1, N_HEADS + 1, dtype=np.float32) / N_HEADS)).astype(np.float32)
    i = np.arange(chunk, dtype=np.float32)[:, None]
    j = np.arange(keys, dtype=np.float32)[None, :]
    dist = np.abs(i + WINDOW - j).astype(np.float32)
    alibi = jnp.asarray(-(slopes[:, None, None] * dist[None]), F32) * LOG2E
    sink_col = jnp.broadcast_to((sinks.astype(F32) * LOG2E)[:, None, None], (N_HEADS, chunk, 1))
    rest = jnp.full((N_HEADS, chunk, MXU_DIM - keys - 1), -jnp.inf, F32)
    return jnp.concatenate([alibi, sink_col, rest], axis=2)


def _attn_layer(x, g, w_in_prep, w_out, sinks, *, ns, tls, chunk, cache=None):
    nb, seq, _ = x.shape
    use_cache = cache is not None
    grid = (nb // ns, seq // tls)
    kvn = N_KV_HEADS * HEAD_DIM
    kvw = 2 * kvn
    rows = ns * tls
    const2 = lambda b, t: (0, 0)
    in_specs = [
        pl.BlockSpec((ns, tls, D_MODEL), lambda b, t: (b, t, 0)),
        pl.BlockSpec((1, D_MODEL), const2),
        pl.BlockSpec(w_in_prep.shape, const2, pipeline_mode=pl.Buffered(1)),
        pl.BlockSpec(w_out.shape, const2, pipeline_mode=pl.Buffered(1)),
        pl.BlockSpec((N_HEADS, chunk, MXU_DIM), lambda b, t: (0, 0, 0), pipeline_mode=pl.Buffered(1)),
    ]
    args = [x, g, w_in_prep, w_out, _attn_bias(chunk, sinks)]
    if use_cache:
        in_specs += [pl.BlockSpec((ns, WINDOW, kvn), lambda b, t: (b, 0, 0))] * 2
        args += list(cache)
    out_shape = (
        jax.ShapeDtypeStruct((nb, seq, D_MODEL), F32),
        jax.ShapeDtypeStruct((nb, WINDOW, kvn), F32),
        jax.ShapeDtypeStruct((nb, WINDOW, kvn), F32),
    )
    out_specs = (
        pl.BlockSpec((ns, tls, D_MODEL), lambda b, t: (b, t, 0)),
        pl.BlockSpec((ns, WINDOW, kvn), lambda b, t: (b, 0, 0)),
        pl.BlockSpec((ns, WINDOW, kvn), lambda b, t: (b, 0, 0)),
    )
    scratch = [
        pltpu.VMEM((rows, D_MODEL), BF16),
        pltpu.VMEM((rows, D_MODEL), F32),
        pltpu.VMEM((ns, WINDOW + tls, kvw), BF16),
        pltpu.VMEM((ns, WINDOW + tls, kvw), BF16),
        pltpu.VMEM((ns, WINDOW + tls, kvw), BF16),
        pltpu.VMEM((rows, D_MODEL), BF16),
    ]
    return pl.pallas_call(
        functools.partial(_attn_kernel, ns=ns, tls=tls, chunk=chunk, use_cache=use_cache),
        grid=grid, in_specs=in_specs, out_specs=out_specs, out_shape=out_shape,
        scratch_shapes=scratch,
        compiler_params=pltpu.CompilerParams(
            dimension_semantics=("arbitrary", "arbitrary"),
            vmem_limit_bytes=VMEM_LIMIT_BYTES),
        name="attn_cache" if use_cache else "attn_prompt",
    )(*args)


def _split3(x):
    hi = x.astype(BF16)
    r1 = x - hi.astype(F32)
    mid = r1.astype(BF16)
    lo = (r1 - mid.astype(F32)).astype(BF16)
    return hi, mid, lo


def _dn_kernel(*refs, ns, tls, chunk, use_state):
    rm = not use_state
    if use_state:
        (x_ref, g_ref, w1_ref, w2_ref, cw_ref, alog_ref, dtb_ref, ng_ref, wout_ref, fg_ref,
         tri_ref, cst_ref, sst_ref,
         y_ref, cout_ref, s_ref,
         cbuf, qn_s, kn_s, v_s, z_s, bt_s, gc_s, o_s, on_s,
         qg_s, kd_s, u_s, w_s, intra_s, sb_s) = refs
    else:
        (x_ref, g_ref, w1_ref, w2_ref, cw_ref, alog_ref, dtb_ref, ng_ref, wout_ref, fg_ref,
         tri_ref,
         y_ref, cout_ref, s_ref,
         cbuf, qn_s, kn_s, v_s, z_s, bt_s, gc_s, o_s, on_s,
         qg_s, kd_s, u_s, w_s, intra_s, sb_s, hist_s, xs_s, ys_s) = refs
        assert ns == 1 and chunk == SUBLANES * SUBLANES
    t = pl.program_id(1)
    rows = ns * tls
    hist = CONV_WIDTH - 1
    qk_w = DN_HEADS * DN_DIM
    conv_w = 3 * qk_w
    n_pairs = DN_HEADS // 2
    rres = tls // SUBLANES

    if use_state:
        s_ref[...] = sst_ref[...]
        cbuf[:, CONV_PAD - hist:CONV_PAD, :] = cst_ref[...]
        sb_s[...] = jnp.zeros(sb_s.shape, BF16)
        for s in range(ns):
            for hd in range(DN_HEADS):
                d0 = (hd % 2) * DN_DIM
                sb_s[s, hd // 2, d0:d0 + DN_DIM, d0:d0 + DN_DIM] = sst_ref[s, hd].astype(BF16)
    else:
        @pl.when(t == 0)
        def _():
            s_ref[...] = jnp.zeros(s_ref.shape, F32)
            sb_s[...] = jnp.zeros(sb_s.shape, BF16)
            hist_s[...] = jnp.zeros(hist_s.shape, F32)

    n_slabs = D_MODEL // LANES
    if rm:
        for j in range(n_slabs):
            xs_s[j * tls:(j + 1) * tls, :] = x_ref[0, :, j * LANES:(j + 1) * LANES]
        x = jnp.concatenate(
            [jnp.concatenate([xs_s[pl.ds(j * tls + r, rres, stride=SUBLANES), :] for j in range(n_slabs)], axis=1)
             for r in range(SUBLANES)], axis=0)
    else:
        x = x_ref[...].reshape(rows, D_MODEL)
    h = (x * _rms_scale(x) * g_ref[...]).astype(BF16)

    def phase_a():
        ba = _dot(h, w2_ref[...])
        bt_s[...] = _sigmoid(ba[:, :LANES])
        a = ba[:, LANES:] + dtb_ref[...]
        softplus = jnp.maximum(a, 0.0) + jnp.log1p(jnp.exp(-jnp.abs(a)))
        g = -jnp.exp(alog_ref[...]) * softplus
        g_hi, g_mid, g_lo = _split3(g)
        gcs = _dot(tri_ref[...], jnp.concatenate([g_hi, g_mid, g_lo], axis=1))
        gc_s[...] = gcs[:, :LANES] + gcs[:, LANES:2 * LANES] + gcs[:, 2 * LANES:]
        yield
        for c0 in range(0, conv_w, MXU_DIM):
            raw = _dot(h, w1_ref[:, c0:c0 + MXU_DIM])
            if rm:
                cbuf[:, c0:c0 + MXU_DIM] = raw
            else:
                cbuf[:, CONV_PAD:, c0:c0 + MXU_DIM] = raw.reshape(ns, tls, MXU_DIM)
            yield
        for c0 in range(0, qk_w, MXU_DIM):
            z_s[:, c0:c0 + MXU_DIM] = _dot(h, w1_ref[:, conv_w + c0:conv_w + c0 + MXU_DIM])
            yield

    def conv_block(s, cols):
        if not rm:
            acc = None
            for j in range(CONV_WIDTH):
                lo = CONV_PAD - hist + j
                term = cbuf[s, lo:lo + tls, cols] * cw_ref[j:j + 1, cols]
                acc = term if acc is None else acc + term
            return acc
        xb = cbuf[:, cols]
        wrapped = []
        for k in range(hist):
            r = SUBLANES - hist + k
            full = jnp.concatenate([hist_s[k, :, cols], xb[r * rres:(r + 1) * rres]], axis=0)
            wrapped.append(full[SUBLANES - 1:SUBLANES - 1 + rres])
        ext = jnp.concatenate(wrapped + [xb], axis=0)
        acc = None
        for j in range(CONV_WIDTH):
            term = ext[j * rres:j * rres + tls] * cw_ref[j:j + 1, cols]
            acc = term if acc is None else acc + term
        return acc

    def phase_b():
        for blk in range(conv_w // LANES):
            cols = slice(blk * LANES, (blk + 1) * LANES)
            seg, hb = divmod(blk, DN_HEADS)
            dst = slice(hb * LANES, (hb + 1) * LANES)
            for s in range(ns):
                acc = conv_block(s, cols)
                yv = acc * _sigmoid(acc)
                rs = slice(s * tls, (s + 1) * tls)
                if seg == 2:
                    v_s[rs, dst] = yv
                else:
                    inv = lax.rsqrt(jnp.sum(yv * yv, axis=-1, keepdims=True) + EPS)
                    if seg == 0:
                        qn_s[rs, dst] = yv * (inv * (DN_DIM ** -0.5))
                    else:
                        kn_s[rs, dst] = yv * inv
            if blk % 2 == 1:
                yield

    ga, gb = phase_a(), phase_b()
    for _ in range(1 + qk_w // MXU_DIM):
        next(ga)
    for _ in range(2 * DN_HEADS * LANES // (2 * LANES)):
        next(gb)
        next(ga)

    def save_conv_tail():
        if rm:
            for k in range(hist):
                r = SUBLANES - hist + k
                last = cbuf[(r + 1) * rres - SUBLANES:(r + 1) * rres, :]
                hist_s[k] = last
                cout_ref[0, k:k + 1, :] = last[SUBLANES - 1:SUBLANES]
        else:
            cout_ref[...] = cbuf[:, CONV_PAD + tls - hist:CONV_PAD + tls, :]

    hg = MXU_DIM // chunk
    n_groups = DN_HEADS // hg
    n_levels = int(round(math.log2(chunk)))
    ri = lax.broadcasted_iota(jnp.int32, (MXU_DIM, MXU_DIM), 0)
    ci = lax.broadcasted_iota(jnp.int32, (MXU_DIM, MXU_DIM), 1)

    def time_of(pos):
        p = pos % chunk
        return SUBLANES * (p % SUBLANES) + p // SUBLANES if rm else p

    tri_mask = (ri // chunk == ci // chunk) & (time_of(ri) >= time_of(ci))
    diag = ri == ci
    eye = jnp.where(diag, 1.0, 0.0).astype(F32)
    lane = lax.broadcasted_iota(jnp.int32, (MXU_DIM, LANES), 1)
    ones_rows = jnp.ones((2 * SUBLANES, LANES), BF16)
    n_chunks = tls // chunk

    def hcols(hd):
        return slice(hd * LANES, (hd + 1) * LANES)

    def crows(ref, s, c, cols):
        if rm:
            return jnp.concatenate(
                [ref[r * rres + SUBLANES * c:r * rres + SUBLANES * (c + 1), cols] for r in range(SUBLANES)], axis=0)
        f0 = s * tls + c * chunk
        return ref[f0:f0 + chunk, cols]

    def crow_last(ref, s, c, cols):
        r = (SUBLANES - 1) * rres + SUBLANES * c + SUBLANES - 1 if rm else s * tls + (c + 1) * chunk - 1
        return ref[r:r + 1, cols]

    def phase_c(groups):
        st = []
        for (s, c, grp) in groups:
            f0 = s * tls + c * chunk
            rsl = slice(f0, f0 + chunk)
            heads = list(range(grp * hg, (grp + 1) * hg))
            k_h, q_h, gc_cols, beta_cols = [], [], [], []
            for hd in heads:
                beta_h = crows(bt_s, s, c, slice(hd, hd + 1))
                gc_h = crows(gc_s, s, c, slice(hd, hd + 1))
                gl_h = crow_last(gc_s, s, c, slice(hd, hd + 1))
                kf = crows(kn_s, s, c, hcols(hd))
                qf = crows(qn_s, s, c, hcols(hd))
                qg_s[rsl, hcols(hd)] = (qf * jnp.exp(gc_h)).astype(BF16)
                kd_s[rsl, hcols(hd)] = (kf * jnp.exp(gl_h - gc_h)).astype(BF16)
                k_h.append(kf.astype(BF16))
                q_h.append(qf.astype(BF16))
                gc_cols.append(gc_h)
                beta_cols.append(beta_h)
            gc_c = jnp.concatenate(gc_cols, axis=0)
            hi, mid, lo = (p.astype(F32) for p in _split3(gc_c))
            rmat = jnp.where(lane == 0, hi, jnp.where(lane == 1, mid, jnp.where(lane == 2, lo, 0.0)))
            gc_row = _dot_nt(ones_rows, rmat.astype(BF16))[0:1]
            k_st = jnp.concatenate(k_h, axis=0)
            q_st = jnp.concatenate(q_h, axis=0)
            st.append(dict(
                rsl=rsl, heads=heads, sc=(s, c), gi=(s * n_chunks + c) * n_groups + grp,
                beta=jnp.concatenate(beta_cols, axis=0),
                gd=gc_c - gc_row,
                kq=_dot_nt(jnp.concatenate([k_st, q_st], axis=0), k_st)))
        yield
        for d in st:
            decay = jnp.exp(jnp.where(tri_mask, d.pop('gd'), -jnp.inf))
            kq = d.pop('kq')
            m = jnp.where(diag, 0.0, kq[:MXU_DIM] * decay * d.pop('beta'))
            intra_s[d['gi']] = (kq[MXU_DIM:] * decay).astype(BF16)
            mb = m.astype(BF16)
            d['p'] = eye - m
            d['q'] = _dot(mb, mb).astype(BF16)
        yield
        for lvl in range(1, n_levels):
            for d in st:
                qb = d['q']
                if lvl < n_levels - 1:
                    pq = _dot(jnp.concatenate([d['p'].astype(BF16), qb], axis=0), qb)
                    d['p'] = d['p'] + pq[:MXU_DIM]
                    d['q'] = pq[MXU_DIM:].astype(BF16)
                else:
                    d['p'] = d['p'] + _dot(d['p'].astype(BF16), qb)
            yield
        for d in st:
            rhs_rows = []
            for hd in d['heads']:
                beta_h = crows(bt_s, *d['sc'], slice(hd, hd + 1))
                gc_h = crows(gc_s, *d['sc'], slice(hd, hd + 1))
                vb = (crows(v_s, *d['sc'], hcols(hd)) * beta_h).astype(BF16)
                kbg = (crows(kn_s, *d['sc'], hcols(hd)) * (beta_h * jnp.exp(gc_h))).astype(BF16)
                rhs_rows.append(jnp.concatenate([vb, kbg], axis=1))
            uw = _dot(d['p'].astype(BF16), jnp.concatenate(rhs_rows, axis=0))
            for i, hd in enumerate(d['heads']):
                hs = slice(i * chunk, (i + 1) * chunk)
                u_s[d['rsl'], hcols(hd)] = uw[hs, :LANES]
                w_s[d['rsl'], hcols(hd)] = uw[hs, LANES:].astype(BF16)
        yield

    def phase_d(s, c):
        f0 = s * tls + c * chunk
        rsl = slice(f0, f0 + chunk)
        both = []
        for p in range(n_pairs):
            pc = slice(p * MXU_DIM, (p + 1) * MXU_DIM)
            lhs = jnp.concatenate([w_s[rsl, pc], qg_s[rsl, pc]], axis=0)
            both.append(_dot(lhs, sb_s[s, p]))
        yield
        vnb, outer = [], []
        for p in range(n_pairs):
            pc = slice(p * MXU_DIM, (p + 1) * MXU_DIM)
            v_new = (u_s[rsl, pc] - both[p][:chunk]).astype(BF16)
            for i in range(2):
                hd = 2 * p + i
                vh = v_new[:, i * LANES:(i + 1) * LANES]
                vnb.append(vh)
                outer.append(_dot_tn(kd_s[rsl, hcols(hd)], vh))
        for grp in range(n_groups):
            heads = list(range(grp * hg, (grp + 1) * hg))
            v_st = jnp.concatenate([vnb[hd] for hd in heads], axis=0)
            qs_st = jnp.concatenate(
                [both[hd // 2][chunk:, (hd % 2) * LANES:(hd % 2 + 1) * LANES] for hd in heads], axis=0)
            o_st = qs_st + _dot(intra_s[(s * n_chunks + c) * n_groups + grp], v_st)
            for i, hd in enumerate(heads):
                o_s[rsl, hcols(hd)] = o_st[i * chunk:(i + 1) * chunk]
        yield
        for hd in range(DN_HEADS):
            gl = crow_last(gc_s, s, c, slice(hd, hd + 1))
            s_new = s_ref[s, hd] * jnp.exp(gl) + outer[hd]
            s_ref[s, hd] = s_new
            d0 = (hd % 2) * DN_DIM
            sb_s[s, hd // 2, d0:d0 + DN_DIM, d0:d0 + DN_DIM] = s_new.astype(BF16)
        yield

    def phase_e(r0, r1):
        rsl = slice(r0, r1)
        for hd in range(DN_HEADS):
            o = o_s[rsl, hcols(hd)]
            if rm:
                zz = jnp.concatenate([crows(z_s, 0, c, hcols(hd)) for c in range(r0 // chunk, r1 // chunk)], axis=0)
            else:
                zz = z_s[rsl, hcols(hd)]
            on_s[rsl, hcols(hd)] = (o * _rms_scale(o) * ng_ref[...] * (zz * _sigmoid(zz))).astype(BF16)
            yield
        y1 = _dot(on_s[rsl, :], wout_ref[...])
        if rm:
            for i, c in enumerate(range(r0 // chunk, r1 // chunk)):
                for r in range(SUBLANES):
                    p0 = i * chunk + r * SUBLANES
                    for j in range(n_slabs):
                        ys_s[pl.ds(j * tls + c * chunk + r, SUBLANES, stride=SUBLANES), :] = (
                            y1[p0:p0 + SUBLANES, j * LANES:(j + 1) * LANES])
            y1 = jnp.concatenate([ys_s[j * tls + r0:j * tls + r1, :] for j in range(n_slabs)], axis=1)
            x2 = x_ref[0, r0:r1, :] + y1
            y_ref[0, r0:r1, :] = x2 * _rms_scale(x2) * fg_ref[...]
        else:
            x2 = x_ref[r0 // tls:r1 // tls].reshape(r1 - r0, D_MODEL) + y1
            yv = x2 * _rms_scale(x2) * fg_ref[...]
            y_ref[r0 // tls:r1 // tls] = yv.reshape((r1 - r0) // tls, tls, D_MODEL)
        yield

    units = [(s, c) for s in range(ns) for c in range(n_chunks)]
    upb = max(1, 4 // n_groups)
    batches = [units[i:i + upb] for i in range(0, len(units), upb)]
    brows = upb * chunk

    def c_of(batch):
        return phase_c([(s, c, grp) for (s, c) in batch for grp in range(n_groups)])

    def d_of(batch):
        if use_state:
            return [phase_d(s, c) for (s, c) in batch]
        return [_chain([phase_d(s, c) for (s, c) in batch])]

    _run_interleaved(c_of(batches[0]), gb, ga)
    save_conv_tail()
    for i in range(1, len(batches)):
        extra = [phase_e((i - 2) * brows, (i - 1) * brows)] if i >= 2 else []
        _run_interleaved(c_of(batches[i]), *d_of(batches[i - 1]), *extra)
    last = len(batches) - 1
    extra = [phase_e((last - 1) * brows, last * brows)] if last >= 1 else []
    _run_interleaved(*d_of(batches[last]), *extra)
    _run_interleaved(phase_e(last * brows, (last + 1) * brows))


def _block_tri(rows, chunk, residue_major):
    q = np.arange(rows)
    tm = SUBLANES * (q % (rows // SUBLANES)) + q // (rows // SUBLANES) if residue_major else q
    m = (tm[:, None] // chunk == tm[None, :] // chunk) & (tm[:, None] >= tm[None, :])
    return jnp.asarray(m.astype(np.float32), BF16)


def _dn_layer(x, g, w1, w2, conv_w, alog, dtb, ng, w_out, fg, *, ns, tls, chunk, state=None):
    nb, seq, _ = x.shape
    use_state = state is not None
    grid = (nb // ns, seq // tls)
    rows = ns * tls
    hist = CONV_WIDTH - 1
    conv_dim = 3 * DN_HEADS * DN_DIM
    n_stacks = rows * DN_HEADS // MXU_DIM
    const2 = lambda b, t: (0, 0)
    tri = _block_tri(rows, chunk, not use_state)
    args = [x, g, w1, w2, conv_w, alog, dtb, ng, w_out, fg, tri]
    x_spec = pl.BlockSpec((ns, tls, D_MODEL), lambda b, t: (b, t, 0))
    in_specs = [x_spec]
    in_specs += [pl.BlockSpec(a.shape, const2, pipeline_mode=pl.Buffered(1)) for a in args[1:]]
    if use_state:
        in_specs += [
            pl.BlockSpec((None, ns, hist, conv_dim), lambda b, t: (0, b, 0, 0)),
            pl.BlockSpec((None, ns, DN_HEADS, DN_DIM, DN_DIM), lambda b, t: (0, b, 0, 0, 0)),
        ]
        args += list(state)
    out_shape = (
        jax.ShapeDtypeStruct((nb, seq, D_MODEL), F32),
        jax.ShapeDtypeStruct((nb, hist, conv_dim), F32),
        jax.ShapeDtypeStruct((nb, DN_HEADS, DN_DIM, DN_DIM), F32),
    )
    out_specs = (
        x_spec,
        pl.BlockSpec((ns, hist, conv_dim), lambda b, t: (b, 0, 0)),
        pl.BlockSpec((ns, DN_HEADS, DN_DIM, DN_DIM), lambda b, t: (b, 0, 0, 0)),
    )
    if use_state:
        cbuf = pltpu.VMEM((ns, CONV_PAD + tls, conv_dim), F32)
    else:
        cbuf = pltpu.VMEM((tls, conv_dim), F32)
    scratch = [
        cbuf,
        pltpu.VMEM((rows, D_MODEL), F32),
        pltpu.VMEM((rows, D_MODEL), F32),
        pltpu.VMEM((rows, D_MODEL), F32),
        pltpu.VMEM((rows, D_MODEL), F32),
        pltpu.VMEM((rows, LANES), F32),
        pltpu.VMEM((rows, LANES), F32),
        pltpu.VMEM((rows, D_MODEL), F32),
        pltpu.VMEM((rows, D_MODEL), BF16),
        pltpu.VMEM((rows, D_MODEL), BF16),
        pltpu.VMEM((rows, D_MODEL), BF16),
        pltpu.VMEM((rows, D_MODEL), F32),
        pltpu.VMEM((rows, D_MODEL), BF16),
        pltpu.VMEM((n_stacks, MXU_DIM, MXU_DIM), BF16),
        pltpu.VMEM((ns, DN_HEADS // 2, MXU_DIM, MXU_DIM), BF16),
    ]
    if not use_state:
        scratch.append(pltpu.VMEM((hist, SUBLANES, conv_dim), F32))
        scratch.append(pltpu.VMEM((D_MODEL // LANES * tls, LANES), F32))
        scratch.append(pltpu.VMEM((D_MODEL // LANES * tls, LANES), F32))
    return pl.pallas_call(
        functools.partial(_dn_kernel, ns=ns, tls=tls, chunk=chunk, use_state=use_state),
        grid=grid, in_specs=in_specs, out_specs=out_specs, out_shape=out_shape,
        scratch_shapes=scratch,
        compiler_params=pltpu.CompilerParams(
            dimension_semantics=("arbitrary", "arbitrary"),
            vmem_limit_bytes=VMEM_LIMIT_BYTES),
        name="dn_state" if use_state else "dn_prompt",
    )(*args)


def _pad_lanes(v):
    return jnp.pad(v.astype(F32)[None, :], ((0, 0), (0, LANES - v.shape[0])))


def kernel(x_prompt, x_sample, cache_k, cache_v, state_conv, state_ssm, norm_g, final_norm_g,
           attn_w_in, attn_sinks, attn_w_out, dn_w_in, dn_conv_w, dn_a_log, dn_dt_bias,
           dn_norm_g, dn_w_out):
    attn_width = N_HEADS * HEAD_DIM
    kv_width = N_KV_HEADS * HEAD_DIM
    col_scale = jnp.where(jnp.arange(attn_w_in.shape[-1]) < attn_width, HEAD_DIM ** -0.5 * LOG2E, 1.0).astype(F32)
    a_win = (attn_w_in[0] * col_scale).astype(BF16)
    a_wout = attn_w_out[0].astype(BF16)
    g0 = norm_g[0].reshape(1, D_MODEL)
    sinks = attn_sinks[0]

    wide = 3 * DN_HEADS * DN_DIM + DN_HEADS * DN_DIM
    dw = dn_w_in[0]
    d_w1 = dw.astype(BF16)
    lane_pad = ((0, 0), (0, LANES - DN_HEADS))
    d_w2 = jnp.concatenate([jnp.pad(dw[:, wide:wide + DN_HEADS], lane_pad),
                            jnp.pad(dw[:, wide + DN_HEADS:], lane_pad)], axis=1).astype(BF16)
    d_wout = dn_w_out[0].astype(BF16)
    g1 = norm_g[1].reshape(1, D_MODEL)
    fg = final_norm_g.reshape(1, D_MODEL)
    alog = _pad_lanes(dn_a_log[0])
    dtb = _pad_lanes(dn_dt_bias[0])
    ng = dn_norm_g[0].reshape(1, DN_DIM)
    dn_args = (g1, d_w1, d_w2, dn_conv_w[0], alog, dtb, ng, d_wout, fg)

    x1p, kp, vp = _attn_layer(x_prompt, g0, a_win, a_wout, sinks, ns=1, tls=ATTN_TILE_ROWS, chunk=CHUNK)
    yp, cp, sp = _dn_layer(x1p, *dn_args, ns=1, tls=DN_TILE_ROWS, chunk=CHUNK)

    n_run, dec_seq = x_sample.shape[:2]
    ck = cache_k[0].reshape(n_run, WINDOW, kv_width)
    cv = cache_v[0].reshape(n_run, WINDOW, kv_width)
    x1s, ks, vs = _attn_layer(x_sample, g0, a_win, a_wout, sinks, ns=RUN_SEQS_PER_STEP, tls=dec_seq,
                              chunk=min(CHUNK, dec_seq), cache=(ck, cv))
    ys, cs, ss = _dn_layer(x1s, *dn_args, ns=RUN_SEQS_PER_STEP, tls=dec_seq, chunk=min(CHUNK, dec_seq),
                           state=(state_conv, state_ssm))

    def kv_out(t):
        return t.reshape(1, t.shape[0], WINDOW, N_KV_HEADS, HEAD_DIM)

    return (yp, ys, kv_out(kp), kv_out(vp), cp[None], sp[None],
            kv_out(ks), kv_out(vs), cs[None], ss[None])
```
